```python
import math
import jax
import jax.numpy as jnp
from jax import lax
import numpy as np

D_MODEL = 1024
BATCH = 8
SEQ = 2048
DEPTH = 2

N_EVEN = (DEPTH + 1) // 2
N_ODD = DEPTH // 2
NORM_EPS = 1e-5

SSD_HEADS = 16
SSD_HEAD_DIM = 64
SSD_INNER = SSD_HEADS * SSD_HEAD_DIM
SSD_GROUPS = 2
SSD_STATE = 128
SSD_CONV = 4
SSD_CHUNK = 128
SSD_CONV_DIM = SSD_INNER + 2 * SSD_GROUPS * SSD_STATE
DT_MIN = 1e-3
DT_MAX = 1e-1

MOBA_HEADS = 8
MOBA_HEAD_DIM = 128
MOBA_INNER = MOBA_HEADS * MOBA_HEAD_DIM
MOBA_BLOCK = 256
MOBA_TOPK = 3
MOBA_QCHUNK = 32

Z_END = SSD_INNER
XBC_END = Z_END + SSD_CONV_DIM
DT_END = XBC_END + SSD_HEADS
Q_END = DT_END + MOBA_INNER
K_END = Q_END + MOBA_INNER
V_END = K_END + MOBA_INNER
EVEN_IN = V_END + MOBA_INNER
EVEN_MIX = SSD_INNER + MOBA_INNER

DIL_HEADS = 16
DIL_HEAD_DIM = 128
DIL_INNER = DIL_HEADS * DIL_HEAD_DIM
DIL_PATTERNS = ((128, 1), (512, 4), (2048, 16))
DIL_QBLOCK = 128
ODD_IN = 4 * DIL_INNER

kernel_name = 'hybrid_ssd_moba_dilated'


def rmsnorm(x, g):
    xf = x.astype(jnp.float32)
    y = xf * lax.rsqrt(jnp.mean(xf * xf, axis=-1, keepdims=True) + NORM_EPS)
    return (y * g.astype(jnp.float32)).astype(x.dtype)


def causal_dwconv(u, w, b):
    c = u.shape[-1]
    y = lax.conv_general_dilated(u, w[:, None, :].astype(u.dtype), window_strides=(1,),
                                 padding=[(w.shape[0] - 1, 0)],
                                 dimension_numbers=('NWC', 'WIO', 'NWC'),
                                 feature_group_count=c)
    return y + b.astype(u.dtype)


def ssd_chunked(x, dt, a, bmat, cmat, d_skip):
    f32 = jnp.float32
    bsz, s, h, p = x.shape
    g, n = bmat.shape[2], bmat.shape[3]
    l = SSD_CHUNK
    nc = s // l
    xf = x.astype(f32)
    dtf = dt.astype(f32)
    bh = jnp.repeat(bmat.astype(f32), h // g, axis=2).reshape(bsz, nc, l, h, n)
    ch = jnp.repeat(cmat.astype(f32), h // g, axis=2).reshape(bsz, nc, l, h, n)
    xc = (xf * dtf[..., None]).reshape(bsz, nc, l, h, p)
    ac = (dtf * a.astype(f32)).reshape(bsz, nc, l, h).transpose(0, 3, 1, 2)
    a_cum = jnp.cumsum(ac, axis=-1)
    seg = a_cum[..., :, None] - a_cum[..., None, :]
    causal = jnp.tril(jnp.ones((l, l), dtype=bool))
    lmat = jnp.exp(jnp.where(causal, seg, -jnp.inf))
    scores = jnp.einsum('bclhn,bcshn->bhcls', ch, bh)
    y_diag = jnp.einsum('bhcls,bcshp->bclhp', scores * lmat, xc)
    decay_states = jnp.exp(a_cum[..., -1:] - a_cum)
    states = jnp.einsum('bclhn,bhcl,bclhp->bchpn', bh, decay_states, xc)
    chunk_decay = jnp.exp(a_cum[..., -1])

    def step(carry, inp):
        st, dec = inp
        return carry * dec[..., None, None] + st, carry

    init = jnp.zeros((bsz, h, p, n), f32)
    _, prev = lax.scan(step, init, (states.transpose(1, 0, 2, 3, 4), chunk_decay.transpose(2, 0, 1)))
    prev = prev.transpose(1, 0, 2, 3, 4)
    y_off = jnp.einsum('bclhn,bchpn,bhcl->bclhp', ch, prev, jnp.exp(a_cum))
    y = (y_diag + y_off).reshape(bsz, s, h, p) + xf * d_skip.astype(f32)[:, None]
    return y.astype(x.dtype)


def gated_group_rmsnorm(y, z, g):
    bsz, s, _ = y.shape
    u = (y * jax.nn.silu(z)).astype(jnp.float32).reshape(bsz, s, SSD_GROUPS, -1)
    u = u * lax.rsqrt(jnp.mean(u * u, axis=-1, keepdims=True) + NORM_EPS)
    return (u.reshape(bsz, s, -1) * g.astype(jnp.float32)).astype(y.dtype)


def moba_attention(q, k, v):
    f32 = jnp.float32
    bsz, h, s, d = q.shape
    blk = MOBA_BLOCK
    nb = -(-s // blk)
    pad = nb * blk - s
    kb = jnp.pad(k, ((0, 0), (0, 0), (0, pad), (0, 0))).reshape(bsz, h, nb, blk, d)
    vb = jnp.pad(v, ((0, 0), (0, 0), (0, pad), (0, 0))).reshape(bsz, h, nb, blk, d)
    scale = d ** -0.5
    kmean = jnp.mean(kb.astype(f32), axis=3)
    gate = jnp.einsum('bhsd,bhnd->bhsn', q.astype(f32), kmean)
    qblk = jnp.arange(s) // blk
    past = jnp.arange(nb)[None, :] < qblk[:, None]
    gate = jnp.where(past, gate, -jnp.inf)
    k_sel = max(1, min(MOBA_TOPK, nb - 1))
    _, sel = lax.top_k(gate, k_sel)
    valid = sel < qblk[:, None]
    qc = MOBA_QCHUNK
    nq = s // qc
    qs = q.reshape(bsz, h, nq, qc, d).transpose(2, 0, 1, 3, 4)
    sels = sel.reshape(bsz, h, nq, qc, k_sel).transpose(2, 0, 1, 3, 4)
    valids = valid.reshape(bsz, h, nq, qc, k_sel).transpose(2, 0, 1, 3, 4)
    poss = jnp.arange(s, dtype=jnp.int32).reshape(nq, qc)
    bi = jnp.arange(bsz)[:, None, None, None]
    hi = jnp.arange(h)[None, :, None, None]
    nsel = k_sel * blk

    def one_chunk(args):
        qq, sc, vc, pos = args
        own = pos[0] // blk
        k_own = lax.dynamic_index_in_dim(kb, own, axis=2, keepdims=False)
        v_own = lax.dynamic_index_in_dim(vb, own, axis=2, keepdims=False)
        k_g = kb[bi, hi, sc]
        v_g = vb[bi, hi, sc]
        s_sel = jnp.einsum('bhqd,bhqjkd->bhqjk', qq, k_g).astype(f32) * scale
        s_sel = jnp.where(vc[..., None], s_sel, -jnp.inf)
        key_pos = own * blk + jnp.arange(blk)
        s_own = jnp.einsum('bhqd,bhkd->bhqk', qq, k_own).astype(f32) * scale
        s_own = jnp.where(key_pos[None, :] <= pos[:, None], s_own, -jnp.inf)
        logits = jnp.concatenate([s_sel.reshape(bsz, h, qc, nsel), s_own], axis=-1)
        pr = jax.nn.softmax(logits, axis=-1)
        p_sel = pr[..., :nsel].reshape(bsz, h, qc, k_sel, blk).astype(v.dtype)
        p_own = pr[..., nsel:].astype(v.dtype)
        return (jnp.einsum('bhqjk,bhqjkd->bhqd', p_sel, v_g)
                + jnp.einsum('bhqk,bhkd->bhqd', p_own, v_own))

    out = lax.map(one_chunk, (qs, sels, valids, poss))
    return out.transpose(1, 0, 3, 2, 4).reshape(bsz, s, h * d)


def dilated_attention(q, k, v):
    f32 = jnp.float32
    bsz, h, s, d = q.shape
    qb_len = DIL_QBLOCK
    scale = d ** -0.5
    outs, lses = [], []
    for window, r in DIL_PATTERNS:
        span = window // r
        L = -(-s // r)
        Lp = -(-L // qb_len) * qb_len
        nblk = Lp // qb_len

        def to_phase(t):
            t = jnp.pad(t, ((0, 0), (0, 0), (0, L * r - s), (0, 0)))
            t = t.reshape(bsz, h, L, r, d).transpose(0, 1, 3, 2, 4)
            t = jnp.pad(t, ((0, 0), (0, 0), (0, 0), (0, Lp - L), (0, 0)))
            return t.reshape(bsz, h, r, nblk, qb_len, d)

        def with_prev(t):
            prev = jnp.pad(t, ((0, 0), (0, 0), (0, 0), (1, 0), (0, 0), (0, 0)))[:, :, :, :-1]
            return jnp.concatenate([prev, t], axis=4)

        qp = to_phase(q)
        kw = with_prev(to_phase(k))
        vw = with_prev(to_phase(v))
        logits = jnp.einsum('bhrnqd,bhrnkd->bhrnqk', qp, kw).astype(f32) * scale
        qi = jnp.arange(qb_len)[:, None]
        kj = jnp.arange(2 * qb_len)[None, :]
        m = qi + qb_len - kj
        bidx = jnp.arange(nblk)[:, None, None]
        ok = (m >= 0) & (m <= span) & (bidx * qb_len + kj - qb_len >= 0)
        logits = jnp.where(ok, logits, -jnp.inf)
        lse = jax.nn.logsumexp(logits, axis=-1)
        pr = jnp.exp(logits - lse[..., None]).astype(v.dtype)
        o = jnp.einsum('bhrnqk,bhrnkd->bhrnqd', pr, vw)
        o = o.reshape(bsz, h, r, Lp, d)[:, :, :, :L].transpose(0, 1, 3, 2, 4)
        o = o.reshape(bsz, h, L * r, d)[:, :, :s]
        lse = lse.reshape(bsz, h, r, Lp)[..., :L].transpose(0, 1, 3, 2).reshape(bsz, h, L * r)[..., :s]
        outs.append(o)
        lses.append(lse)
    wts = jax.nn.softmax(jnp.stack(lses, axis=0), axis=0)
    out = jnp.einsum('pbhs,pbhsd->bhsd', wts, jnp.stack(outs, axis=0).astype(f32))
    return out.astype(q.dtype)


def even_mixer(hn, w_in, conv_w, conv_b, dt_bias, a_log, d_skip, ssd_norm, w_out):
    bsz, s, _ = hn.shape
    proj = hn @ w_in
    z, xbc, dt_raw, q, k, v, gate = jnp.split(proj, [Z_END, XBC_END, DT_END, Q_END, K_END, V_END], axis=-1)
    xbc = jax.nn.silu(causal_dwconv(xbc, conv_w, conv_b))
    xs, bm, cm = jnp.split(xbc, [SSD_INNER, SSD_INNER + SSD_GROUPS * SSD_STATE], axis=-1)
    xs = xs.reshape(bsz, s, SSD_HEADS, SSD_HEAD_DIM)
    bm = bm.reshape(bsz, s, SSD_GROUPS, SSD_STATE)
    cm = cm.reshape(bsz, s, SSD_GROUPS, SSD_STATE)
    dt = jax.nn.softplus((dt_raw + dt_bias).astype(jnp.float32))
    a = -jnp.exp(a_log.astype(jnp.float32))
    y = ssd_chunked(xs, dt, a, bm, cm, d_skip).reshape(bsz, s, SSD_INNER)
    y_ssd = gated_group_rmsnorm(y, z, ssd_norm)
    def heads(t):
        return t.reshape(bsz, s, MOBA_HEADS, MOBA_HEAD_DIM).transpose(0, 2, 1, 3)
    o = moba_attention(heads(q), heads(k), heads(v))
    y_moba = o * jax.nn.silu(gate)
    return jnp.concatenate([y_ssd, y_moba], axis=-1) @ w_out


def odd_mixer(hn, w_in, w_out):
    bsz, s, _ = hn.shape
    proj = hn @ w_in
    q, k, v, gate = jnp.split(proj, [DIL_INNER, 2 * DIL_INNER, 3 * DIL_INNER], axis=-1)
    def heads(t):
        return t.reshape(bsz, s, DIL_HEADS, DIL_HEAD_DIM).transpose(0, 2, 1, 3)
    o = dilated_attention(heads(q), heads(k), heads(v))
    o = o.transpose(0, 2, 1, 3).reshape(bsz, s, DIL_INNER)
    return (o * jax.nn.silu(gate)) @ w_out


def setup_inputs(seed: int = 0) -> dict:
    key = jax.random.key(seed)
    ks = jax.random.split(key, 16)
    f32 = jnp.float32

    def nrm(k, shape, scale):
        return jax.random.normal(k, shape, f32) * scale

    x = nrm(ks[0], (BATCH, SEQ, D_MODEL), 1.0)
    even_norm = 1.0 + nrm(ks[1], (N_EVEN, D_MODEL), 0.02)
    even_w_in = nrm(ks[2], (N_EVEN, D_MODEL, EVEN_IN), D_MODEL ** -0.5)
    ssd_conv_w = nrm(ks[3], (N_EVEN, SSD_CONV, SSD_CONV_DIM), SSD_CONV ** -0.5)
    ssd_conv_b = nrm(ks[4], (N_EVEN, SSD_CONV_DIM), 0.02)
    dt0 = jnp.exp(jax.random.uniform(ks[5], (N_EVEN, SSD_HEADS), f32, math.log(DT_MIN), math.log(DT_MAX)))
    ssd_dt_bias = dt0 + jnp.log(-jnp.expm1(-dt0))
    ssd_a_log = jnp.log(jax.random.uniform(ks[6], (N_EVEN, SSD_HEADS), f32, 1.0, 16.0))
    ssd_d = 1.0 + nrm(ks[7], (N_EVEN, SSD_HEADS), 0.02)
    ssd_norm = 1.0 + nrm(ks[8], (N_EVEN, SSD_INNER), 0.02)
    even_w_out = nrm(ks[9], (N_EVEN, EVEN_MIX, D_MODEL), EVEN_MIX ** -0.5)
    odd_norm = 1.0 + nrm(ks[10], (N_ODD, D_MODEL), 0.02)
    odd_w_in = nrm(ks[11], (N_ODD, D_MODEL, ODD_IN), D_MODEL ** -0.5)
    odd_w_out = nrm(ks[12], (N_ODD, DIL_INNER, D_MODEL), DIL_INNER ** -0.5)
    final_norm = 1.0 + nrm(ks[13], (D_MODEL,), 0.02)
    return {'x': x, 'even_norm': even_norm, 'even_w_in': even_w_in, 'ssd_conv_w': ssd_conv_w,
            'ssd_conv_b': ssd_conv_b, 'ssd_dt_bias': ssd_dt_bias, 'ssd_a_log': ssd_a_log,
            'ssd_d': ssd_d, 'ssd_norm': ssd_norm, 'even_w_out': even_w_out,
            'odd_norm': odd_norm, 'odd_w_in': odd_w_in, 'odd_w_out': odd_w_out,
            'final_norm': final_norm}


def reference(x, even_norm, even_w_in, ssd_conv_w, ssd_conv_b, ssd_dt_bias, ssd_a_log, ssd_d,
              ssd_norm, even_w_out, odd_norm, odd_w_in, odd_w_out, final_norm):
    for layer in range(DEPTH):
        i = layer // 2
        if layer % 2 == 0:
            hn = rmsnorm(x, even_norm[i])
            x = x + even_mixer(hn, even_w_in[i], ssd_conv_w[i], ssd_conv_b[i], ssd_dt_bias[i],
                               ssd_a_log[i], ssd_d[i], ssd_norm[i], even_w_out[i])
        else:
            hn = rmsnorm(x, odd_norm[i])
            x = x + odd_mixer(hn, odd_w_in[i], odd_w_out[i])
    return rmsnorm(x, final_norm)
```

```python
import functools
import math

import jax
import jax.numpy as jnp
from jax import lax
from jax.experimental import pallas as pl
from jax.experimental.pallas import tpu as pltpu

F32 = jnp.float32
BF16 = jnp.bfloat16

NORM_EPS = 1e-5
D_MODEL = 1024

SSD_HEADS = 16
SSD_HEAD_DIM = 64
SSD_INNER = SSD_HEADS * SSD_HEAD_DIM
SSD_GROUPS = 2
SSD_STATE = 128
SSD_CONV = 4
SSD_CHUNK = 128
SSD_BC = 2 * SSD_GROUPS * SSD_STATE
SSD_CONV_DIM = SSD_INNER + SSD_BC

MOBA_HEADS = 8
MOBA_HEAD_DIM = 128
MOBA_INNER = MOBA_HEADS * MOBA_HEAD_DIM
MOBA_BLOCK = 256
MOBA_TOPK = 3

DIL_HEADS = 16
DIL_HEAD_DIM = 128
DIL_INNER = DIL_HEADS * DIL_HEAD_DIM
DIL_QBLOCK = 128
DIL_PATTERNS = ((128, 1), (512, 4), (2048, 16))

LANES = 128
NEG_BIG = -1e30
VMEM_LIMIT = 48 * 1024 * 1024

P0_Z = 0
P0_X = P0_Z + SSD_INNER
P0_G = P0_X + SSD_INNER
P0_Q = P0_G + MOBA_INNER
P0_K = P0_Q + MOBA_INNER
P0_V = P0_K + MOBA_INNER
P0_BC = P0_V + MOBA_INNER
P0_N = P0_BC + SSD_BC


def _nt_dot(a, b):
    return lax.dot_general(a, b, (((1,), (1,)), ((), ())), preferred_element_type=F32)


def _dot(a, b):
    return jnp.dot(a, b, preferred_element_type=F32)


def _sigmoid(x):
    return 1.0 / (1.0 + jnp.exp(-x))


def _norm_proj_body(*refs, has_aux, row_chunk):
    if has_aux:
        x_ref, g_ref, w_ref, waux_ref, o_ref, oaux_ref, hn_ref = refs
    else:
        x_ref, g_ref, w_ref, o_ref, hn_ref = refs
    j = pl.program_id(1)

    @pl.when(j == 0)
    def _():
        g = g_ref[...]

        def norm_rows(c, carry):
            r0 = pl.multiple_of(c * row_chunk, row_chunk)
            x = x_ref[pl.ds(r0, row_chunk), :]
            ms = jnp.mean(x * x, axis=-1, keepdims=True)
            hn_ref[pl.ds(r0, row_chunk), :] = (x * lax.rsqrt(ms + NORM_EPS) * g).astype(BF16)
            return carry

        lax.fori_loop(0, x_ref.shape[0] // row_chunk, norm_rows, 0)
        if has_aux:
            oaux_ref[...] = _dot(hn_ref[...], waux_ref[...])

    o_ref[...] = _dot(hn_ref[...], w_ref[...]).astype(o_ref.dtype)


def _norm_proj(x2d, gain, w, w_aux=None, *, tm=1024, tn=512):
    t, d = x2d.shape
    n = w.shape[1]
    has_aux = w_aux is not None
    in_specs = [pl.BlockSpec((tm, d), lambda i, j: (i, 0)),
                pl.BlockSpec((1, d), lambda i, j: (0, 0)),
                pl.BlockSpec((d, tn), lambda i, j: (0, j))]
    out_specs = [pl.BlockSpec((tm, tn), lambda i, j: (i, j))]
    out_shape = [jax.ShapeDtypeStruct((t, n), BF16)]
    args = [x2d, gain.reshape(1, d).astype(F32), w]
    if has_aux:
        in_specs.append(pl.BlockSpec((d, LANES), lambda i, j: (0, 0)))
        out_specs.append(pl.BlockSpec((tm, LANES), lambda i, j: (i, 0)))
        out_shape.append(jax.ShapeDtypeStruct((t, LANES), F32))
        args.append(w_aux)
    outs = pl.pallas_call(
        functools.partial(_norm_proj_body, has_aux=has_aux, row_chunk=256),
        grid=(t // tm, n // tn),
        in_specs=in_specs,
        out_specs=out_specs,
        out_shape=out_shape,
        scratch_shapes=[pltpu.VMEM((tm, d), BF16)],
        compiler_params=pltpu.CompilerParams(
            dimension_semantics=("parallel", "arbitrary"), vmem_limit_bytes=VMEM_LIMIT),
        name="norm_proj_aux" if has_aux else "norm_proj",
    )(*args)
    return outs if has_aux else outs[0]


def _split3(a):
    hi = a.astype(BF16)
    r = a - hi.astype(F32)
    mid = r.astype(BF16)
    lo = (r - mid.astype(F32)).astype(BF16)
    return hi, mid, lo


def _expand_heads(a, e):
    hi, mid, lo = _split3(a)
    return _dot(hi, e) + _dot(mid, e) + _dot(lo, e)


def _ssd_body(z_ref, xs_ref, bc_ref, dt_ref, cw_ref, cb_ref, dtb_ref, alog_ref, dskip_ref, nrm_ref,
              e_ref, tri_ref, y_ref, state_ref, ubuf_ref):
    c = pl.program_id(1)
    L = SSD_CHUNK
    tail = 8

    @pl.when(c == 0)
    def _():
        state_ref[...] = jnp.zeros_like(state_ref)
        ubuf_ref[0:tail, :] = jnp.zeros((tail, SSD_CONV_DIM), F32)

    u = jnp.concatenate([xs_ref[...], bc_ref[...]], axis=1).astype(F32)
    ubuf_ref[tail:tail + L, :] = u
    cw = cw_ref[...]
    acc = cb_ref[...] + cw[SSD_CONV - 1:SSD_CONV, :] * u
    for k in range(SSD_CONV - 1):
        acc = acc + cw[k:k + 1, :] * ubuf_ref[pl.ds(tail - (SSD_CONV - 1) + k, L), :]
    ubuf_ref[0:tail, :] = u[L - tail:L, :]
    act = acc * _sigmoid(acc)

    xs = act[:, :SSD_INNER]

    dtr = dt_ref[...] + dtb_ref[...]
    dt = jnp.maximum(dtr, 0.0) + jnp.log1p(jnp.exp(-jnp.abs(dtr)))
    a = -jnp.exp(alog_ref[...])
    ac = dt * a
    tri = tri_ref[...]
    hi, mid, lo = _split3(ac)
    a_cum = _dot(tri, hi) + _dot(tri, mid) + _dot(tri, lo)
    a_last = a_cum[L - 1:L, :]
    dec_states = jnp.exp(a_last - a_cum)
    exp_acum = jnp.exp(a_cum)
    e = e_ref[...]
    dt_e = _expand_heads(dt, e)
    dec_e = _expand_heads(dec_states, e)
    ea_e = _expand_heads(exp_acum, e)
    a_cum_t = a_cum.T

    xc = xs * dt_e
    xc_b = xc.astype(BF16)
    xdec_b = (xc * dec_e).astype(BF16)

    row = lax.broadcasted_iota(jnp.int32, (L, L), 0)
    col = lax.broadcasted_iota(jnp.int32, (L, L), 1)
    causal = col <= row
    lane = lax.broadcasted_iota(jnp.int32, (L, LANES), 1)
    low_half = lane < SSD_HEAD_DIM

    heads_per_group = SSD_HEADS // SSD_GROUPS
    gw = heads_per_group * SSD_HEAD_DIM
    for g in range(SSD_GROUPS):
        b_g = act[:, SSD_INNER + g * SSD_STATE:SSD_INNER + (g + 1) * SSD_STATE]
        c_g = act[:, SSD_INNER + (SSD_GROUPS + g) * SSD_STATE:SSD_INNER + (SSD_GROUPS + g + 1) * SSD_STATE]
        b_gb = b_g.astype(BF16)
        c_gb = c_g.astype(BF16)
        scores = _nt_dot(c_gb, b_gb)
        gs = slice(g * gw, (g + 1) * gw)

        prev = state_ref[:, gs]
        y_off = _dot(c_gb, prev.astype(BF16))
        st_new = _dot(b_g.T.astype(BF16), xdec_b[:, gs])
        state_ref[:, gs] = prev * ea_e[L - 1:L, gs] + st_new

        pieces = []
        for pair in range(heads_per_group // 2):
            ms = []
            for hh in range(2):
                h = g * heads_per_group + 2 * pair + hh
                seg = a_cum[:, h:h + 1] - a_cum_t[h:h + 1, :]
                lmat = jnp.exp(jnp.where(causal, seg, NEG_BIG))
                ms.append((scores * lmat).astype(BF16))
            m_pair = jnp.concatenate(ms, axis=1)
            cs = slice(g * gw + pair * LANES, g * gw + (pair + 1) * LANES)
            x_pair = xc_b[:, cs]
            zero = jnp.zeros_like(x_pair)
            rhs = jnp.concatenate([jnp.where(low_half, x_pair, zero),
                                   jnp.where(low_half, zero, x_pair)], axis=0)
            pieces.append(_dot(m_pair, rhs))
        y_diag = jnp.concatenate(pieces, axis=1)

        y = y_diag + y_off * ea_e[:, gs] + xs[:, gs] * dskip_ref[:, gs]
        zg = z_ref[:, gs].astype(F32)
        ug = y * (zg * _sigmoid(zg))
        ug = ug * lax.rsqrt(jnp.mean(ug * ug, axis=-1, keepdims=True) + NORM_EPS)
        y_ref[:, gs] = (ug * nrm_ref[:, gs]).astype(y_ref.dtype)


def _ssd(proj, dt_raw, conv_w, conv_b, dt_bias, a_log, d_skip, ssd_norm, bsz, seq):
    t = bsz * seq
    L = SSD_CHUNK
    nc = seq // L

    def pad_heads(v):
        return jnp.pad(v.astype(F32), (0, LANES - SSD_HEADS)).reshape(1, LANES)

    head_of_chan = jnp.arange(SSD_INNER, dtype=jnp.int32) // SSD_HEAD_DIM
    expand = (jnp.arange(LANES, dtype=jnp.int32)[:, None] == head_of_chan[None, :]).astype(BF16)
    tri = (jnp.arange(L)[:, None] >= jnp.arange(L)[None, :]).astype(BF16)
    d_chan = jnp.repeat(d_skip.astype(F32), SSD_HEAD_DIM).reshape(1, SSD_INNER)

    def rows(b, c):
        return b * nc + c

    const = lambda b, c: (0, 0)
    return pl.pallas_call(
        _ssd_body,
        grid=(bsz, nc),
        in_specs=[
            pl.BlockSpec((L, SSD_INNER), lambda b, c: (rows(b, c), P0_Z // SSD_INNER)),
            pl.BlockSpec((L, SSD_INNER), lambda b, c: (rows(b, c), P0_X // SSD_INNER)),
            pl.BlockSpec((L, SSD_BC), lambda b, c: (rows(b, c), P0_BC // SSD_BC)),
            pl.BlockSpec((L, LANES), lambda b, c: (rows(b, c), 0)),
            pl.BlockSpec((SSD_CONV, SSD_CONV_DIM), const),
            pl.BlockSpec((1, SSD_CONV_DIM), const),
            pl.BlockSpec((1, LANES), const),
            pl.BlockSpec((1, LANES), const),
            pl.BlockSpec((1, SSD_INNER), const),
            pl.BlockSpec((1, SSD_INNER), const),
            pl.BlockSpec((LANES, SSD_INNER), const),
            pl.BlockSpec((L, L), const),
        ],
        out_specs=pl.BlockSpec((L, SSD_INNER), lambda b, c: (rows(b, c), 0)),
        out_shape=jax.ShapeDtypeStruct((t, SSD_INNER), BF16),
        scratch_shapes=[pltpu.VMEM((SSD_STATE, SSD_INNER), F32),
                        pltpu.VMEM((8 + L, SSD_CONV_DIM), F32)],
        compiler_params=pltpu.CompilerParams(
            dimension_semantics=("parallel", "arbitrary"), vmem_limit_bytes=VMEM_LIMIT),
        name="ssd_mixer",
    )(proj, proj, proj, dt_raw, conv_w.astype(F32), conv_b.reshape(1, -1).astype(F32),
      pad_heads(dt_bias), pad_heads(a_log), d_chan, ssd_norm.reshape(1, -1).astype(F32), expand, tri)


def _moba_body(q_ref, k_ref, v_ref, o_ref, kmean_ref, *, nb):
    i = pl.program_id(2)
    blk = MOBA_BLOCK

    @pl.when(i == 0)
    def _():
        means = [jnp.mean(k_ref[j * blk:(j + 1) * blk, :].astype(F32), axis=0, keepdims=True)
                 for j in range(nb)]
        km = jnp.concatenate(means, axis=0)
        kmean_ref[...] = jnp.concatenate([km] * (LANES // nb), axis=0).astype(BF16)

    q = q_ref[...]
    gate = _nt_dot(q, kmean_ref[...])
    lane = lax.broadcasted_iota(jnp.int32, (blk, LANES), 1)
    jblk = lane & (nb - 1)
    valid = jblk < i
    gm = jnp.where(valid, gate, -jnp.inf)
    rank = jnp.zeros((blk, LANES), F32)
    for r in range(1, nb):
        gr = pltpu.roll(gm, r, axis=1)
        kblk = (jblk - r) & (nb - 1)
        beats = (gr > gm) | ((gr == gm) & (kblk < jblk))
        rank = rank + jnp.where(beats, 1.0, 0.0)
    sel = valid & (rank < float(MOBA_TOPK))
    q_aug = jnp.concatenate([q, jnp.where(sel, 0.0, NEG_BIG).astype(BF16)], axis=1)

    row = lax.broadcasted_iota(jnp.int32, (blk, blk), 0)
    col = lax.broadcasted_iota(jnp.int32, (blk, blk), 1)

    own = pl.multiple_of(i * blk, blk)
    s = _nt_dot(q, k_ref[pl.ds(own, blk), :])
    s = jnp.where(col <= row, s, NEG_BIG)
    m = jnp.max(s, axis=-1, keepdims=True)
    p = jnp.exp(s - m)
    l = jnp.sum(p, axis=-1, keepdims=True)
    acc = _dot(p.astype(BF16), v_ref[pl.ds(own, blk), :])

    def past_block(j, carry):
        m, l, acc = carry
        r0 = pl.multiple_of(j * blk, blk)
        k_aug = jnp.concatenate([k_ref[pl.ds(r0, blk), :],
                                 jnp.where(lane == j, 1.0, 0.0).astype(BF16)], axis=1)
        s = _nt_dot(q_aug, k_aug)
        m_new = jnp.maximum(m, jnp.max(s, axis=-1, keepdims=True))
        alpha = jnp.exp(m - m_new)
        p = jnp.exp(s - m_new)
        l = alpha * l + jnp.sum(p, axis=-1, keepdims=True)
        acc = alpha * acc + _dot(p.astype(BF16), v_ref[pl.ds(r0, blk), :])
        return m_new, l, acc

    m, l, acc = lax.fori_loop(0, i, past_block, (m, l, acc))
    o_ref[...] = (acc / l).astype(o_ref.dtype)


def _moba(proj, bsz, seq):
    t = bsz * seq
    blk = MOBA_BLOCK
    nb = seq // blk
    d = MOBA_HEAD_DIM

    def head_block(base):
        return pl.BlockSpec((seq, d), lambda b, h, i: (b, base // d + h))

    return pl.pallas_call(
        functools.partial(_moba_body, nb=nb),
        grid=(bsz, MOBA_HEADS, nb),
        in_specs=[pl.BlockSpec((blk, d), lambda b, h, i: (b * nb + i, P0_Q // d + h)),
                  head_block(P0_K), head_block(P0_V)],
        out_specs=pl.BlockSpec((blk, d), lambda b, h, i: (b * nb + i, h)),
        out_shape=jax.ShapeDtypeStruct((t, MOBA_INNER), BF16),
        scratch_shapes=[pltpu.VMEM((LANES, d), BF16)],
        compiler_params=pltpu.CompilerParams(
            dimension_semantics=("parallel", "parallel", "arbitrary"), vmem_limit_bytes=VMEM_LIMIT),
        name="moba_attn",
    )(proj, proj, proj)


def _out_proj_body(*refs, n_plain, final_norm):
    plain = refs[:n_plain]
    a_ref, g_ref, x_ref, w_ref = refs[n_plain:n_plain + 4]
    o_ref = refs[-1]
    g = g_ref[...].astype(F32)
    gated = (a_ref[...].astype(F32) * (g * _sigmoid(g))).astype(BF16)
    k0 = 0
    y = x_ref[...]
    for r in plain:
        kw = r.shape[1]
        y = y + _dot(r[...], w_ref[k0:k0 + kw, :])
        k0 += kw
    y = y + _dot(gated, w_ref[k0:k0 + gated.shape[1], :])
    if final_norm:
        fn_ref = refs[-2]
        y = y * lax.rsqrt(jnp.mean(y * y, axis=-1, keepdims=True) + NORM_EPS) * fn_ref[...]
    o_ref[...] = y


def _out_proj(plain, gated, gate_src, gate_col, x2d, w, final_gain=None, *, tm=512):
    t, d = x2d.shape
    kg = gated.shape[1]
    in_specs = [pl.BlockSpec((tm, a.shape[1]), lambda i: (i, 0)) for a in plain]
    in_specs += [pl.BlockSpec((tm, kg), lambda i: (i, 0)),
                 pl.BlockSpec((tm, kg), lambda i: (i, gate_col // kg)),
                 pl.BlockSpec((tm, d), lambda i: (i, 0)),
                 pl.BlockSpec(w.shape, lambda i: (0, 0))]
    args = list(plain) + [gated, gate_src, x2d, w]
    if final_gain is not None:
        in_specs.append(pl.BlockSpec((1, d), lambda i: (0, 0)))
        args.append(final_gain.reshape(1, d).astype(F32))
    return pl.pallas_call(
        functools.partial(_out_proj_body, n_plain=len(plain), final_norm=final_gain is not None),
        grid=(t // tm,),
        in_specs=in_specs,
        out_specs=pl.BlockSpec((tm, d), lambda i: (i, 0)),
        out_shape=jax.ShapeDtypeStruct((t, d), F32),
        compiler_params=pltpu.CompilerParams(
            dimension_semantics=("parallel",), vmem_limit_bytes=VMEM_LIMIT),
        name="out_proj_final" if final_gain is not None else "out_proj",
    )(*args)


def _dil_block(qb, kb, vb, first):
    qn, kn = qb.shape[0], kb.shape[0]
    s = _nt_dot(qb, kb)
    row = lax.broadcasted_iota(jnp.int32, (qn, kn), 0)
    col = lax.broadcasted_iota(jnp.int32, (qn, kn), 1)
    if first:
        ok = col <= row
    else:
        ok = (col >= row) & (col <= row + qn)
    s = jnp.where(ok, s, NEG_BIG)
    m = jnp.max(s, axis=-1, keepdims=True)
    p = jnp.exp(s - m)
    l = jnp.sum(p, axis=-1, keepdims=True)
    o = _dot(p.astype(BF16), vb) / l
    return o, m + jnp.log(l)


def _dil_body(q_ref, k_ref, v_ref, o_ref, qf, kf, vf, qp, kp, vp, opat, lpat, *, seq):
    qb_len = DIL_QBLOCK

    qf[...] = q_ref[...].astype(F32)
    kf[...] = k_ref[...].astype(F32)
    vf[...] = v_ref[...].astype(F32)

    for pi, (window, r) in enumerate(DIL_PATTERNS):
        L = seq // r
        nblk = L // qb_len
        if r == 1:
            src_q, src_k, src_v = q_ref, k_ref, v_ref
        else:
            for ph in range(r):
                qp[ph * L:(ph + 1) * L, :] = qf[pl.ds(ph, L, stride=r), :].astype(BF16)
                kp[ph * L:(ph + 1) * L, :] = kf[pl.ds(ph, L, stride=r), :].astype(BF16)
                vp[ph * L:(ph + 1) * L, :] = vf[pl.ds(ph, L, stride=r), :].astype(BF16)
            src_q, src_k, src_v = qp, kp, vp
        for ph in range(r):
            for n in range(nblk):
                q0 = ph * L + n * qb_len
                qb = src_q[q0:q0 + qb_len, :]
                if n == 0:
                    kb = src_k[q0:q0 + qb_len, :]
                    vb = src_v[q0:q0 + qb_len, :]
                else:
                    kb = src_k[q0 - qb_len:q0 + qb_len, :]
                    vb = src_v[q0 - qb_len:q0 + qb_len, :]
                o, lse = _dil_block(qb, kb, vb, n == 0)
                if r == 1:
                    opat[pi, q0:q0 + qb_len, :] = o
                    lpat[pi, q0:q0 + qb_len, :] = lse
                else:
                    t0 = n * qb_len * r + ph
                    opat[pi, pl.ds(t0, qb_len, stride=r), :] = o
                    lpat[pi, pl.ds(t0, qb_len, stride=r), :] = lse

    rows = 256
    for c in range(seq // rows):
        rs = slice(c * rows, (c + 1) * rows)
        ls = [lpat[pi, rs, :] for pi in range(len(DIL_PATTERNS))]
        mx = functools.reduce(jnp.maximum, ls)
        es = [jnp.exp(x - mx) for x in ls]
        den = functools.reduce(lambda a, b: a + b, es)
        num = functools.reduce(lambda a, b: a + b,
                               [e * opat[pi, rs, :] for pi, e in enumerate(es)])
        o_ref[rs, :] = (num / den).astype(o_ref.dtype)


def _dilated(proj, bsz, seq):
    t = bsz * seq
    d = DIL_HEAD_DIM
    npat = len(DIL_PATTERNS)

    def head_block(base):
        return pl.BlockSpec((seq, d), lambda b, h: (b, base // d + h))

    return pl.pallas_call(
        functools.partial(_dil_body, seq=seq),
        grid=(bsz, DIL_HEADS),
        in_specs=[head_block(0), head_block(DIL_INNER), head_block(2 * DIL_INNER)],
        out_specs=pl.BlockSpec((seq, d), lambda b, h: (b, h)),
        out_shape=jax.ShapeDtypeStruct((t, DIL_INNER), BF16),
        scratch_shapes=[pltpu.VMEM((seq, d), F32)] * 3 + [pltpu.VMEM((seq, d), BF16)] * 3
        + [pltpu.VMEM((npat, seq, d), F32), pltpu.VMEM((npat, seq, 1), F32)],
        compiler_params=pltpu.CompilerParams(
            dimension_semantics=("parallel", "parallel"), vmem_limit_bytes=VMEM_LIMIT),
        name="dilated_attn",
    )(proj, proj, proj)


def kernel(x, even_norm, even_w_in, ssd_conv_w, ssd_conv_b, ssd_dt_bias, ssd_a_log, ssd_d, ssd_norm,
           even_w_out, odd_norm, odd_w_in, odd_w_out, final_norm):
    bsz, seq, d = x.shape
    t = bsz * seq
    x2d = x.reshape(t, d)

    w = even_w_in[0]
    z_end = SSD_INNER
    xbc_end = z_end + SSD_CONV_DIM
    dt_end = xbc_end + SSD_HEADS
    q_end = dt_end + MOBA_INNER
    v_end = q_end + 2 * MOBA_INNER
    x_end = z_end + SSD_INNER
    moba_scale = MOBA_HEAD_DIM ** -0.5
    w_main = jnp.concatenate([w[:, :x_end], w[:, v_end:], w[:, dt_end:q_end] * moba_scale,
                              w[:, q_end:v_end], w[:, x_end:xbc_end]],
                             axis=1).astype(BF16)
    w_dt = jnp.pad(w[:, xbc_end:dt_end], ((0, 0), (0, LANES - SSD_HEADS))).astype(BF16)
    proj0, dt_raw = _norm_proj(x2d, even_norm[0], w_main, w_dt)
    y_ssd = _ssd(proj0, dt_raw, ssd_conv_w[0], ssd_conv_b[0], ssd_dt_bias[0], ssd_a_log[0],
                 ssd_d[0], ssd_norm[0], bsz, seq)
    o_moba = _moba(proj0, bsz, seq)
    x1 = _out_proj([y_ssd], o_moba, proj0, P0_G, x2d, even_w_out[0].astype(BF16))

    w1 = odd_w_in[0]
    dil_scale = DIL_HEAD_DIM ** -0.5
    w1_main = jnp.concatenate([w1[:, :DIL_INNER] * dil_scale, w1[:, DIL_INNER:]], axis=1).astype(BF16)
    proj1 = _norm_proj(x1, odd_norm[0], w1_main)
    o_dil = _dilated(proj1, bsz, seq)
    out = _out_proj([], o_dil, proj1, 3 * DIL_INNER, x1, odd_w_out[0].astype(BF16), final_norm)
    return out.reshape(bsz, seq, d)
```

```python
import functools
import math

import jax
import jax.numpy as jnp
from jax import lax
from jax.experimental import pallas as pl
from jax.experimental.pallas import tpu as pltpu

F32 = jnp.float32
BF16 = jnp.bfloat16

NORM_EPS = 1e-5
D_MODEL = 1024

SSD_HEADS = 16
SSD_HEAD_DIM = 64
SSD_INNER = SSD_HEADS * SSD_HEAD_DIM
SSD_GROUPS = 2
SSD_STATE = 128
SSD_CONV = 4
SSD_CHUNK = 128
SSD_BC = 2 * SSD_GROUPS * SSD_STATE
SSD_CONV_DIM = SSD_INNER + SSD_BC

MOBA_HEADS = 8
MOBA_HEAD_DIM = 128
MOBA_INNER = MOBA_HEADS * MOBA_HEAD_DIM
MOBA_BLOCK = 256
MOBA_TOPK = 3

DIL_HEADS = 16
DIL_HEAD_DIM = 128
DIL_INNER = DIL_HEADS * DIL_HEAD_DIM
DIL_QBLOCK = 128
DIL_PATTERNS = ((128, 1), (512, 4), (2048, 16))

LANES = 128
NEG_BIG = -1e30
VMEM_LIMIT = 48 * 1024 * 1024

P0_Z = 0
P0_X = P0_Z + SSD_INNER
P0_G = P0_X + SSD_INNER
P0_Q = P0_G + MOBA_INNER
P0_K = P0_Q + MOBA_INNER
P0_V = P0_K + MOBA_INNER
P0_BC = P0_V + MOBA_INNER
P0_N = P0_BC + SSD_BC


def _nt_dot(a, b):
    return lax.dot_general(a, b, (((1,), (1,)), ((), ())), preferred_element_type=F32)


def _dot(a, b):
    return jnp.dot(a, b, preferred_element_type=F32)


def _sigmoid(x):
    return 1.0 / (1.0 + jnp.exp(-x))


def _norm_proj_body(*refs, has_aux, row_chunk, phases, n_phase_tiles):
    if has_aux:
        x_ref, g_ref, w_ref, waux_ref, o_ref, oaux_ref, hn_ref = refs
    elif phases:
        x_ref, g_ref, w_ref, o_ref, hn_ref, hnp_ref, hnf_ref = refs
    else:
        x_ref, g_ref, w_ref, o_ref, hn_ref = refs
    j = pl.program_id(1)
    tm = x_ref.shape[0]

    @pl.when(j == 0)
    def _():
        g = g_ref[...]

        def norm_rows(c, carry):
            r0 = pl.multiple_of(c * row_chunk, row_chunk)
            x = x_ref[pl.ds(r0, row_chunk), :]
            ms = jnp.mean(x * x, axis=-1, keepdims=True)
            hn = x * lax.rsqrt(ms + NORM_EPS) * g
            hn_ref[pl.ds(r0, row_chunk), :] = hn.astype(BF16)
            if phases:
                for cc in range(x.shape[1] // LANES):
                    hnf_ref[cc, pl.ds(r0, row_chunk), :] = hn[:, cc * LANES:(cc + 1) * LANES]
            return carry

        lax.fori_loop(0, tm // row_chunk, norm_rows, 0)
        if phases:
            per = tm // phases
            for p in range(phases):
                for cc in range(hnf_ref.shape[0]):
                    hnp_ref[p * per:(p + 1) * per, cc * LANES:(cc + 1) * LANES] = (
                        hnf_ref[cc, pl.ds(p, per, stride=phases), :].astype(BF16))
        if has_aux:
            oaux_ref[...] = _dot(hn_ref[...], waux_ref[...])

    if phases:
        @pl.when(j < n_phase_tiles)
        def _():
            o_ref[...] = _dot(hnp_ref[...], w_ref[...]).astype(o_ref.dtype)

        @pl.when(j >= n_phase_tiles)
        def _():
            o_ref[...] = _dot(hn_ref[...], w_ref[...]).astype(o_ref.dtype)
    else:
        o_ref[...] = _dot(hn_ref[...], w_ref[...]).astype(o_ref.dtype)


def _norm_proj(x2d, gain, w, w_aux=None, *, tm=1024, tn=512, phases=0, phase_cols=0):
    t, d = x2d.shape
    n = w.shape[1]
    has_aux = w_aux is not None
    in_specs = [pl.BlockSpec((tm, d), lambda i, j: (i, 0)),
                pl.BlockSpec((1, d), lambda i, j: (0, 0)),
                pl.BlockSpec((d, tn), lambda i, j: (0, j))]
    out_specs = [pl.BlockSpec((tm, tn), lambda i, j: (i, j))]
    out_shape = [jax.ShapeDtypeStruct((t, n), BF16)]
    args = [x2d, gain.reshape(1, d).astype(F32), w]
    if has_aux:
        in_specs.append(pl.BlockSpec((d, LANES), lambda i, j: (0, 0)))
        out_specs.append(pl.BlockSpec((tm, LANES), lambda i, j: (i, 0)))
        out_shape.append(jax.ShapeDtypeStruct((t, LANES), F32))
        args.append(w_aux)
    outs = pl.pallas_call(
        functools.partial(_norm_proj_body, has_aux=has_aux, row_chunk=256, phases=phases,
                          n_phase_tiles=phase_cols // tn),
        grid=(t // tm, n // tn),
        in_specs=in_specs,
        out_specs=out_specs,
        out_shape=out_shape,
        scratch_shapes=[pltpu.VMEM((tm, d), BF16)] * (2 if phases else 1)
        + ([pltpu.VMEM((d // LANES, tm, LANES), F32)] if phases else []),
        compiler_params=pltpu.CompilerParams(
            dimension_semantics=("parallel", "arbitrary"), vmem_limit_bytes=VMEM_LIMIT),
        name="norm_proj_aux" if has_aux else "norm_proj",
    )(*args)
    return outs if has_aux else outs[0]


def _split3(a):
    hi = a.astype(BF16)
    r = a - hi.astype(F32)
    mid = r.astype(BF16)
    lo = (r - mid.astype(F32)).astype(BF16)
    return hi, mid, lo


def _expand_heads(a, e):
    hi, mid, lo = _split3(a)
    return _dot(hi, e) + _dot(mid, e) + _dot(lo, e)


def _ssd_body(z_ref, xs_ref, bc_ref, dt_ref, cw_ref, cb_ref, dtb_ref, alog_ref, dskip_ref, nrm_ref,
              e_ref, tri_ref, y_ref, state_ref, ubuf_ref):
    c = pl.program_id(1)
    L = SSD_CHUNK
    tail = 8

    @pl.when(c == 0)
    def _():
        state_ref[...] = jnp.zeros_like(state_ref)
        ubuf_ref[0:tail, :] = jnp.zeros((tail, SSD_CONV_DIM), F32)

    u = jnp.concatenate([xs_ref[...], bc_ref[...]], axis=1).astype(F32)
    ubuf_ref[tail:tail + L, :] = u
    cw = cw_ref[...]
    acc = cb_ref[...] + cw[SSD_CONV - 1:SSD_CONV, :] * u
    for k in range(SSD_CONV - 1):
        acc = acc + cw[k:k + 1, :] * ubuf_ref[pl.ds(tail - (SSD_CONV - 1) + k, L), :]
    ubuf_ref[0:tail, :] = u[L - tail:L, :]
    act = acc * _sigmoid(acc)

    xs = act[:, :SSD_INNER]

    dtr = dt_ref[...] + dtb_ref[...]
    dt = jnp.maximum(dtr, 0.0) + jnp.log1p(jnp.exp(-jnp.abs(dtr)))
    a = -jnp.exp(alog_ref[...])
    ac = dt * a
    tri = tri_ref[...]
    hi, mid, lo = _split3(ac)
    a_cum = _dot(tri, hi) + _dot(tri, mid) + _dot(tri, lo)
    a_last = a_cum[L - 1:L, :]
    dec_states = jnp.exp(a_last - a_cum)
    exp_acum = jnp.exp(a_cum)
    e = e_ref[...]
    dt_e = _expand_heads(dt, e)
    dec_e = _expand_heads(dec_states, e)
    ea_e = _expand_heads(exp_acum, e)
    a_cum_t = a_cum.T

    xc = xs * dt_e
    xc_b = xc.astype(BF16)
    xdec_b = (xc * dec_e).astype(BF16)

    row = lax.broadcasted_iota(jnp.int32, (L, L), 0)
    col = lax.broadcasted_iota(jnp.int32, (L, L), 1)
    causal = col <= row
    lane = lax.broadcasted_iota(jnp.int32, (L, LANES), 1)
    low_half = lane < SSD_HEAD_DIM

    heads_per_group = SSD_HEADS // SSD_GROUPS
    gw = heads_per_group * SSD_HEAD_DIM
    for g in range(SSD_GROUPS):
        b_g = act[:, SSD_INNER + g * SSD_STATE:SSD_INNER + (g + 1) * SSD_STATE]
        c_g = act[:, SSD_INNER + (SSD_GROUPS + g) * SSD_STATE:SSD_INNER + (SSD_GROUPS + g + 1) * SSD_STATE]
        b_gb = b_g.astype(BF16)
        c_gb = c_g.astype(BF16)
        scores = _nt_dot(c_gb, b_gb)
        gs = slice(g * gw, (g + 1) * gw)

        prev = state_ref[:, gs]
        y_off = _dot(c_gb, prev.astype(BF16))
        st_new = _dot(b_g.T.astype(BF16), xdec_b[:, gs])
        state_ref[:, gs] = prev * ea_e[L - 1:L, gs] + st_new

        pieces = []
        for pair in range(heads_per_group // 2):
            ms = []
            for hh in range(2):
                h = g * heads_per_group + 2 * pair + hh
                seg = a_cum[:, h:h + 1] - a_cum_t[h:h + 1, :]
                lmat = jnp.exp(jnp.where(causal, seg, NEG_BIG))
                ms.append((scores * lmat).astype(BF16))
            m_pair = jnp.concatenate(ms, axis=1)
            cs = slice(g * gw + pair * LANES, g * gw + (pair + 1) * LANES)
            x_pair = xc_b[:, cs]
            zero = jnp.zeros_like(x_pair)
            rhs = jnp.concatenate([jnp.where(low_half, x_pair, zero),
                                   jnp.where(low_half, zero, x_pair)], axis=0)
            pieces.append(_dot(m_pair, rhs))
        y_diag = jnp.concatenate(pieces, axis=1)

        y = y_diag + y_off * ea_e[:, gs] + xs[:, gs] * dskip_ref[:, gs]
        zg = z_ref[:, gs].astype(F32)
        ug = y * (zg * _sigmoid(zg))
        ug = ug * lax.rsqrt(jnp.mean(ug * ug, axis=-1, keepdims=True) + NORM_EPS)
        y_ref[:, gs] = (ug * nrm_ref[:, gs]).astype(y_ref.dtype)


def _ssd(proj, dt_raw, conv_w, conv_b, dt_bias, a_log, d_skip, ssd_norm, bsz, seq):
    t = bsz * seq
    L = SSD_CHUNK
    nc = seq // L

    def pad_heads(v):
        return jnp.pad(v.astype(F32), (0, LANES - SSD_HEADS)).reshape(1, LANES)

    head_of_chan = jnp.arange(SSD_INNER, dtype=jnp.int32) // SSD_HEAD_DIM
    expand = (jnp.arange(LANES, dtype=jnp.int32)[:, None] == head_of_chan[None, :]).astype(BF16)
    tri = (jnp.arange(L)[:, None] >= jnp.arange(L)[None, :]).astype(BF16)
    d_chan = jnp.repeat(d_skip.astype(F32), SSD_HEAD_DIM).reshape(1, SSD_INNER)

    def rows(b, c):
        return b * nc + c

    const = lambda b, c: (0, 0)
    return pl.pallas_call(
        _ssd_body,
        grid=(bsz, nc),
        in_specs=[
            pl.BlockSpec((L, SSD_INNER), lambda b, c: (rows(b, c), P0_Z // SSD_INNER)),
            pl.BlockSpec((L, SSD_INNER), lambda b, c: (rows(b, c), P0_X // SSD_INNER)),
            pl.BlockSpec((L, SSD_BC), lambda b, c: (rows(b, c), P0_BC // SSD_BC)),
            pl.BlockSpec((L, LANES), lambda b, c: (rows(b, c), 0)),
            pl.BlockSpec((SSD_CONV, SSD_CONV_DIM), const),
            pl.BlockSpec((1, SSD_CONV_DIM), const),
            pl.BlockSpec((1, LANES), const),
            pl.BlockSpec((1, LANES), const),
            pl.BlockSpec((1, SSD_INNER), const),
            pl.BlockSpec((1, SSD_INNER), const),
            pl.BlockSpec((LANES, SSD_INNER), const),
            pl.BlockSpec((L, L), const),
        ],
        out_specs=pl.BlockSpec((L, SSD_INNER), lambda b, c: (rows(b, c), 0)),
        out_shape=jax.ShapeDtypeStruct((t, SSD_INNER), BF16),
        scratch_shapes=[pltpu.VMEM((SSD_STATE, SSD_INNER), F32),
                        pltpu.VMEM((8 + L, SSD_CONV_DIM), F32)],
        compiler_params=pltpu.CompilerParams(
            dimension_semantics=("parallel", "arbitrary"), vmem_limit_bytes=VMEM_LIMIT),
        name="ssd_mixer",
    )(proj, proj, proj, dt_raw, conv_w.astype(F32), conv_b.reshape(1, -1).astype(F32),
      pad_heads(dt_bias), pad_heads(a_log), d_chan, ssd_norm.reshape(1, -1).astype(F32), expand, tri)


def _moba_body(q_ref, k_ref, v_ref, avg_ref, hot_ref, eye_ref, o_ref, *, nb, q_tile):
    blk = MOBA_BLOCK
    n_dense = MOBA_TOPK + 1

    kmean = _dot(avg_ref[...], k_ref[...]).astype(BF16)
    jrow = lax.broadcasted_iota(jnp.int32, (nb, blk), 0)
    row = lax.broadcasted_iota(jnp.int32, (q_tile, blk), 0)
    col = lax.broadcasted_iota(jnp.int32, (q_tile, blk), 1)

    def query_block(i):
        q_i = q_ref[i * blk:(i + 1) * blk, :]
        if i < n_dense:
            return q_i, None
        gate_t = _nt_dot(kmean, q_i)[0:nb, :]
        valid = jrow < i
        gm = jnp.where(valid, gate_t, -jnp.inf)
        rank = jnp.zeros((nb, blk), F32)
        for r in range(1, nb):
            gr = pltpu.roll(gm, r, axis=0)
            lower = ((jrow - r) & (nb - 1)) < jrow
            rank = rank + jnp.where(gr > gm, 1.0, 0.0) + jnp.where((gr == gm) & lower, 1.0, 0.0)
        keep = valid & (rank < float(MOBA_TOPK))
        bias_t = jnp.where(keep, 0.0, NEG_BIG)
        bias_t = jnp.concatenate([bias_t, jnp.zeros((LANES - nb, blk), F32)], axis=0).astype(BF16)
        bias = _nt_dot(eye_ref[...], bias_t).astype(BF16)
        return q_i, jnp.concatenate([q_i, bias], axis=1)

    def scores(i, h, q_i, q_aug):
        rs = slice(h * q_tile, (h + 1) * q_tile)
        s_own = _nt_dot(q_i[rs, :], k_ref[i * blk:(i + 1) * blk, :])
        s_own = jnp.where(col <= row + h * q_tile, s_own, NEG_BIG)
        if i == 0:
            return s_own, None
        if q_aug is None:
            return s_own, _nt_dot(q_i[rs, :], k_ref[0:i * blk, :])
        k_past = jnp.concatenate([k_ref[0:i * blk, :], hot_ref[0:i * blk, :]], axis=1)
        return s_own, _nt_dot(q_aug[rs, :], k_past)

    def finish(i, h, s_own, s_past):
        m = jnp.max(s_own, axis=-1, keepdims=True)
        if s_past is None:
            p = jnp.exp2(s_own - m)
        else:
            m = jnp.maximum(m, jnp.max(s_past, axis=-1, keepdims=True))
            p = jnp.concatenate([jnp.exp2(s_past - m), jnp.exp2(s_own - m)], axis=1)
        l = jnp.sum(p, axis=-1, keepdims=True)
        acc = _dot(p.astype(BF16), v_ref[0:(i + 1) * blk, :])
        o_ref[i * blk + h * q_tile:i * blk + (h + 1) * q_tile, :] = (acc / l).astype(o_ref.dtype)

    tiles = [(i, h) for i in range(nb) for h in range(blk // q_tile)]
    qcache = {}
    pending = None
    for i, h in tiles:
        if i not in qcache:
            qcache = {i: query_block(i)}
        s_next = scores(i, h, *qcache[i])
        if pending is not None:
            finish(*pending)
        pending = (i, h) + s_next
    finish(*pending)


def _moba(proj, bsz, seq):
    t = bsz * seq
    blk = MOBA_BLOCK
    nb = seq // blk
    d = MOBA_HEAD_DIM
    blk_of_key = jnp.arange(seq, dtype=jnp.int32) // blk
    avg = (jnp.arange(2 * nb, dtype=jnp.int32)[:, None] == blk_of_key[None, :]).astype(BF16) / blk
    hot = (blk_of_key[:, None] == jnp.arange(LANES, dtype=jnp.int32)[None, :]).astype(BF16)
    eye = jnp.eye(blk, dtype=BF16)

    def head_block(base):
        return pl.BlockSpec((seq, d), lambda b, h: (b, base // d + h))

    const = lambda b, h: (0, 0)
    return pl.pallas_call(
        functools.partial(_moba_body, nb=nb, q_tile=128),
        grid=(bsz, MOBA_HEADS),
        in_specs=[head_block(P0_Q), head_block(P0_K), head_block(P0_V),
                  pl.BlockSpec((2 * nb, seq), const), pl.BlockSpec((seq, LANES), const),
                  pl.BlockSpec((blk, blk), const)],
        out_specs=pl.BlockSpec((seq, d), lambda b, h: (b, h)),
        out_shape=jax.ShapeDtypeStruct((t, MOBA_INNER), BF16),
        compiler_params=pltpu.CompilerParams(
            dimension_semantics=("parallel", "parallel"), vmem_limit_bytes=VMEM_LIMIT),
        name="moba_attn",
    )(proj, proj, proj, avg, hot, eye)


def _out_proj_body(*refs, n_plain, final_norm):
    plain = refs[:n_plain]
    a_ref, g_ref, x_ref, w_ref = refs[n_plain:n_plain + 4]
    o_ref = refs[-1]
    g = g_ref[...].astype(F32)
    gated = (a_ref[...].astype(F32) * (g * _sigmoid(g))).astype(BF16)
    k0 = 0
    y = x_ref[...]
    for r in plain:
        kw = r.shape[1]
        y = y + _dot(r[...], w_ref[k0:k0 + kw, :])
        k0 += kw
    y = y + _dot(gated, w_ref[k0:k0 + gated.shape[1], :])
    if final_norm:
        fn_ref = refs[-2]
        y = y * lax.rsqrt(jnp.mean(y * y, axis=-1, keepdims=True) + NORM_EPS) * fn_ref[...]
    o_ref[...] = y


def _out_proj(plain, gated, gate_src, gate_col, x2d, w, final_gain=None, *, tm=512):
    t, d = x2d.shape
    kg = gated.shape[1]
    in_specs = [pl.BlockSpec((tm, a.shape[1]), lambda i: (i, 0)) for a in plain]
    in_specs += [pl.BlockSpec((tm, kg), lambda i: (i, 0)),
                 pl.BlockSpec((tm, kg), lambda i: (i, gate_col // kg)),
                 pl.BlockSpec((tm, d), lambda i: (i, 0)),
                 pl.BlockSpec(w.shape, lambda i: (0, 0))]
    args = list(plain) + [gated, gate_src, x2d, w]
    if final_gain is not None:
        in_specs.append(pl.BlockSpec((1, d), lambda i: (0, 0)))
        args.append(final_gain.reshape(1, d).astype(F32))
    return pl.pallas_call(
        functools.partial(_out_proj_body, n_plain=len(plain), final_norm=final_gain is not None),
        grid=(t // tm,),
        in_specs=in_specs,
        out_specs=pl.BlockSpec((tm, d), lambda i: (i, 0)),
        out_shape=jax.ShapeDtypeStruct((t, d), F32),
        compiler_params=pltpu.CompilerParams(
            dimension_semantics=("parallel",), vmem_limit_bytes=VMEM_LIMIT),
        name="out_proj_final" if final_gain is not None else "out_proj",
    )(*args)


DIL_PHASES = 8


def _dil_tiles(seq):
    ph, qb = DIL_PHASES, DIL_QBLOCK
    per = seq // ph
    tiles = []
    rows = qb // ph
    for n in range(seq // qb):
        qs = [(p * per + rows * n, rows) for p in range(ph)]
        if n == 0:
            ks, mask = [(p * per, rows) for p in range(ph)], "a_first"
        else:
            ks, mask = [(p * per + rows * (n - 1), 2 * rows) for p in range(ph)], "a"
        tiles.append((qs, ks, mask, "init"))
    seg = ph // 4
    rows = qb // seg
    for p4 in range(4):
        for n in range(seq // 4 // qb):
            qs = [((p4 + 4 * j) * per + rows * n, rows) for j in range(seg)]
            if n == 0:
                ks, mask = [((p4 + 4 * j) * per, rows) for j in range(seg)], "b_first"
            else:
                ks, mask = [((p4 + 4 * j) * per + rows * (n - 1), 2 * rows) for j in range(seg)], "b"
            tiles.append((qs, ks, mask, "merge"))
    for p in range(ph):
        for h in range(per // qb):
            qs = [(p * per + qb * h, qb)]
            ks = [(p * per, qb * (h + 1))]
            tiles.append((qs, ks, "c%d" % h, "final"))
    return tiles


def _dil_masks(seq):
    import numpy as np
    ph, qb = DIL_PHASES, DIL_QBLOCK
    per = seq // ph

    def pos(n_rows, seg_rows, step):
        i = np.arange(n_rows)
        return step * (i % seg_rows) + i // seg_rows

    def band(qpos, kpos, span):
        dist = qpos[:, None] - kpos[None, :]
        return np.where((dist >= 0) & (dist <= span), 0.0, NEG_BIG).astype(np.float32)

    masks = {}
    for name, seg in (("a", ph), ("b", ph // 4)):
        rows = qb // seg
        masks[name] = band(qb + pos(qb, rows, seg), pos(2 * qb, 2 * rows, seg), qb)
        masks[name + "_first"] = band(pos(qb, rows, seg), pos(qb, rows, seg), qb)
    for h in range(per // qb):
        lq = qb * h + np.arange(qb)
        lk = np.arange(qb * (h + 1))
        same = (lq[:, None] - lk[None, :]) % 2 == 0
        masks["c%d" % h] = np.where(same & (lk[None, :] <= lq[:, None]), 0.0, NEG_BIG).astype(np.float32)
    return masks


def _dil_body(*refs, seq, mask_names):
    q_ref, k_ref, v_ref = refs[:3]
    mask_refs = dict(zip(mask_names, refs[3:3 + len(mask_names)]))
    o_ref, acc_ref, m_ref, l_ref, onat_ref = refs[3 + len(mask_names):]
    qb = DIL_QBLOCK
    d = DIL_HEAD_DIM

    def rows_of(ref, slices):
        parts = [ref[s:s + n, :] for s, n in slices]
        return parts[0] if len(parts) == 1 else jnp.concatenate(parts, axis=0)

    def put_rows(ref, slices, val):
        r0 = 0
        for s, n in slices:
            ref[s:s + n, :] = val[r0:r0 + n, :]
            r0 += n

    def scores(tile):
        qs, ks, mask, _ = tile
        return _nt_dot(rows_of(q_ref, qs), rows_of(k_ref, ks)) + mask_refs[mask][...]

    def finish(tile, s):
        qs, ks, _, stage = tile
        m_blk = jnp.max(s, axis=-1, keepdims=True)
        if stage == "init":
            m_new = m_blk
        else:
            m_old = rows_of(m_ref, qs)
            m_new = jnp.maximum(m_old[:, 0:1], m_blk)
        p = jnp.exp2(s - m_new).astype(BF16)
        vb = rows_of(v_ref, ks)
        pv = _dot(p, jnp.concatenate([vb, jnp.ones_like(vb)], axis=1))
        acc, l = pv[:, :d], pv[:, d:]
        m_b = jnp.broadcast_to(m_new, (qb, d))
        if stage != "init":
            alpha = jnp.exp2(m_old - m_b)
            acc = alpha * rows_of(acc_ref, qs) + acc
            l = alpha * rows_of(l_ref, qs) + l
        if stage == "final":
            (s0, _), = qs
            p_idx, l0 = divmod(s0, seq // DIL_PHASES)
            onat_ref[pl.ds(DIL_PHASES * l0 + p_idx, qb, stride=DIL_PHASES), :] = acc / l
        else:
            put_rows(acc_ref, qs, acc)
            put_rows(l_ref, qs, l)
            put_rows(m_ref, qs, m_b)

    pending = None
    for tile in _dil_tiles(seq):
        s_next = scores(tile)
        if pending is not None:
            finish(*pending)
        pending = (tile, s_next)
    finish(*pending)
    o_ref[...] = onat_ref[...].astype(o_ref.dtype)


def _dilated(proj, bsz, seq):
    t = bsz * seq
    d = DIL_HEAD_DIM
    masks = _dil_masks(seq)
    names = tuple(sorted(masks))

    def head_block(base):
        return pl.BlockSpec((seq, d), lambda b, h: (b, base // d + h))

    return pl.pallas_call(
        functools.partial(_dil_body, seq=seq, mask_names=names),
        grid=(bsz, DIL_HEADS),
        in_specs=[head_block(0), head_block(DIL_INNER), head_block(2 * DIL_INNER)]
        + [pl.BlockSpec(masks[n].shape, lambda b, h: (0, 0)) for n in names],
        out_specs=pl.BlockSpec((seq, d), lambda b, h: (b, h)),
        out_shape=jax.ShapeDtypeStruct((t, DIL_INNER), BF16),
        scratch_shapes=[pltpu.VMEM((seq, d), F32)] * 4,
        compiler_params=pltpu.CompilerParams(
            dimension_semantics=("parallel", "parallel"), vmem_limit_bytes=VMEM_LIMIT),
        name="dilated_attn",
    )(proj, proj, proj, *[jnp.asarray(masks[n]) for n in names])


def kernel(x, even_norm, even_w_in, ssd_conv_w, ssd_conv_b, ssd_dt_bias, ssd_a_log, ssd_d, ssd_norm,
           even_w_out, odd_norm, odd_w_in, odd_w_out, final_norm):
    bsz, seq, d = x.shape
    t = bsz * seq
    x2d = x.reshape(t, d)

    w = even_w_in[0]
    z_end = SSD_INNER
    xbc_end = z_end + SSD_CONV_DIM
    dt_end = xbc_end + SSD_HEADS
    q_end = dt_end + MOBA_INNER
    v_end = q_end + 2 * MOBA_INNER
    x_end = z_end + SSD_INNER
    moba_scale = MOBA_HEAD_DIM ** -0.5 * math.log2(math.e)
    w_main = jnp.concatenate([w[:, :x_end], w[:, v_end:], w[:, dt_end:q_end] * moba_scale,
                              w[:, q_end:v_end], w[:, x_end:xbc_end]],
                             axis=1).astype(BF16)
    w_dt = jnp.pad(w[:, xbc_end:dt_end], ((0, 0), (0, LANES - SSD_HEADS))).astype(BF16)
    proj0, dt_raw = _norm_proj(x2d, even_norm[0], w_main, w_dt)
    y_ssd = _ssd(proj0, dt_raw, ssd_conv_w[0], ssd_conv_b[0], ssd_dt_bias[0], ssd_a_log[0],
                 ssd_d[0], ssd_norm[0], bsz, seq)
    o_moba = _moba(proj0, bsz, seq)
    x1 = _out_proj([y_ssd], o_moba, proj0, P0_G, x2d, even_w_out[0].astype(BF16))

    w1 = odd_w_in[0]
    dil_scale = DIL_HEAD_DIM ** -0.5 * math.log2(math.e)
    w1_main = jnp.concatenate([w1[:, :DIL_INNER] * dil_scale, w1[:, DIL_INNER:]], axis=1).astype(BF16)
    proj1 = _norm_proj(x1, odd_norm[0], w1_main, tm=seq, phases=DIL_PHASES, phase_cols=3 * DIL_INNER)
    o_dil = _dilated(proj1, bsz, seq)
    out = _out_proj([], o_dil, proj1, 3 * DIL_INNER, x1, odd_w_out[0].astype(BF16), final_norm)
    return out.reshape(bsz, seq, d)
```

```python
import functools
import math

import jax
import jax.numpy as jnp
from jax import lax
from jax.experimental import pallas as pl
from jax.experimental.pallas import tpu as pltpu

F32 = jnp.float32
BF16 = jnp.bfloat16

NORM_EPS = 1e-5
D_MODEL = 1024

SSD_HEADS = 16
SSD_HEAD_DIM = 64
SSD_INNER = SSD_HEADS * SSD_HEAD_DIM
SSD_GROUPS = 2
SSD_STATE = 128
SSD_CONV = 4
SSD_CHUNK = 128
SSD_BC = 2 * SSD_GROUPS * SSD_STATE
SSD_CONV_DIM = SSD_INNER + SSD_BC

MOBA_HEADS = 8
MOBA_HEAD_DIM = 128
MOBA_INNER = MOBA_HEADS * MOBA_HEAD_DIM
MOBA_BLOCK = 256
MOBA_TOPK = 3

DIL_HEADS = 16
DIL_HEAD_DIM = 128
DIL_INNER = DIL_HEADS * DIL_HEAD_DIM
DIL_QBLOCK = 128
DIL_PATTERNS = ((128, 1), (512, 4), (2048, 16))

LANES = 128
NEG_BIG = -1e30
VMEM_LIMIT = 56 * 1024 * 1024

P0_Z = 0
P0_X = P0_Z + SSD_INNER
P0_G = P0_X + SSD_INNER
P0_Q = P0_G + MOBA_INNER
P0_K = P0_Q + MOBA_INNER
P0_V = P0_K + MOBA_INNER
P0_BC = P0_V + MOBA_INNER
P0_N = P0_BC + SSD_BC


def _nt_dot(a, b):
    return lax.dot_general(a, b, (((1,), (1,)), ((), ())), preferred_element_type=F32)


def _dot(a, b):
    return jnp.dot(a, b, preferred_element_type=F32)


def _sigmoid(x):
    return 1.0 / (1.0 + jnp.exp(-x))


def _norm_proj_body(*refs, has_aux, row_chunk, phases, n_phase_tiles):
    if has_aux:
        x_ref, g_ref, w_ref, waux_ref, o_ref, oaux_ref, hn_ref = refs
    elif phases:
        x_ref, g_ref, w_ref, o_ref, hn_ref, hnp_ref, hnf_ref = refs
    else:
        x_ref, g_ref, w_ref, o_ref, hn_ref = refs
    j = pl.program_id(1)
    tm = x_ref.shape[0]

    @pl.when(j == 0)
    def _():
        g = g_ref[...]

        def norm_rows(c, carry):
            r0 = pl.multiple_of(c * row_chunk, row_chunk)
            x = x_ref[pl.ds(r0, row_chunk), :]
            ms = jnp.mean(x * x, axis=-1, keepdims=True)
            hn = x * lax.rsqrt(ms + NORM_EPS) * g
            hn_ref[pl.ds(r0, row_chunk), :] = hn.astype(BF16)
            if phases:
                for cc in range(x.shape[1] // LANES):
                    hnf_ref[cc, pl.ds(r0, row_chunk), :] = hn[:, cc * LANES:(cc + 1) * LANES]
            return carry

        lax.fori_loop(0, tm // row_chunk, norm_rows, 0)
        if phases:
            per = tm // phases
            for p in range(phases):
                for cc in range(hnf_ref.shape[0]):
                    hnp_ref[p * per:(p + 1) * per, cc * LANES:(cc + 1) * LANES] = (
                        hnf_ref[cc, pl.ds(p, per, stride=phases), :].astype(BF16))
        if has_aux:
            oaux_ref[...] = _dot(hn_ref[...], waux_ref[...])

    def project(src_ref):
        sub = min(tm, 512)
        for r0 in range(0, tm, sub):
            o_ref[r0:r0 + sub, :] = _dot(src_ref[r0:r0 + sub, :], w_ref[...]).astype(o_ref.dtype)

    if phases:
        @pl.when(j < n_phase_tiles)
        def _():
            project(hnp_ref)

        @pl.when(j >= n_phase_tiles)
        def _():
            project(hn_ref)
    else:
        project(hn_ref)


def _norm_proj(x2d, gain, w, w_aux=None, *, tm=1024, tn=512, phases=0, phase_cols=0):
    t, d = x2d.shape
    n = w.shape[1]
    has_aux = w_aux is not None
    in_specs = [pl.BlockSpec((tm, d), lambda i, j: (i, 0)),
                pl.BlockSpec((1, d), lambda i, j: (0, 0)),
                pl.BlockSpec((d, tn), lambda i, j: (0, j))]
    out_specs = [pl.BlockSpec((tm, tn), lambda i, j: (i, j))]
    out_shape = [jax.ShapeDtypeStruct((t, n), BF16)]
    args = [x2d, gain.reshape(1, d).astype(F32), w]
    if has_aux:
        in_specs.append(pl.BlockSpec((d, LANES), lambda i, j: (0, 0)))
        out_specs.append(pl.BlockSpec((tm, LANES), lambda i, j: (i, 0)))
        out_shape.append(jax.ShapeDtypeStruct((t, LANES), F32))
        args.append(w_aux)
    outs = pl.pallas_call(
        functools.partial(_norm_proj_body, has_aux=has_aux, row_chunk=256, phases=phases,
                          n_phase_tiles=phase_cols // tn),
        grid=(t // tm, n // tn),
        in_specs=in_specs,
        out_specs=out_specs,
        out_shape=out_shape,
        scratch_shapes=[pltpu.VMEM((tm, d), BF16)] * (2 if phases else 1)
        + ([pltpu.VMEM((d // LANES, tm, LANES), F32)] if phases else []),
        compiler_params=pltpu.CompilerParams(
            dimension_semantics=("parallel", "arbitrary"), vmem_limit_bytes=VMEM_LIMIT),
        name="norm_proj_aux" if has_aux else "norm_proj",
    )(*args)
    return outs if has_aux else outs[0]


def _split3(a):
    hi = a.astype(BF16)
    r = a - hi.astype(F32)
    mid = r.astype(BF16)
    lo = (r - mid.astype(F32)).astype(BF16)
    return hi, mid, lo


def _expand_heads(a, e):
    hi, mid, lo = _split3(a)
    return _dot(hi, e) + _dot(mid, e) + _dot(lo, e)


def _ssd_body(z_ref, xs_ref, bc_ref, dt_ref, cw_ref, cb_ref, dtb_ref, alog_ref, dskip_ref, nrm_ref,
              e_ref, tri_ref, y_ref, state_ref, ubuf_ref):
    c = pl.program_id(1)
    L = SSD_CHUNK
    tail = 8

    @pl.when(c == 0)
    def _():
        state_ref[...] = jnp.zeros_like(state_ref)
        ubuf_ref[0:tail, :] = jnp.zeros((tail, SSD_CONV_DIM), F32)

    row = lax.broadcasted_iota(jnp.int32, (L, L), 0)
    col = lax.broadcasted_iota(jnp.int32, (L, L), 1)
    causal = col <= row
    lane = lax.broadcasted_iota(jnp.int32, (L, LANES), 1)
    low_half = lane < SSD_HEAD_DIM
    heads_per_group = SSD_HEADS // SSD_GROUPS
    gw = heads_per_group * SSD_HEAD_DIM

    def chunk(r):
        u = jnp.concatenate([xs_ref[r, :], bc_ref[r, :]], axis=1).astype(F32)
        ubuf_ref[tail:tail + L, :] = u
        cw = cw_ref[...]
        acc = cb_ref[...] + cw[SSD_CONV - 1:SSD_CONV, :] * u
        for k in range(SSD_CONV - 1):
            acc = acc + cw[k:k + 1, :] * ubuf_ref[pl.ds(tail - (SSD_CONV - 1) + k, L), :]
        ubuf_ref[0:tail, :] = u[L - tail:L, :]
        act = acc * _sigmoid(acc)

        xs = act[:, :SSD_INNER]

        dtr = dt_ref[r, :] + dtb_ref[...]
        dt = jnp.maximum(dtr, 0.0) + jnp.log1p(jnp.exp(-jnp.abs(dtr)))
        a = -jnp.exp(alog_ref[...])
        ac = dt * a
        tri = tri_ref[...]
        hi, mid, lo = _split3(ac)
        a_cum = _dot(tri, hi) + _dot(tri, mid) + _dot(tri, lo)
        a_last = a_cum[L - 1:L, :]
        dec_states = jnp.exp(a_last - a_cum)
        exp_acum = jnp.exp(a_cum)
        e = e_ref[...]
        dt_e = _expand_heads(dt, e)
        dec_e = _expand_heads(dec_states, e)
        ea_e = _expand_heads(exp_acum, e)
        a_cum_t = a_cum.T

        xc = xs * dt_e
        xc_b = xc.astype(BF16)
        xdec_b = (xc * dec_e).astype(BF16)

        for g in range(SSD_GROUPS):
            b_g = act[:, SSD_INNER + g * SSD_STATE:SSD_INNER + (g + 1) * SSD_STATE]
            c_g = act[:, SSD_INNER + (SSD_GROUPS + g) * SSD_STATE:
                      SSD_INNER + (SSD_GROUPS + g + 1) * SSD_STATE]
            b_gb = b_g.astype(BF16)
            c_gb = c_g.astype(BF16)
            scores = _nt_dot(c_gb, b_gb)
            gs = slice(g * gw, (g + 1) * gw)

            prev = state_ref[:, gs]
            y_off = _dot(c_gb, prev.astype(BF16))
            st_new = _dot(b_g.T.astype(BF16), xdec_b[:, gs])
            state_ref[:, gs] = prev * ea_e[L - 1:L, gs] + st_new

            pieces = []
            for pair in range(heads_per_group // 2):
                ms = []
                for hh in range(2):
                    h = g * heads_per_group + 2 * pair + hh
                    seg = a_cum[:, h:h + 1] - a_cum_t[h:h + 1, :]
                    lmat = jnp.exp(jnp.where(causal, seg, NEG_BIG))
                    ms.append((scores * lmat).astype(BF16))
                m_pair = jnp.concatenate(ms, axis=1)
                cs = slice(g * gw + pair * LANES, g * gw + (pair + 1) * LANES)
                x_pair = xc_b[:, cs]
                zero = jnp.zeros_like(x_pair)
                rhs = jnp.concatenate([jnp.where(low_half, x_pair, zero),
                                       jnp.where(low_half, zero, x_pair)], axis=0)
                pieces.append(_dot(m_pair, rhs))
            y_diag = jnp.concatenate(pieces, axis=1)

            y = y_diag + y_off * ea_e[:, gs] + xs[:, gs] * dskip_ref[:, gs]
            zg = z_ref[r, gs].astype(F32)
            ug = y * (zg * _sigmoid(zg))
            ug = ug * lax.rsqrt(jnp.mean(ug * ug, axis=-1, keepdims=True) + NORM_EPS)
            y_ref[r, gs] = (ug * nrm_ref[:, gs]).astype(y_ref.dtype)

    for ci in range(xs_ref.shape[0] // L):
        chunk(slice(ci * L, (ci + 1) * L))


def _ssd(proj, dt_raw, conv_w, conv_b, dt_bias, a_log, d_skip, ssd_norm, bsz, seq, *, chunks_per_step=2):
    t = bsz * seq
    L = SSD_CHUNK
    R = chunks_per_step * L
    nc = seq // R

    def pad_heads(v):
        return jnp.pad(v.astype(F32), (0, LANES - SSD_HEADS)).reshape(1, LANES)

    head_of_chan = jnp.arange(SSD_INNER, dtype=jnp.int32) // SSD_HEAD_DIM
    expand = (jnp.arange(LANES, dtype=jnp.int32)[:, None] == head_of_chan[None, :]).astype(BF16)
    tri = (jnp.arange(L)[:, None] >= jnp.arange(L)[None, :]).astype(BF16)
    d_chan = jnp.repeat(d_skip.astype(F32), SSD_HEAD_DIM).reshape(1, SSD_INNER)

    def rows(b, c):
        return b * nc + c

    const = lambda b, c: (0, 0)
    return pl.pallas_call(
        _ssd_body,
        grid=(bsz, nc),
        in_specs=[
            pl.BlockSpec((R, SSD_INNER), lambda b, c: (rows(b, c), P0_Z // SSD_INNER)),
            pl.BlockSpec((R, SSD_INNER), lambda b, c: (rows(b, c), P0_X // SSD_INNER)),
            pl.BlockSpec((R, SSD_BC), lambda b, c: (rows(b, c), P0_BC // SSD_BC)),
            pl.BlockSpec((R, LANES), lambda b, c: (rows(b, c), 0)),
            pl.BlockSpec((SSD_CONV, SSD_CONV_DIM), const),
            pl.BlockSpec((1, SSD_CONV_DIM), const),
            pl.BlockSpec((1, LANES), const),
            pl.BlockSpec((1, LANES), const),
            pl.BlockSpec((1, SSD_INNER), const),
            pl.BlockSpec((1, SSD_INNER), const),
            pl.BlockSpec((LANES, SSD_INNER), const),
            pl.BlockSpec((L, L), const),
        ],
        out_specs=pl.BlockSpec((R, SSD_INNER), lambda b, c: (rows(b, c), 0)),
        out_shape=jax.ShapeDtypeStruct((t, SSD_INNER), BF16),
        scratch_shapes=[pltpu.VMEM((SSD_STATE, SSD_INNER), F32),
                        pltpu.VMEM((8 + L, SSD_CONV_DIM), F32)],
        compiler_params=pltpu.CompilerParams(
            dimension_semantics=("parallel", "arbitrary"), vmem_limit_bytes=VMEM_LIMIT),
        name="ssd_mixer",
    )(proj, proj, proj, dt_raw, conv_w.astype(F32), conv_b.reshape(1, -1).astype(F32),
      pad_heads(dt_bias), pad_heads(a_log), d_chan, ssd_norm.reshape(1, -1).astype(F32), expand, tri)


def _moba_body(q_ref, k_ref, v_ref, avg_ref, hot_ref, eye_ref, o_ref, *, nb, q_tile):
    blk = MOBA_BLOCK
    n_dense = MOBA_TOPK + 1

    kmean = _dot(avg_ref[...], k_ref[...]).astype(BF16)
    jrow = lax.broadcasted_iota(jnp.int32, (nb, blk), 0)
    row = lax.broadcasted_iota(jnp.int32, (q_tile, blk), 0)
    col = lax.broadcasted_iota(jnp.int32, (q_tile, blk), 1)

    def query_block(i):
        q_i = q_ref[i * blk:(i + 1) * blk, :]
        if i < n_dense:
            return q_i, None
        gate_t = _nt_dot(kmean, q_i)[0:nb, :]
        valid = jrow < i
        gm = jnp.where(valid, gate_t, -jnp.inf)
        rank = jnp.zeros((nb, blk), F32)
        for r in range(1, nb):
            gr = pltpu.roll(gm, r, axis=0)
            lower = ((jrow - r) & (nb - 1)) < jrow
            rank = rank + jnp.where(gr > gm, 1.0, 0.0) + jnp.where((gr == gm) & lower, 1.0, 0.0)
        keep = valid & (rank < float(MOBA_TOPK))
        bias_t = jnp.where(keep, 0.0, NEG_BIG)
        bias_t = jnp.concatenate([bias_t, jnp.zeros((LANES - nb, blk), F32)], axis=0).astype(BF16)
        bias = _nt_dot(eye_ref[...], bias_t).astype(BF16)
        return q_i, jnp.concatenate([q_i, bias], axis=1)

    def scores(i, h, q_i, q_aug):
        rs = slice(h * q_tile, (h + 1) * q_tile)
        s_own = _nt_dot(q_i[rs, :], k_ref[i * blk:(i + 1) * blk, :])
        s_own = jnp.where(col <= row + h * q_tile, s_own, NEG_BIG)
        if i == 0:
            return s_own, None
        if q_aug is None:
            return s_own, _nt_dot(q_i[rs, :], k_ref[0:i * blk, :])
        k_past = jnp.concatenate([k_ref[0:i * blk, :], hot_ref[0:i * blk, :]], axis=1)
        return s_own, _nt_dot(q_aug[rs, :], k_past)

    def finish(i, h, s_own, s_past):
        m = jnp.max(s_own, axis=-1, keepdims=True)
        if s_past is None:
            p = jnp.exp2(s_own - m)
        else:
            m = jnp.maximum(m, jnp.max(s_past, axis=-1, keepdims=True))
            p = jnp.concatenate([jnp.exp2(s_past - m), jnp.exp2(s_own - m)], axis=1)
        l = jnp.sum(p, axis=-1, keepdims=True)
        acc = _dot(p.astype(BF16), v_ref[0:(i + 1) * blk, :])
        o_ref[i * blk + h * q_tile:i * blk + (h + 1) * q_tile, :] = (acc / l).astype(o_ref.dtype)

    tiles = [(i, h) for i in range(nb) for h in range(blk // q_tile)]
    qcache = {}
    pending = None
    for i, h in tiles:
        if i not in qcache:
            qcache = {i: query_block(i)}
        s_next = scores(i, h, *qcache[i])
        if pending is not None:
            finish(*pending)
        pending = (i, h) + s_next
    finish(*pending)


def _moba(proj, bsz, seq):
    t = bsz * seq
    blk = MOBA_BLOCK
    nb = seq // blk
    d = MOBA_HEAD_DIM
    blk_of_key = jnp.arange(seq, dtype=jnp.int32) // blk
    avg = (jnp.arange(2 * nb, dtype=jnp.int32)[:, None] == blk_of_key[None, :]).astype(BF16) / blk
    hot = (blk_of_key[:, None] == jnp.arange(LANES, dtype=jnp.int32)[None, :]).astype(BF16)
    eye = jnp.eye(blk, dtype=BF16)

    def head_block(base):
        return pl.BlockSpec((seq, d), lambda b, h: (b, base // d + h))

    const = lambda b, h: (0, 0)
    return pl.pallas_call(
        functools.partial(_moba_body, nb=nb, q_tile=256),
        grid=(bsz, MOBA_HEADS),
        in_specs=[head_block(P0_Q), head_block(P0_K), head_block(P0_V),
                  pl.BlockSpec((2 * nb, seq), const), pl.BlockSpec((seq, LANES), const),
                  pl.BlockSpec((blk, blk), const)],
        out_specs=pl.BlockSpec((seq, d), lambda b, h: (b, h)),
        out_shape=jax.ShapeDtypeStruct((t, MOBA_INNER), BF16),
        compiler_params=pltpu.CompilerParams(
            dimension_semantics=("parallel", "parallel"), vmem_limit_bytes=VMEM_LIMIT),
        name="moba_attn",
    )(proj, proj, proj, avg, hot, eye)


def _out_proj_body(*refs, n_plain, final_norm):
    plain = refs[:n_plain]
    a_ref, g_ref, x_ref, w_ref = refs[n_plain:n_plain + 4]
    o_ref = refs[-1]
    g = g_ref[...].astype(F32)
    gated = (a_ref[...].astype(F32) * (g * _sigmoid(g))).astype(BF16)
    k0 = 0
    y = x_ref[...]
    for r in plain:
        kw = r.shape[1]
        y = y + _dot(r[...], w_ref[k0:k0 + kw, :])
        k0 += kw
    y = y + _dot(gated, w_ref[k0:k0 + gated.shape[1], :])
    if final_norm:
        fn_ref = refs[-2]
        y = y * lax.rsqrt(jnp.mean(y * y, axis=-1, keepdims=True) + NORM_EPS) * fn_ref[...]
    o_ref[...] = y


def _out_proj(plain, gated, gate_src, gate_col, x2d, w, final_gain=None, *, tm=512):
    t, d = x2d.shape
    kg = gated.shape[1]
    in_specs = [pl.BlockSpec((tm, a.shape[1]), lambda i: (i, 0)) for a in plain]
    in_specs += [pl.BlockSpec((tm, kg), lambda i: (i, 0)),
                 pl.BlockSpec((tm, kg), lambda i: (i, gate_col // kg)),
                 pl.BlockSpec((tm, d), lambda i: (i, 0)),
                 pl.BlockSpec(w.shape, lambda i: (0, 0))]
    args = list(plain) + [gated, gate_src, x2d, w]
    if final_gain is not None:
        in_specs.append(pl.BlockSpec((1, d), lambda i: (0, 0)))
        args.append(final_gain.reshape(1, d).astype(F32))
    return pl.pallas_call(
        functools.partial(_out_proj_body, n_plain=len(plain), final_norm=final_gain is not None),
        grid=(t // tm,),
        in_specs=in_specs,
        out_specs=pl.BlockSpec((tm, d), lambda i: (i, 0)),
        out_shape=jax.ShapeDtypeStruct((t, d), F32),
        compiler_params=pltpu.CompilerParams(
            dimension_semantics=("parallel",), vmem_limit_bytes=VMEM_LIMIT),
        name="out_proj_final" if final_gain is not None else "out_proj",
    )(*args)


DIL_PHASES = 8


def _dil_tiles(seq):
    ph, qb = DIL_PHASES, DIL_QBLOCK
    per = seq // ph
    tiles = []
    rows = qb // ph
    for n in range(seq // qb):
        qs = [(p * per + rows * n, rows) for p in range(ph)]
        if n == 0:
            ks, mask = [(p * per, rows) for p in range(ph)], "a_first"
        else:
            ks, mask = [(p * per + rows * (n - 1), 2 * rows) for p in range(ph)], "a"
        tiles.append((qs, ks, mask, "init"))
    seg = ph // 4
    rows = qb // seg
    for p4 in range(4):
        for n in range(seq // 4 // qb):
            qs = [((p4 + 4 * j) * per + rows * n, rows) for j in range(seg)]
            if n == 0:
                ks, mask = [((p4 + 4 * j) * per, rows) for j in range(seg)], "b_first"
            else:
                ks, mask = [((p4 + 4 * j) * per + rows * (n - 1), 2 * rows) for j in range(seg)], "b"
            tiles.append((qs, ks, mask, "merge"))
    for p in range(ph):
        for h in range(per // qb):
            qs = [(p * per + qb * h, qb)]
            ks = [(p * per, qb * (h + 1))]
            tiles.append((qs, ks, "c%d" % h, "final"))
    return tiles


def _dil_masks(seq):
    import numpy as np
    ph, qb = DIL_PHASES, DIL_QBLOCK
    per = seq // ph

    def pos(n_rows, seg_rows, step):
        i = np.arange(n_rows)
        return step * (i % seg_rows) + i // seg_rows

    def band(qpos, kpos, span):
        dist = qpos[:, None] - kpos[None, :]
        return np.where((dist >= 0) & (dist <= span), 0.0, NEG_BIG).astype(np.float32)

    masks = {}
    for name, seg in (("a", ph), ("b", ph // 4)):
        rows = qb // seg
        masks[name] = band(qb + pos(qb, rows, seg), pos(2 * qb, 2 * rows, seg), qb)
        masks[name + "_first"] = band(pos(qb, rows, seg), pos(qb, rows, seg), qb)
    for h in range(per // qb):
        lq = qb * h + np.arange(qb)
        lk = np.arange(qb * (h + 1))
        same = (lq[:, None] - lk[None, :]) % 2 == 0
        masks["c%d" % h] = np.where(same & (lk[None, :] <= lq[:, None]), 0.0, NEG_BIG).astype(np.float32)
    return masks


def _dil_body(*refs, seq, mask_names, group_size):
    q_ref, k_ref, v_ref = refs[:3]
    mask_refs = dict(zip(mask_names, refs[3:3 + len(mask_names)]))
    o_ref, acc_ref, m_ref, l_ref, onat_ref = refs[3 + len(mask_names):]
    qb = DIL_QBLOCK
    d = DIL_HEAD_DIM

    def rows_of(ref, slices):
        parts = [ref[s:s + n, :] for s, n in slices]
        return parts[0] if len(parts) == 1 else jnp.concatenate(parts, axis=0)

    def put_rows(ref, slices, val):
        r0 = 0
        for s, n in slices:
            ref[s:s + n, :] = val[r0:r0 + n, :]
            r0 += n

    def cat(parts):
        return parts[0] if len(parts) == 1 else jnp.concatenate(parts, axis=0)

    def scores(group):
        mask = mask_refs[group[0][2]][...]
        return cat([_nt_dot(rows_of(q_ref, qs), rows_of(k_ref, ks)) + mask for qs, ks, _, _ in group])

    def probs(group, s):
        stage = group[0][3]
        m_blk = jnp.max(s, axis=-1, keepdims=True)
        if stage == "init":
            m_old, m_new = None, m_blk
        else:
            m_old = cat([rows_of(m_ref, qs) for qs, _, _, _ in group])
            m_new = jnp.maximum(m_old[:, 0:1], m_blk)
        return jnp.exp2(s - m_new).astype(BF16), m_new, m_old

    def accumulate(group, p, m_new, m_old):
        stage = group[0][3]
        pvs = []
        for g, (_, ks, _, _) in enumerate(group):
            vb = rows_of(v_ref, ks)
            pvs.append(_dot(p[g * qb:(g + 1) * qb, :], jnp.concatenate([vb, jnp.ones_like(vb)], axis=1)))
        pv = cat(pvs)
        acc, l = pv[:, :d], pv[:, d:]
        m_b = jnp.broadcast_to(m_new, acc.shape)
        if stage != "init":
            alpha = jnp.exp2(m_old - m_b)
            acc = alpha * cat([rows_of(acc_ref, qs) for qs, _, _, _ in group]) + acc
            l = alpha * cat([rows_of(l_ref, qs) for qs, _, _, _ in group]) + l
        if stage == "final":
            out = acc / l
        for g, (qs, _, _, _) in enumerate(group):
            rs = slice(g * qb, (g + 1) * qb)
            if stage == "final":
                (s0, _), = qs
                p_idx, l0 = divmod(s0, seq // DIL_PHASES)
                onat_ref[pl.ds(DIL_PHASES * l0 + p_idx, qb, stride=DIL_PHASES), :] = out[rs, :]
            else:
                put_rows(acc_ref, qs, acc[rs, :])
                put_rows(l_ref, qs, l[rs, :])
                put_rows(m_ref, qs, m_b[rs, :])

    groups = []
    for tile in _dil_tiles(seq):
        if groups and len(groups[-1]) < group_size and groups[-1][0][2:] == tile[2:]:
            groups[-1].append(tile)
        else:
            groups.append([tile])

    def owned(group):
        return {r for qs, _, _, _ in group for s0, n in qs for r in range(s0, s0 + n)}

    for ga, gb in zip(groups, groups[1:]):
        assert not owned(ga) & owned(gb)
    scored, ready = None, None
    for group in groups + [None, None]:
        s_new = None if group is None else (group, scores(group))
        p_new = None if scored is None else (scored[0],) + probs(*scored)
        if ready is not None:
            accumulate(*ready)
        scored, ready = s_new, p_new
    o_ref[...] = onat_ref[...].astype(o_ref.dtype)


def _dilated(proj, bsz, seq):
    t = bsz * seq
    d = DIL_HEAD_DIM
    masks = _dil_masks(seq)
    names = tuple(sorted(masks))

    def head_block(base):
        return pl.BlockSpec((seq, d), lambda b, h: (b, base // d + h))

    return pl.pallas_call(
        functools.partial(_dil_body, seq=seq, mask_names=names, group_size=1),
        grid=(bsz, DIL_HEADS),
        in_specs=[head_block(0), head_block(DIL_INNER), head_block(2 * DIL_INNER)]
        + [pl.BlockSpec(masks[n].shape, lambda b, h: (0, 0)) for n in names],
        out_specs=pl.BlockSpec((seq, d), lambda b, h: (b, h)),
        out_shape=jax.ShapeDtypeStruct((t, DIL_INNER), BF16),
        scratch_shapes=[pltpu.VMEM((seq, d), F32)] * 4,
        compiler_params=pltpu.CompilerParams(
            dimension_semantics=("parallel", "parallel"), vmem_limit_bytes=VMEM_LIMIT),
        name="dilated_attn",
    )(proj, proj, proj, *[jnp.asarray(masks[n]) for n in names])


def kernel(x, even_norm, even_w_in, ssd_conv_w, ssd_conv_b, ssd_dt_bias, ssd_a_log, ssd_d, ssd_norm,
           even_w_out, odd_norm, odd_w_in, odd_w_out, final_norm):
    bsz, seq, d = x.shape
    t = bsz * seq
    x2d = x.reshape(t, d)

    w = even_w_in[0]
    z_end = SSD_INNER
    xbc_end = z_end + SSD_CONV_DIM
    dt_end = xbc_end + SSD_HEADS
    q_end = dt_end + MOBA_INNER
    v_end = q_end + 2 * MOBA_INNER
    x_end = z_end + SSD_INNER
    moba_scale = MOBA_HEAD_DIM ** -0.5 * math.log2(math.e)
    w_main = jnp.concatenate([w[:, :x_end], w[:, v_end:], w[:, dt_end:q_end] * moba_scale,
                              w[:, q_end:v_end], w[:, x_end:xbc_end]],
                             axis=1).astype(BF16)
    w_dt = jnp.pad(w[:, xbc_end:dt_end], ((0, 0), (0, LANES - SSD_HEADS))).astype(BF16)
    proj0, dt_raw = _norm_proj(x2d, even_norm[0], w_main, w_dt, tm=seq, tn=P0_N // 4)
    y_ssd = _ssd(proj0, dt_raw, ssd_conv_w[0], ssd_conv_b[0], ssd_dt_bias[0], ssd_a_log[0],
                 ssd_d[0], ssd_norm[0], bsz, seq)
    o_moba = _moba(proj0, bsz, seq)
    x1 = _out_proj([y_ssd], o_moba, proj0, P0_G, x2d, even_w_out[0].astype(BF16))

    w1 = odd_w_in[0]
    dil_scale = DIL_HEAD_DIM ** -0.5 * math.log2(math.e)
    w1_main = jnp.concatenate([w1[:, :DIL_INNER] * dil_scale, w1[:, DIL_INNER:]], axis=1).astype(BF16)
    proj1 = _norm_proj(x1, odd_norm[0], w1_main, tm=seq, tn=1024, phases=DIL_PHASES,
                       phase_cols=3 * DIL_INNER)
    o_dil = _dilated(proj1, bsz, seq)
    out = _out_proj([], o_dil, proj1, 3 * DIL_INNER, x1, odd_w_out[0].astype(BF16), final_norm)
    return out.reshape(bsz, seq, d)
```

```python
import functools
import math

import jax
import jax.numpy as jnp
from jax import lax
from jax.experimental import pallas as pl
from jax.experimental.pallas import tpu as pltpu

F32 = jnp.float32
BF16 = jnp.bfloat16

NORM_EPS = 1e-5
D_MODEL = 1024

SSD_HEADS = 16
SSD_HEAD_DIM = 64
SSD_INNER = SSD_HEADS * SSD_HEAD_DIM
SSD_GROUPS = 2
SSD_STATE = 128
SSD_CONV = 4
SSD_CHUNK = 128
SSD_BC = 2 * SSD_GROUPS * SSD_STATE
SSD_CONV_DIM = SSD_INNER + SSD_BC

MOBA_HEADS = 8
MOBA_HEAD_DIM = 128
MOBA_INNER = MOBA_HEADS * MOBA_HEAD_DIM
MOBA_BLOCK = 256
MOBA_TOPK = 3

DIL_HEADS = 16
DIL_HEAD_DIM = 128
DIL_INNER = DIL_HEADS * DIL_HEAD_DIM
DIL_QBLOCK = 128
DIL_PATTERNS = ((128, 1), (512, 4), (2048, 16))

LANES = 128
NEG_BIG = -1e30
VMEM_LIMIT = 56 * 1024 * 1024

P0_Z = 0
P0_X = P0_Z + SSD_INNER
P0_G = P0_X + SSD_INNER
P0_Q = P0_G + MOBA_INNER
P0_K = P0_Q + MOBA_INNER
P0_V = P0_K + MOBA_INNER
P0_BC = P0_V + MOBA_INNER
P0_N = P0_BC + SSD_BC


def _nt_dot(a, b):
    return lax.dot_general(a, b, (((1,), (1,)), ((), ())), preferred_element_type=F32)


def _dot(a, b):
    return jnp.dot(a, b, preferred_element_type=F32)


def _sigmoid(x):
    return 1.0 / (1.0 + jnp.exp(-x))


def _norm_proj_body(*refs, has_aux, row_chunk, phases, n_phase_tiles):
    if has_aux:
        x_ref, g_ref, w_ref, waux_ref, o_ref, oaux_ref, hn_ref = refs
    elif phases:
        x_ref, g_ref, w_ref, o_ref, hn_ref, hnp_ref, hnf_ref = refs
    else:
        x_ref, g_ref, w_ref, o_ref, hn_ref = refs
    j = pl.program_id(1)
    tm = x_ref.shape[0]

    @pl.when(j == 0)
    def _():
        g = g_ref[...]

        def norm_rows(c, carry):
            r0 = pl.multiple_of(c * row_chunk, row_chunk)
            x = x_ref[pl.ds(r0, row_chunk), :]
            ms = jnp.mean(x * x, axis=-1, keepdims=True)
            hn = x * lax.rsqrt(ms + NORM_EPS) * g
            hn_ref[pl.ds(r0, row_chunk), :] = hn.astype(BF16)
            if phases:
                for cc in range(x.shape[1] // LANES):
                    hnf_ref[cc, pl.ds(r0, row_chunk), :] = hn[:, cc * LANES:(cc + 1) * LANES]
            return carry

        lax.fori_loop(0, tm // row_chunk, norm_rows, 0)
        if phases:
            per = tm // phases
            for p in range(phases):
                for cc in range(hnf_ref.shape[0]):
                    hnp_ref[p * per:(p + 1) * per, cc * LANES:(cc + 1) * LANES] = (
                        hnf_ref[cc, pl.ds(p, per, stride=phases), :].astype(BF16))
        if has_aux:
            oaux_ref[...] = _dot(hn_ref[...], waux_ref[...])

    def project(src_ref):
        sub = min(tm, 512)
        for r0 in range(0, tm, sub):
            o_ref[r0:r0 + sub, :] = _dot(src_ref[r0:r0 + sub, :], w_ref[...]).astype(o_ref.dtype)

    if phases:
        @pl.when(j < n_phase_tiles)
        def _():
            project(hnp_ref)

        @pl.when(j >= n_phase_tiles)
        def _():
            project(hn_ref)
    else:
        project(hn_ref)


def _norm_proj(x2d, gain, w, w_aux=None, *, tm=1024, tn=512, phases=0, phase_cols=0):
    t, d = x2d.shape
    n = w.shape[1]
    has_aux = w_aux is not None
    in_specs = [pl.BlockSpec((tm, d), lambda i, j: (i, 0)),
                pl.BlockSpec((1, d), lambda i, j: (0, 0)),
                pl.BlockSpec((d, tn), lambda i, j: (0, j))]
    out_specs = [pl.BlockSpec((tm, tn), lambda i, j: (i, j))]
    out_shape = [jax.ShapeDtypeStruct((t, n), BF16)]
    args = [x2d, gain.reshape(1, d).astype(F32), w]
    if has_aux:
        in_specs.append(pl.BlockSpec((d, LANES), lambda i, j: (0, 0)))
        out_specs.append(pl.BlockSpec((tm, LANES), lambda i, j: (i, 0)))
        out_shape.append(jax.ShapeDtypeStruct((t, LANES), F32))
        args.append(w_aux)
    outs = pl.pallas_call(
        functools.partial(_norm_proj_body, has_aux=has_aux, row_chunk=256, phases=phases,
                          n_phase_tiles=phase_cols // tn),
        grid=(t // tm, n // tn),
        in_specs=in_specs,
        out_specs=out_specs,
        out_shape=out_shape,
        scratch_shapes=[pltpu.VMEM((tm, d), BF16)] * (2 if phases else 1)
        + ([pltpu.VMEM((d // LANES, tm, LANES), F32)] if phases else []),
        compiler_params=pltpu.CompilerParams(
            dimension_semantics=("parallel", "arbitrary"), vmem_limit_bytes=VMEM_LIMIT),
        name="norm_proj_aux" if has_aux else "norm_proj",
    )(*args)
    return outs if has_aux else outs[0]


def _split3(a):
    hi = a.astype(BF16)
    r = a - hi.astype(F32)
    mid = r.astype(BF16)
    lo = (r - mid.astype(F32)).astype(BF16)
    return hi, mid, lo


def _expand_heads(a, e):
    hi, mid, lo = _split3(a)
    return _dot(hi, e) + _dot(mid, e) + _dot(lo, e)


def _ssd_body(z_ref, xs_ref, bc_ref, dt_ref, cw_ref, cb_ref, dtb_ref, alog_ref, dskip_ref, nrm_ref,
              e_ref, tri_ref, y_ref, state_ref, ubuf_ref):
    c = pl.program_id(1)
    L = SSD_CHUNK
    tail = 8

    @pl.when(c == 0)
    def _():
        state_ref[...] = jnp.zeros_like(state_ref)
        ubuf_ref[0:tail, :] = jnp.zeros((tail, SSD_CONV_DIM), F32)

    row = lax.broadcasted_iota(jnp.int32, (L, L), 0)
    col = lax.broadcasted_iota(jnp.int32, (L, L), 1)
    causal = col <= row
    lane = lax.broadcasted_iota(jnp.int32, (L, LANES), 1)
    low_half = lane < SSD_HEAD_DIM
    heads_per_group = SSD_HEADS // SSD_GROUPS
    gw = heads_per_group * SSD_HEAD_DIM

    def chunk(r):
        u = jnp.concatenate([xs_ref[r, :], bc_ref[r, :]], axis=1).astype(F32)
        ubuf_ref[tail:tail + L, :] = u
        cw = cw_ref[...]
        acc = cb_ref[...] + cw[SSD_CONV - 1:SSD_CONV, :] * u
        for k in range(SSD_CONV - 1):
            acc = acc + cw[k:k + 1, :] * ubuf_ref[pl.ds(tail - (SSD_CONV - 1) + k, L), :]
        ubuf_ref[0:tail, :] = u[L - tail:L, :]
        act = acc * _sigmoid(acc)

        xs = act[:, :SSD_INNER]

        dtr = dt_ref[r, :] + dtb_ref[...]
        dt = jnp.maximum(dtr, 0.0) + jnp.log1p(jnp.exp(-jnp.abs(dtr)))
        a = -jnp.exp(alog_ref[...])
        ac = dt * a
        tri = tri_ref[...]
        hi, mid, lo = _split3(ac)
        a_cum = _dot(tri, hi) + _dot(tri, mid) + _dot(tri, lo)
        a_last = a_cum[L - 1:L, :]
        dec_states = jnp.exp(a_last - a_cum)
        exp_acum = jnp.exp(a_cum)
        e = e_ref[...]
        dt_e = _expand_heads(dt, e)
        dec_e = _expand_heads(dec_states, e)
        ea_e = _expand_heads(exp_acum, e)
        a_cum_t = a_cum.T

        xc = xs * dt_e
        xc_b = xc.astype(BF16)
        xdec_b = (xc * dec_e).astype(BF16)

        for g in range(SSD_GROUPS):
            b_g = act[:, SSD_INNER + g * SSD_STATE:SSD_INNER + (g + 1) * SSD_STATE]
            c_g = act[:, SSD_INNER + (SSD_GROUPS + g) * SSD_STATE:
                      SSD_INNER + (SSD_GROUPS + g + 1) * SSD_STATE]
            b_gb = b_g.astype(BF16)
            c_gb = c_g.astype(BF16)
            scores = _nt_dot(c_gb, b_gb)
            gs = slice(g * gw, (g + 1) * gw)

            prev = state_ref[:, gs]
            y_off = _dot(c_gb, prev.astype(BF16))
            st_new = _dot(b_g.T.astype(BF16), xdec_b[:, gs])
            state_ref[:, gs] = prev * ea_e[L - 1:L, gs] + st_new

            pieces = []
            for pair in range(heads_per_group // 2):
                ms = []
                for hh in range(2):
                    h = g * heads_per_group + 2 * pair + hh
                    seg = a_cum[:, h:h + 1] - a_cum_t[h:h + 1, :]
                    lmat = jnp.exp(jnp.where(causal, seg, NEG_BIG))
                    ms.append((scores * lmat).astype(BF16))
                m_pair = jnp.concatenate(ms, axis=1)
                cs = slice(g * gw + pair * LANES, g * gw + (pair + 1) * LANES)
                x_pair = xc_b[:, cs]
                zero = jnp.zeros_like(x_pair)
                rhs = jnp.concatenate([jnp.where(low_half, x_pair, zero),
                                       jnp.where(low_half, zero, x_pair)], axis=0)
                pieces.append(_dot(m_pair, rhs))
            y_diag = jnp.concatenate(pieces, axis=1)

            y = y_diag + y_off * ea_e[:, gs] + xs[:, gs] * dskip_ref[:, gs]
            zg = z_ref[r, gs].astype(F32)
            ug = y * (zg * _sigmoid(zg))
            ug = ug * lax.rsqrt(jnp.mean(ug * ug, axis=-1, keepdims=True) + NORM_EPS)
            y_ref[r, gs] = (ug * nrm_ref[:, gs]).astype(y_ref.dtype)

    for ci in range(xs_ref.shape[0] // L):
        chunk(slice(ci * L, (ci + 1) * L))


def _ssd(proj, dt_raw, conv_w, conv_b, dt_bias, a_log, d_skip, ssd_norm, bsz, seq, *, chunks_per_step=4):
    t = bsz * seq
    L = SSD_CHUNK
    R = chunks_per_step * L
    nc = seq // R

    def pad_heads(v):
        return jnp.pad(v.astype(F32), (0, LANES - SSD_HEADS)).reshape(1, LANES)

    head_of_chan = jnp.arange(SSD_INNER, dtype=jnp.int32) // SSD_HEAD_DIM
    expand = (jnp.arange(LANES, dtype=jnp.int32)[:, None] == head_of_chan[None, :]).astype(BF16)
    tri = (jnp.arange(L)[:, None] >= jnp.arange(L)[None, :]).astype(BF16)
    d_chan = jnp.repeat(d_skip.astype(F32), SSD_HEAD_DIM).reshape(1, SSD_INNER)

    def rows(b, c):
        return b * nc + c

    const = lambda b, c: (0, 0)
    return pl.pallas_call(
        _ssd_body,
        grid=(bsz, nc),
        in_specs=[
            pl.BlockSpec((R, SSD_INNER), lambda b, c: (rows(b, c), P0_Z // SSD_INNER)),
            pl.BlockSpec((R, SSD_INNER), lambda b, c: (rows(b, c), P0_X // SSD_INNER)),
            pl.BlockSpec((R, SSD_BC), lambda b, c: (rows(b, c), P0_BC // SSD_BC)),
            pl.BlockSpec((R, LANES), lambda b, c: (rows(b, c), 0)),
            pl.BlockSpec((SSD_CONV, SSD_CONV_DIM), const),
            pl.BlockSpec((1, SSD_CONV_DIM), const),
            pl.BlockSpec((1, LANES), const),
            pl.BlockSpec((1, LANES), const),
            pl.BlockSpec((1, SSD_INNER), const),
            pl.BlockSpec((1, SSD_INNER), const),
            pl.BlockSpec((LANES, SSD_INNER), const),
            pl.BlockSpec((L, L), const),
        ],
        out_specs=pl.BlockSpec((R, SSD_INNER), lambda b, c: (rows(b, c), 0)),
        out_shape=jax.ShapeDtypeStruct((t, SSD_INNER), BF16),
        scratch_shapes=[pltpu.VMEM((SSD_STATE, SSD_INNER), F32),
                        pltpu.VMEM((8 + L, SSD_CONV_DIM), F32)],
        compiler_params=pltpu.CompilerParams(
            dimension_semantics=("parallel", "arbitrary"), vmem_limit_bytes=VMEM_LIMIT),
        name="ssd_mixer",
    )(proj, proj, proj, dt_raw, conv_w.astype(F32), conv_b.reshape(1, -1).astype(F32),
      pad_heads(dt_bias), pad_heads(a_log), d_chan, ssd_norm.reshape(1, -1).astype(F32), expand, tri)


def _moba_body(q_ref, k_ref, v_ref, avg_ref, hot_ref, eye_ref, o_ref, *, nb, q_tile):
    blk = MOBA_BLOCK
    n_dense = MOBA_TOPK + 1

    kmean = _dot(avg_ref[...], k_ref[...]).astype(BF16)
    jrow = lax.broadcasted_iota(jnp.int32, (nb, blk), 0)
    row = lax.broadcasted_iota(jnp.int32, (q_tile, blk), 0)
    col = lax.broadcasted_iota(jnp.int32, (q_tile, blk), 1)

    def query_block(i):
        q_i = q_ref[i * blk:(i + 1) * blk, :]
        if i < n_dense:
            return q_i, None
        gate_t = _nt_dot(kmean, q_i)[0:nb, :]
        valid = jrow < i
        gm = jnp.where(valid, gate_t, -jnp.inf)
        rank = jnp.zeros((nb, blk), F32)
        for r in range(1, nb):
            gr = pltpu.roll(gm, r, axis=0)
            lower = ((jrow - r) & (nb - 1)) < jrow
            rank = rank + jnp.where(gr > gm, 1.0, 0.0) + jnp.where((gr == gm) & lower, 1.0, 0.0)
        keep = valid & (rank < float(MOBA_TOPK))
        bias_t = jnp.where(keep, 0.0, NEG_BIG)
        bias_t = jnp.concatenate([bias_t, jnp.zeros((LANES - nb, blk), F32)], axis=0).astype(BF16)
        bias = _nt_dot(eye_ref[...], bias_t).astype(BF16)
        return q_i, jnp.concatenate([q_i, bias], axis=1)

    def scores(i, h, q_i, q_aug):
        rs = slice(h * q_tile, (h + 1) * q_tile)
        s_own = _nt_dot(q_i[rs, :], k_ref[i * blk:(i + 1) * blk, :])
        s_own = jnp.where(col <= row + h * q_tile, s_own, NEG_BIG)
        if i == 0:
            return s_own, None
        if q_aug is None:
            return s_own, _nt_dot(q_i[rs, :], k_ref[0:i * blk, :])
        k_past = jnp.concatenate([k_ref[0:i * blk, :], hot_ref[0:i * blk, :]], axis=1)
        return s_own, _nt_dot(q_aug[rs, :], k_past)

    def finish(i, h, s_own, s_past):
        m = jnp.max(s_own, axis=-1, keepdims=True)
        if s_past is None:
            p = jnp.exp2(s_own - m)
        else:
            m = jnp.maximum(m, jnp.max(s_past, axis=-1, keepdims=True))
            p = jnp.concatenate([jnp.exp2(s_past - m), jnp.exp2(s_own - m)], axis=1)
        l = jnp.sum(p, axis=-1, keepdims=True)
        acc = _dot(p.astype(BF16), v_ref[0:(i + 1) * blk, :])
        o_ref[i * blk + h * q_tile:i * blk + (h + 1) * q_tile, :] = (acc / l).astype(o_ref.dtype)

    tiles = [(i, h) for i in range(nb) for h in range(blk // q_tile)]
    qcache = {}
    pending = None
    for i, h in tiles:
        if i not in qcache:
            qcache = {i: query_block(i)}
        s_next = scores(i, h, *qcache[i])
        if pending is not None:
            finish(*pending)
        pending = (i, h) + s_next
    finish(*pending)


def _moba(proj, bsz, seq):
    t = bsz * seq
    blk = MOBA_BLOCK
    nb = seq // blk
    d = MOBA_HEAD_DIM
    blk_of_key = jnp.arange(seq, dtype=jnp.int32) // blk
    avg = (jnp.arange(2 * nb, dtype=jnp.int32)[:, None] == blk_of_key[None, :]).astype(BF16) / blk
    hot = (blk_of_key[:, None] == jnp.arange(LANES, dtype=jnp.int32)[None, :]).astype(BF16)
    eye = jnp.eye(blk, dtype=BF16)

    def head_block(base):
        return pl.BlockSpec((seq, d), lambda b, h: (b, base // d + h))

    const = lambda b, h: (0, 0)
    return pl.pallas_call(
        functools.partial(_moba_body, nb=nb, q_tile=256),
        grid=(bsz, MOBA_HEADS),
        in_specs=[head_block(P0_Q), head_block(P0_K), head_block(P0_V),
                  pl.BlockSpec((2 * nb, seq), const), pl.BlockSpec((seq, LANES), const),
                  pl.BlockSpec((blk, blk), const)],
        out_specs=pl.BlockSpec((seq, d), lambda b, h: (b, h)),
        out_shape=jax.ShapeDtypeStruct((t, MOBA_INNER), BF16),
        compiler_params=pltpu.CompilerParams(
            dimension_semantics=("parallel", "parallel"), vmem_limit_bytes=VMEM_LIMIT),
        name="moba_attn",
    )(proj, proj, proj, avg, hot, eye)


def _out_proj_body(*refs, n_plain, final_norm):
    plain = refs[:n_plain]
    a_ref, g_ref, x_ref, w_ref = refs[n_plain:n_plain + 4]
    o_ref = refs[-1]
    g = g_ref[...].astype(F32)
    gated = (a_ref[...].astype(F32) * (g * _sigmoid(g))).astype(BF16)
    k0 = 0
    y = x_ref[...]
    for r in plain:
        kw = r.shape[1]
        y = y + _dot(r[...], w_ref[k0:k0 + kw, :])
        k0 += kw
    y = y + _dot(gated, w_ref[k0:k0 + gated.shape[1], :])
    if final_norm:
        fn_ref = refs[-2]
        y = y * lax.rsqrt(jnp.mean(y * y, axis=-1, keepdims=True) + NORM_EPS) * fn_ref[...]
    o_ref[...] = y


def _out_proj(plain, gated, gate_src, gate_col, x2d, w, final_gain=None, *, tm=1024):
    t, d = x2d.shape
    kg = gated.shape[1]
    in_specs = [pl.BlockSpec((tm, a.shape[1]), lambda i: (i, 0)) for a in plain]
    in_specs += [pl.BlockSpec((tm, kg), lambda i: (i, 0)),
                 pl.BlockSpec((tm, kg), lambda i: (i, gate_col // kg)),
                 pl.BlockSpec((tm, d), lambda i: (i, 0)),
                 pl.BlockSpec(w.shape, lambda i: (0, 0))]
    args = list(plain) + [gated, gate_src, x2d, w]
    if final_gain is not None:
        in_specs.append(pl.BlockSpec((1, d), lambda i: (0, 0)))
        args.append(final_gain.reshape(1, d).astype(F32))
    return pl.pallas_call(
        functools.partial(_out_proj_body, n_plain=len(plain), final_norm=final_gain is not None),
        grid=(t // tm,),
        in_specs=in_specs,
        out_specs=pl.BlockSpec((tm, d), lambda i: (i, 0)),
        out_shape=jax.ShapeDtypeStruct((t, d), F32),
        compiler_params=pltpu.CompilerParams(
            dimension_semantics=("parallel",), vmem_limit_bytes=VMEM_LIMIT),
        name="out_proj_final" if final_gain is not None else "out_proj",
    )(*args)


DIL_PHASES = 8


def _dil_tiles(seq):
    ph, qb = DIL_PHASES, DIL_QBLOCK
    per = seq // ph
    tiles = []
    rows = qb // ph
    for n in range(seq // qb):
        qs = [(p * per + rows * n, rows) for p in range(ph)]
        if n == 0:
            ks, mask = [(p * per, rows) for p in range(ph)], "a_first"
        else:
            ks, mask = [(p * per + rows * (n - 1), 2 * rows) for p in range(ph)], "a"
        tiles.append((qs, ks, mask, "init"))
    seg = ph // 4
    rows = qb // seg
    for p4 in range(4):
        for n in range(seq // 4 // qb):
            qs = [((p4 + 4 * j) * per + rows * n, rows) for j in range(seg)]
            if n == 0:
                ks, mask = [((p4 + 4 * j) * per, rows) for j in range(seg)], "b_first"
            else:
                ks, mask = [((p4 + 4 * j) * per + rows * (n - 1), 2 * rows) for j in range(seg)], "b"
            tiles.append((qs, ks, mask, "merge"))
    for p in range(ph):
        for h in range(per // qb):
            qs = [(p * per + qb * h, qb)]
            ks = [(p * per, qb * (h + 1))]
            tiles.append((qs, ks, "c%d" % h, "final"))
    return tiles


def _dil_masks(seq):
    import numpy as np
    ph, qb = DIL_PHASES, DIL_QBLOCK
    per = seq // ph

    def pos(n_rows, seg_rows, step):
        i = np.arange(n_rows)
        return step * (i % seg_rows) + i // seg_rows

    def band(qpos, kpos, span):
        dist = qpos[:, None] - kpos[None, :]
        return np.where((dist >= 0) & (dist <= span), 0.0, NEG_BIG).astype(np.float32)

    masks = {}
    for name, seg in (("a", ph), ("b", ph // 4)):
        rows = qb // seg
        masks[name] = band(qb + pos(qb, rows, seg), pos(2 * qb, 2 * rows, seg), qb)
        masks[name + "_first"] = band(pos(qb, rows, seg), pos(qb, rows, seg), qb)
    for h in range(per // qb):
        lq = qb * h + np.arange(qb)
        lk = np.arange(qb * (h + 1))
        same = (lq[:, None] - lk[None, :]) % 2 == 0
        masks["c%d" % h] = np.where(same & (lk[None, :] <= lq[:, None]), 0.0, NEG_BIG).astype(np.float32)
    return masks


def _dil_body(*refs, seq, mask_names, group_size):
    q_ref, k_ref, v_ref = refs[:3]
    mask_refs = dict(zip(mask_names, refs[3:3 + len(mask_names)]))
    o_ref, onat_ref = refs[3 + len(mask_names):][:2]
    state_a, state_b = refs[-6:-3], refs[-3:]
    state_in = {"merge": state_a, "final": state_b}
    state_out = {"init": state_a, "merge": state_b}
    qb = DIL_QBLOCK
    d = DIL_HEAD_DIM

    def rows_of(ref, slices):
        parts = [ref[s:s + n, :] for s, n in slices]
        return parts[0] if len(parts) == 1 else jnp.concatenate(parts, axis=0)

    def put_rows(ref, slices, val):
        r0 = 0
        for s, n in slices:
            ref[s:s + n, :] = val[r0:r0 + n, :]
            r0 += n

    def cat(parts):
        return parts[0] if len(parts) == 1 else jnp.concatenate(parts, axis=0)

    def scores(group):
        mask = mask_refs[group[0][2]][...]
        return cat([_nt_dot(rows_of(q_ref, qs), rows_of(k_ref, ks)) + mask for qs, ks, _, _ in group])

    def probs(group, s):
        stage = group[0][3]
        m_new = jnp.broadcast_to(jnp.max(s, axis=-1, keepdims=True), (s.shape[0], d))
        if stage == "init":
            m_old = None
        else:
            m_ref = state_in[stage][1]
            m_old = cat([rows_of(m_ref, qs) for qs, _, _, _ in group])
            m_new = jnp.maximum(m_old, m_new)
        p = jnp.concatenate([jnp.exp2(s[:, c0:c0 + d] - m_new) for c0 in range(0, s.shape[1], d)], axis=1)
        return p.astype(BF16), m_new, m_old

    def accumulate(group, p, m_new, m_old):
        stage = group[0][3]
        pvs = []
        for g, (_, ks, _, _) in enumerate(group):
            vb = rows_of(v_ref, ks)
            pvs.append(_dot(p[g * qb:(g + 1) * qb, :], jnp.concatenate([vb, jnp.ones_like(vb)], axis=1)))
        pv = cat(pvs)
        acc, l = pv[:, :d], pv[:, d:]
        m_b = m_new
        if stage != "init":
            alpha = jnp.exp2(m_old - m_b)
            acc_ref, _, l_ref = state_in[stage]
            acc = alpha * cat([rows_of(acc_ref, qs) for qs, _, _, _ in group]) + acc
            l = alpha * cat([rows_of(l_ref, qs) for qs, _, _, _ in group]) + l
        if stage == "final":
            out = acc / l
        for g, (qs, _, _, _) in enumerate(group):
            rs = slice(g * qb, (g + 1) * qb)
            if stage == "final":
                (s0, _), = qs
                p_idx, l0 = divmod(s0, seq // DIL_PHASES)
                onat_ref[pl.ds(DIL_PHASES * l0 + p_idx, qb, stride=DIL_PHASES), :] = out[rs, :]
            else:
                acc_out, m_out, l_out = state_out[stage]
                put_rows(acc_out, qs, acc[rs, :])
                put_rows(l_out, qs, l[rs, :])
                put_rows(m_out, qs, m_b[rs, :])

    groups = []
    for tile in _dil_tiles(seq):
        if groups and len(groups[-1]) < group_size and groups[-1][0][2:] == tile[2:]:
            groups[-1].append(tile)
        else:
            groups.append([tile])

    def owned(group):
        return {r for qs, _, _, _ in group for s0, n in qs for r in range(s0, s0 + n)}

    for ga, gb in zip(groups, groups[1:]):
        assert not owned(ga) & owned(gb)
    scored, ready = None, None
    for group in groups + [None, None]:
        s_new = None if group is None else (group, scores(group))
        p_new = None if scored is None else (scored[0],) + probs(*scored)
        if ready is not None:
            accumulate(*ready)
        scored, ready = s_new, p_new
    o_ref[...] = onat_ref[...].astype(o_ref.dtype)


def _dilated(proj, bsz, seq):
    t = bsz * seq
    d = DIL_HEAD_DIM
    masks = _dil_masks(seq)
    names = tuple(sorted(masks))

    def head_block(base):
        return pl.BlockSpec((seq, d), lambda b, h: (b, base // d + h))

    return pl.pallas_call(
        functools.partial(_dil_body, seq=seq, mask_names=names, group_size=1),
        grid=(bsz, DIL_HEADS),
        in_specs=[head_block(0), head_block(DIL_INNER), head_block(2 * DIL_INNER)]
        + [pl.BlockSpec(masks[n].shape, lambda b, h: (0, 0)) for n in names],
        out_specs=pl.BlockSpec((seq, d), lambda b, h: (b, h)),
        out_shape=jax.ShapeDtypeStruct((t, DIL_INNER), BF16),
        scratch_shapes=[pltpu.VMEM((seq, d), F32)] * 7,
        compiler_params=pltpu.CompilerParams(
            dimension_semantics=("parallel", "parallel"), vmem_limit_bytes=VMEM_LIMIT),
        name="dilated_attn",
    )(proj, proj, proj, *[jnp.asarray(masks[n]) for n in names])


def kernel(x, even_norm, even_w_in, ssd_conv_w, ssd_conv_b, ssd_dt_bias, ssd_a_log, ssd_d, ssd_norm,
           even_w_out, odd_norm, odd_w_in, odd_w_out, final_norm):
    bsz, seq, d = x.shape
    t = bsz * seq
    x2d = x.reshape(t, d)

    w = even_w_in[0]
    z_end = SSD_INNER
    xbc_end = z_end + SSD_CONV_DIM
    dt_end = xbc_end + SSD_HEADS
    q_end = dt_end + MOBA_INNER
    v_end = q_end + 2 * MOBA_INNER
    x_end = z_end + SSD_INNER
    moba_scale = MOBA_HEAD_DIM ** -0.5 * math.log2(math.e)
    col_scale = jnp.ones((w.shape[1],), F32).at[dt_end:q_end].set(moba_scale)
    w_b = (w * col_scale).astype(BF16)
    w_main = jnp.concatenate([w_b[:, :x_end], w_b[:, v_end:], w_b[:, dt_end:v_end], w_b[:, x_end:xbc_end]],
                             axis=1)
    w_dt = jnp.pad(w_b[:, xbc_end:dt_end], ((0, 0), (0, LANES - SSD_HEADS)))
    proj0, dt_raw = _norm_proj(x2d, even_norm[0], w_main, w_dt, tm=seq, tn=P0_N // 4)
    y_ssd = _ssd(proj0, dt_raw, ssd_conv_w[0], ssd_conv_b[0], ssd_dt_bias[0], ssd_a_log[0],
                 ssd_d[0], ssd_norm[0], bsz, seq)
    o_moba = _moba(proj0, bsz, seq)
    x1 = _out_proj([y_ssd], o_moba, proj0, P0_G, x2d, even_w_out[0].astype(BF16))

    w1 = odd_w_in[0]
    dil_scale = DIL_HEAD_DIM ** -0.5 * math.log2(math.e)
    col_scale1 = jnp.ones((w1.shape[1],), F32).at[:DIL_INNER].set(dil_scale)
    w1_main = (w1 * col_scale1).astype(BF16)
    proj1 = _norm_proj(x1, odd_norm[0], w1_main, tm=seq, tn=1024, phases=DIL_PHASES,
                       phase_cols=3 * DIL_INNER)
    o_dil = _dilated(proj1, bsz, seq)
    out = _out_proj([], o_dil, proj1, 3 * DIL_INNER, x1, odd_w_out[0].astype(BF16), final_norm)
    return out.reshape(bsz, seq, d)
```

```python
import functools
import math

import jax
import jax.numpy as jnp
from jax import lax
from jax.experimental import pallas as pl
from jax.experimental.pallas import tpu as pltpu

F32 = jnp.float32
BF16 = jnp.bfloat16

NORM_EPS = 1e-5
D_MODEL = 1024

SSD_HEADS = 16
SSD_HEAD_DIM = 64
SSD_INNER = SSD_HEADS * SSD_HEAD_DIM
SSD_GROUPS = 2
SSD_STATE = 128
SSD_CONV = 4
SSD_CHUNK = 128
SSD_BC = 2 * SSD_GROUPS * SSD_STATE
SSD_CONV_DIM = SSD_INNER + SSD_BC

MOBA_HEADS = 8
MOBA_HEAD_DIM = 128
MOBA_INNER = MOBA_HEADS * MOBA_HEAD_DIM
MOBA_BLOCK = 256
MOBA_TOPK = 3

DIL_HEADS = 16
DIL_HEAD_DIM = 128
DIL_INNER = DIL_HEADS * DIL_HEAD_DIM
DIL_QBLOCK = 128
DIL_PATTERNS = ((128, 1), (512, 4), (2048, 16))

LANES = 128
NEG_BIG = -1e30
VMEM_LIMIT = 56 * 1024 * 1024

P0_Z = 0
P0_X = P0_Z + SSD_INNER
P0_G = P0_X + SSD_INNER
P0_Q = P0_G + MOBA_INNER
P0_K = P0_Q + MOBA_INNER
P0_V = P0_K + MOBA_INNER
P0_BC = P0_V + MOBA_INNER
P0_N = P0_BC + SSD_BC


def _nt_dot(a, b):
    return lax.dot_general(a, b, (((1,), (1,)), ((), ())), preferred_element_type=F32)


def _dot(a, b):
    return jnp.dot(a, b, preferred_element_type=F32)


def _sigmoid(x):
    return 1.0 / (1.0 + jnp.exp(-x))


def _norm_proj_body(*refs, kind, row_chunk, phases, n_phase_tiles):
    if kind == "aux":
        x_ref, g_ref, w_ref, waux_ref, o_ref, oaux_ref, hn_ref = refs
    else:
        x_ref, g_ref, w_ref, ws_ref, o_ref, hn_ref, hnp_ref, hnf_ref, wb_ref = refs
    j = pl.program_id(1)
    tm = x_ref.shape[0]

    @pl.when(j == 0)
    def _():
        g = g_ref[...]

        def norm_rows(c, carry):
            r0 = pl.multiple_of(c * row_chunk, row_chunk)
            x = x_ref[pl.ds(r0, row_chunk), :]
            ms = jnp.mean(x * x, axis=-1, keepdims=True)
            hn = x * lax.rsqrt(ms + NORM_EPS) * g
            hn_ref[pl.ds(r0, row_chunk), :] = hn.astype(BF16)
            if phases:
                for cc in range(x.shape[1] // LANES):
                    hnf_ref[cc, pl.ds(r0, row_chunk), :] = hn[:, cc * LANES:(cc + 1) * LANES]
            return carry

        lax.fori_loop(0, tm // row_chunk, norm_rows, 0)
        if phases:
            per = tm // phases
            for p in range(phases):
                for cc in range(hnf_ref.shape[0]):
                    hnp_ref[p * per:(p + 1) * per, cc * LANES:(cc + 1) * LANES] = (
                        hnf_ref[cc, pl.ds(p, per, stride=phases), :].astype(BF16))
        if kind == "aux":
            oaux_ref[...] = _nt_dot(hn_ref[...], waux_ref[...])

    if kind != "aux":
        wb_ref[...] = (w_ref[...] * ws_ref[...]).astype(BF16)

    def project(src_ref):
        sub = min(tm, 512)
        for r0 in range(0, tm, sub):
            lhs = src_ref[r0:r0 + sub, :]
            prod = _nt_dot(lhs, w_ref[...]) if kind == "aux" else _dot(lhs, wb_ref[...])
            o_ref[r0:r0 + sub, :] = prod.astype(o_ref.dtype)

    if phases:
        @pl.when(j < n_phase_tiles)
        def _():
            project(hnp_ref)

        @pl.when(j >= n_phase_tiles)
        def _():
            project(hn_ref)
    else:
        project(hn_ref)


def _norm_proj_t(x2d, gain, w_t, w_aux_t, *, tm, tn):
    t, d = x2d.shape
    n = w_t.shape[0]
    return pl.pallas_call(
        functools.partial(_norm_proj_body, kind="aux", row_chunk=256, phases=0, n_phase_tiles=0),
        grid=(t // tm, n // tn),
        in_specs=[pl.BlockSpec((tm, d), lambda i, j: (i, 0)),
                  pl.BlockSpec((1, d), lambda i, j: (0, 0)),
                  pl.BlockSpec((tn, d), lambda i, j: (j, 0)),
                  pl.BlockSpec((LANES, d), lambda i, j: (0, 0))],
        out_specs=[pl.BlockSpec((tm, tn), lambda i, j: (i, j)),
                   pl.BlockSpec((tm, LANES), lambda i, j: (i, 0))],
        out_shape=[jax.ShapeDtypeStruct((t, n), BF16), jax.ShapeDtypeStruct((t, LANES), F32)],
        scratch_shapes=[pltpu.VMEM((tm, d), BF16)],
        compiler_params=pltpu.CompilerParams(
            dimension_semantics=("parallel", "arbitrary"), vmem_limit_bytes=VMEM_LIMIT),
        name="norm_proj_aux",
    )(x2d, gain.reshape(1, d).astype(F32), w_t, w_aux_t)


def _norm_proj_phased(x2d, gain, w, col_scale, *, tm, tn, phases, phase_cols):
    t, d = x2d.shape
    n = w.shape[1]
    return pl.pallas_call(
        functools.partial(_norm_proj_body, kind="phased", row_chunk=256, phases=phases,
                          n_phase_tiles=phase_cols // tn),
        grid=(t // tm, n // tn),
        in_specs=[pl.BlockSpec((tm, d), lambda i, j: (i, 0)),
                  pl.BlockSpec((1, d), lambda i, j: (0, 0)),
                  pl.BlockSpec((d, tn), lambda i, j: (0, j)),
                  pl.BlockSpec((1, tn), lambda i, j: (0, j))],
        out_specs=pl.BlockSpec((tm, tn), lambda i, j: (i, j)),
        out_shape=jax.ShapeDtypeStruct((t, n), BF16),
        scratch_shapes=[pltpu.VMEM((tm, d), BF16), pltpu.VMEM((tm, d), BF16),
                        pltpu.VMEM((d // LANES, tm, LANES), F32), pltpu.VMEM((d, tn), BF16)],
        compiler_params=pltpu.CompilerParams(
            dimension_semantics=("parallel", "arbitrary"), vmem_limit_bytes=VMEM_LIMIT),
        name="norm_proj",
    )(x2d, gain.reshape(1, d).astype(F32), w, col_scale.reshape(1, n))


def _split3(a):
    hi = a.astype(BF16)
    r = a - hi.astype(F32)
    mid = r.astype(BF16)
    lo = (r - mid.astype(F32)).astype(BF16)
    return hi, mid, lo


def _expand_heads(a, e):
    hi, mid, lo = _split3(a)
    return _dot(hi, e) + _dot(mid, e) + _dot(lo, e)


def _ssd_body(z_ref, xs_ref, bc_ref, dt_ref, cw_ref, cb_ref, dtb_ref, alog_ref, dskip_ref, nrm_ref,
              e_ref, tri_ref, y_ref, state_ref, ubuf_ref):
    c = pl.program_id(1)
    L = SSD_CHUNK
    tail = 8

    @pl.when(c == 0)
    def _():
        state_ref[...] = jnp.zeros_like(state_ref)
        ubuf_ref[0:tail, :] = jnp.zeros((tail, SSD_CONV_DIM), F32)

    row = lax.broadcasted_iota(jnp.int32, (L, L), 0)
    col = lax.broadcasted_iota(jnp.int32, (L, L), 1)
    causal = col <= row
    lane = lax.broadcasted_iota(jnp.int32, (L, LANES), 1)
    low_half = lane < SSD_HEAD_DIM
    heads_per_group = SSD_HEADS // SSD_GROUPS
    gw = heads_per_group * SSD_HEAD_DIM

    def chunk(r):
        u = jnp.concatenate([xs_ref[r, :], bc_ref[r, :]], axis=1).astype(F32)
        ubuf_ref[tail:tail + L, :] = u
        cw = cw_ref[...]
        acc = cb_ref[...] + cw[SSD_CONV - 1:SSD_CONV, :] * u
        for k in range(SSD_CONV - 1):
            acc = acc + cw[k:k + 1, :] * ubuf_ref[pl.ds(tail - (SSD_CONV - 1) + k, L), :]
        ubuf_ref[0:tail, :] = u[L - tail:L, :]
        act = acc * _sigmoid(acc)

        xs = act[:, :SSD_INNER]

        dtr = dt_ref[r, :] + dtb_ref[...]
        dt = jnp.maximum(dtr, 0.0) + jnp.log1p(jnp.exp(-jnp.abs(dtr)))
        a = -jnp.exp(alog_ref[...])
        ac = dt * a
        tri = tri_ref[...]
        hi, mid, lo = _split3(ac)
        a_cum = _dot(tri, hi) + _dot(tri, mid) + _dot(tri, lo)
        a_last = a_cum[L - 1:L, :]
        dec_states = jnp.exp(a_last - a_cum)
        exp_acum = jnp.exp(a_cum)
        e = e_ref[...]
        dt_e = _expand_heads(dt, e)
        dec_e = _expand_heads(dec_states, e)
        ea_e = _expand_heads(exp_acum, e)
        a_cum_t = a_cum.T

        xc = xs * dt_e
        xc_b = xc.astype(BF16)
        xdec_b = (xc * dec_e).astype(BF16)

        for g in range(SSD_GROUPS):
            b_g = act[:, SSD_INNER + g * SSD_STATE:SSD_INNER + (g + 1) * SSD_STATE]
            c_g = act[:, SSD_INNER + (SSD_GROUPS + g) * SSD_STATE:
                      SSD_INNER + (SSD_GROUPS + g + 1) * SSD_STATE]
            b_gb = b_g.astype(BF16)
            c_gb = c_g.astype(BF16)
            scores = _nt_dot(c_gb, b_gb)
            gs = slice(g * gw, (g + 1) * gw)

            prev = state_ref[:, gs]
            y_off = _dot(c_gb, prev.astype(BF16))
            st_new = _dot(b_g.T.astype(BF16), xdec_b[:, gs])
            state_ref[:, gs] = prev * ea_e[L - 1:L, gs] + st_new

            pieces = []
            for pair in range(heads_per_group // 2):
                ms = []
                for hh in range(2):
                    h = g * heads_per_group + 2 * pair + hh
                    seg = a_cum[:, h:h + 1] - a_cum_t[h:h + 1, :]
                    lmat = jnp.exp(jnp.where(causal, seg, NEG_BIG))
                    ms.append((scores * lmat).astype(BF16))
                m_pair = jnp.concatenate(ms, axis=1)
                cs = slice(g * gw + pair * LANES, g * gw + (pair + 1) * LANES)
                x_pair = xc_b[:, cs]
                zero = jnp.zeros_like(x_pair)
                rhs = jnp.concatenate([jnp.where(low_half, x_pair, zero),
                                       jnp.where(low_half, zero, x_pair)], axis=0)
                pieces.append(_dot(m_pair, rhs))
            y_diag = jnp.concatenate(pieces, axis=1)

            y = y_diag + y_off * ea_e[:, gs] + xs[:, gs] * dskip_ref[:, gs]
            zg = z_ref[r, gs].astype(F32)
            ug = y * (zg * _sigmoid(zg))
            ug = ug * lax.rsqrt(jnp.mean(ug * ug, axis=-1, keepdims=True) + NORM_EPS)
            y_ref[r, gs] = (ug * nrm_ref[:, gs]).astype(y_ref.dtype)

    for ci in range(xs_ref.shape[0] // L):
        chunk(slice(ci * L, (ci + 1) * L))


def _ssd(proj, dt_raw, conv_w, conv_b, dt_bias, a_log, d_skip, ssd_norm, bsz, seq, *, chunks_per_step=4):
    t = bsz * seq
    L = SSD_CHUNK
    R = chunks_per_step * L
    nc = seq // R

    def pad_heads(v):
        return jnp.pad(v.astype(F32), (0, LANES - SSD_HEADS)).reshape(1, LANES)

    head_of_chan = jnp.arange(SSD_INNER, dtype=jnp.int32) // SSD_HEAD_DIM
    expand = (jnp.arange(LANES, dtype=jnp.int32)[:, None] == head_of_chan[None, :]).astype(BF16)
    tri = (jnp.arange(L)[:, None] >= jnp.arange(L)[None, :]).astype(BF16)
    d_chan = jnp.repeat(d_skip.astype(F32), SSD_HEAD_DIM).reshape(1, SSD_INNER)

    def rows(b, c):
        return b * nc + c

    const = lambda b, c: (0, 0)
    return pl.pallas_call(
        _ssd_body,
        grid=(bsz, nc),
        in_specs=[
            pl.BlockSpec((R, SSD_INNER), lambda b, c: (rows(b, c), P0_Z // SSD_INNER)),
            pl.BlockSpec((R, SSD_INNER), lambda b, c: (rows(b, c), P0_X // SSD_INNER)),
            pl.BlockSpec((R, SSD_BC), lambda b, c: (rows(b, c), P0_BC // SSD_BC)),
            pl.BlockSpec((R, LANES), lambda b, c: (rows(b, c), 0)),
            pl.BlockSpec((SSD_CONV, SSD_CONV_DIM), const),
            pl.BlockSpec((1, SSD_CONV_DIM), const),
            pl.BlockSpec((1, LANES), const),
            pl.BlockSpec((1, LANES), const),
            pl.BlockSpec((1, SSD_INNER), const),
            pl.BlockSpec((1, SSD_INNER), const),
            pl.BlockSpec((LANES, SSD_INNER), const),
            pl.BlockSpec((L, L), const),
        ],
        out_specs=pl.BlockSpec((R, SSD_INNER), lambda b, c: (rows(b, c), 0)),
        out_shape=jax.ShapeDtypeStruct((t, SSD_INNER), BF16),
        scratch_shapes=[pltpu.VMEM((SSD_STATE, SSD_INNER), F32),
                        pltpu.VMEM((8 + L, SSD_CONV_DIM), F32)],
        compiler_params=pltpu.CompilerParams(
            dimension_semantics=("parallel", "arbitrary"), vmem_limit_bytes=VMEM_LIMIT),
        name="ssd_mixer",
    )(proj, proj, proj, dt_raw, conv_w.astype(F32), conv_b.reshape(1, -1).astype(F32),
      pad_heads(dt_bias), pad_heads(a_log), d_chan, ssd_norm.reshape(1, -1).astype(F32), expand, tri)


def _moba_body(q_ref, k_ref, v_ref, avg_ref, hot_ref, eye_ref, o_ref, *, nb, q_tile):
    blk = MOBA_BLOCK
    n_dense = MOBA_TOPK + 1

    kmean = _dot(avg_ref[...], k_ref[...]).astype(BF16)
    jrow = lax.broadcasted_iota(jnp.int32, (nb, blk), 0)
    row = lax.broadcasted_iota(jnp.int32, (q_tile, blk), 0)
    col = lax.broadcasted_iota(jnp.int32, (q_tile, blk), 1)

    def query_block(i):
        q_i = q_ref[i * blk:(i + 1) * blk, :]
        if i < n_dense:
            return q_i, None
        gate_t = _nt_dot(kmean, q_i)[0:nb, :]
        valid = jrow < i
        gm = jnp.where(valid, gate_t, -jnp.inf)
        rank = jnp.zeros((nb, blk), F32)
        for r in range(1, nb):
            gr = pltpu.roll(gm, r, axis=0)
            lower = ((jrow - r) & (nb - 1)) < jrow
            rank = rank + jnp.where(gr > gm, 1.0, 0.0) + jnp.where((gr == gm) & lower, 1.0, 0.0)
        keep = valid & (rank < float(MOBA_TOPK))
        bias_t = jnp.where(keep, 0.0, NEG_BIG)
        bias_t = jnp.concatenate([bias_t, jnp.zeros((LANES - nb, blk), F32)], axis=0).astype(BF16)
        bias = _nt_dot(eye_ref[...], bias_t).astype(BF16)
        return q_i, jnp.concatenate([q_i, bias], axis=1)

    def scores(i, h, q_i, q_aug):
        rs = slice(h * q_tile, (h + 1) * q_tile)
        s_own = _nt_dot(q_i[rs, :], k_ref[i * blk:(i + 1) * blk, :])
        s_own = jnp.where(col <= row + h * q_tile, s_own, NEG_BIG)
        if i == 0:
            return s_own, None
        if q_aug is None:
            return s_own, _nt_dot(q_i[rs, :], k_ref[0:i * blk, :])
        k_past = jnp.concatenate([k_ref[0:i * blk, :], hot_ref[0:i * blk, :]], axis=1)
        return s_own, _nt_dot(q_aug[rs, :], k_past)

    def finish(i, h, s_own, s_past):
        m = jnp.max(s_own, axis=-1, keepdims=True)
        if s_past is None:
            p = jnp.exp2(s_own - m)
        else:
            m = jnp.maximum(m, jnp.max(s_past, axis=-1, keepdims=True))
            p = jnp.concatenate([jnp.exp2(s_past - m), jnp.exp2(s_own - m)], axis=1)
        vb = v_ref[0:(i + 1) * blk, :]
        pv = _dot(p.astype(BF16), jnp.concatenate([vb, jnp.ones_like(vb)], axis=1))
        d = vb.shape[1]
        o_ref[i * blk + h * q_tile:i * blk + (h + 1) * q_tile, :] = (pv[:, :d] / pv[:, d:]).astype(o_ref.dtype)

    tiles = [(i, h) for i in range(nb) for h in range(blk // q_tile)]
    qcache = {}
    pending = None
    for i, h in tiles:
        if i not in qcache:
            qcache = {i: query_block(i)}
        s_next = scores(i, h, *qcache[i])
        if pending is not None:
            finish(*pending)
        pending = (i, h) + s_next
    finish(*pending)


def _moba(proj, bsz, seq):
    t = bsz * seq
    blk = MOBA_BLOCK
    nb = seq // blk
    d = MOBA_HEAD_DIM
    blk_of_key = jnp.arange(seq, dtype=jnp.int32) // blk
    avg = (jnp.arange(2 * nb, dtype=jnp.int32)[:, None] == blk_of_key[None, :]).astype(BF16) / blk
    hot = (blk_of_key[:, None] == jnp.arange(LANES, dtype=jnp.int32)[None, :]).astype(BF16)
    eye = jnp.eye(blk, dtype=BF16)

    def head_block(base):
        return pl.BlockSpec((seq, d), lambda b, h: (b, base // d + h))

    const = lambda b, h: (0, 0)
    return pl.pallas_call(
        functools.partial(_moba_body, nb=nb, q_tile=256),
        grid=(bsz, MOBA_HEADS),
        in_specs=[head_block(P0_Q), head_block(P0_K), head_block(P0_V),
                  pl.BlockSpec((2 * nb, seq), const), pl.BlockSpec((seq, LANES), const),
                  pl.BlockSpec((blk, blk), const)],
        out_specs=pl.BlockSpec((seq, d), lambda b, h: (b, h)),
        out_shape=jax.ShapeDtypeStruct((t, MOBA_INNER), BF16),
        compiler_params=pltpu.CompilerParams(
            dimension_semantics=("parallel", "parallel"), vmem_limit_bytes=VMEM_LIMIT),
        name="moba_attn",
    )(proj, proj, proj, avg, hot, eye)


def _out_proj_body(*refs, n_plain, final_norm):
    plain = refs[:n_plain]
    a_ref, g_ref, x_ref, w_ref = refs[n_plain:n_plain + 4]
    o_ref = refs[-1]
    g = g_ref[...].astype(F32)
    gated = (a_ref[...].astype(F32) * (g * _sigmoid(g))).astype(BF16)
    k0 = 0
    y = x_ref[...]
    for r in plain:
        kw = r.shape[1]
        y = y + _dot(r[...], w_ref[k0:k0 + kw, :])
        k0 += kw
    y = y + _dot(gated, w_ref[k0:k0 + gated.shape[1], :])
    if final_norm:
        fn_ref = refs[-2]
        y = y * lax.rsqrt(jnp.mean(y * y, axis=-1, keepdims=True) + NORM_EPS) * fn_ref[...]
    o_ref[...] = y


def _out_proj(plain, gated, gate_src, gate_col, x2d, w, final_gain=None, *, tm=1024):
    t, d = x2d.shape
    kg = gated.shape[1]
    in_specs = [pl.BlockSpec((tm, a.shape[1]), lambda i: (i, 0)) for a in plain]
    in_specs += [pl.BlockSpec((tm, kg), lambda i: (i, 0)),
                 pl.BlockSpec((tm, kg), lambda i: (i, gate_col // kg)),
                 pl.BlockSpec((tm, d), lambda i: (i, 0)),
                 pl.BlockSpec(w.shape, lambda i: (0, 0))]
    args = list(plain) + [gated, gate_src, x2d, w]
    if final_gain is not None:
        in_specs.append(pl.BlockSpec((1, d), lambda i: (0, 0)))
        args.append(final_gain.reshape(1, d).astype(F32))
    return pl.pallas_call(
        functools.partial(_out_proj_body, n_plain=len(plain), final_norm=final_gain is not None),
        grid=(t // tm,),
        in_specs=in_specs,
        out_specs=pl.BlockSpec((tm, d), lambda i: (i, 0)),
        out_shape=jax.ShapeDtypeStruct((t, d), F32),
        compiler_params=pltpu.CompilerParams(
            dimension_semantics=("parallel",), vmem_limit_bytes=VMEM_LIMIT),
        name="out_proj_final" if final_gain is not None else "out_proj",
    )(*args)


DIL_PHASES = 8


def _dil_tiles(seq):
    ph, qb = DIL_PHASES, DIL_QBLOCK
    per = seq // ph
    tiles = []
    rows = qb // ph
    for n in range(seq // qb):
        qs = [(p * per + rows * n, rows) for p in range(ph)]
        if n == 0:
            ks, mask = [(p * per, rows) for p in range(ph)], "a_first"
        else:
            ks, mask = [(p * per + rows * (n - 1), 2 * rows) for p in range(ph)], "a"
        tiles.append((qs, ks, mask, "init"))
    seg = ph // 4
    rows = qb // seg
    for p4 in range(4):
        for n in range(seq // 4 // qb):
            qs = [((p4 + 4 * j) * per + rows * n, rows) for j in range(seg)]
            if n == 0:
                ks, mask = [((p4 + 4 * j) * per, rows) for j in range(seg)], "b_first"
            else:
                ks, mask = [((p4 + 4 * j) * per + rows * (n - 1), 2 * rows) for j in range(seg)], "b"
            tiles.append((qs, ks, mask, "merge"))
    for p in range(ph):
        for h in range(per // qb):
            qs = [(p * per + qb * h, qb)]
            ks = [(p * per, qb * (h + 1))]
            tiles.append((qs, ks, "c%d" % h, "final"))
    return tiles


def _dil_masks(seq):
    import numpy as np
    ph, qb = DIL_PHASES, DIL_QBLOCK
    per = seq // ph

    def pos(n_rows, seg_rows, step):
        i = np.arange(n_rows)
        return step * (i % seg_rows) + i // seg_rows

    def band(qpos, kpos, span):
        dist = qpos[:, None] - kpos[None, :]
        return np.where((dist >= 0) & (dist <= span), 0.0, NEG_BIG).astype(np.float32)

    masks = {}
    for name, seg in (("a", ph), ("b", ph // 4)):
        rows = qb // seg
        masks[name] = band(qb + pos(qb, rows, seg), pos(2 * qb, 2 * rows, seg), qb)
        masks[name + "_first"] = band(pos(qb, rows, seg), pos(qb, rows, seg), qb)
    for h in range(per // qb):
        lq = qb * h + np.arange(qb)
        lk = np.arange(qb * (h + 1))
        same = (lq[:, None] - lk[None, :]) % 2 == 0
        masks["c%d" % h] = np.where(same & (lk[None, :] <= lq[:, None]), 0.0, NEG_BIG).astype(np.float32)
    return masks


def _dil_body(*refs, seq, mask_names, group_size):
    q_ref, k_ref, v_ref = refs[:3]
    mask_refs = dict(zip(mask_names, refs[3:3 + len(mask_names)]))
    o_ref, onat_ref = refs[3 + len(mask_names):][:2]
    state_a, state_b = refs[-6:-3], refs[-3:]
    state_in = {"merge": state_a, "final": state_b}
    state_out = {"init": state_a, "merge": state_b}
    qb = DIL_QBLOCK
    d = DIL_HEAD_DIM

    def rows_of(ref, slices):
        parts = [ref[s:s + n, :] for s, n in slices]
        return parts[0] if len(parts) == 1 else jnp.concatenate(parts, axis=0)

    def put_rows(ref, slices, val):
        r0 = 0
        for s, n in slices:
            ref[s:s + n, :] = val[r0:r0 + n, :]
            r0 += n

    def cat(parts):
        return parts[0] if len(parts) == 1 else jnp.concatenate(parts, axis=0)

    def scores(group):
        mask = mask_refs[group[0][2]][...]
        return cat([_nt_dot(rows_of(q_ref, qs), rows_of(k_ref, ks)) + mask for qs, ks, _, _ in group])

    def probs(group, s):
        stage = group[0][3]
        m_new = jnp.broadcast_to(jnp.max(s, axis=-1, keepdims=True), (s.shape[0], d))
        if stage == "init":
            m_old = None
        else:
            m_ref = state_in[stage][1]
            m_old = cat([rows_of(m_ref, qs) for qs, _, _, _ in group])
            m_new = jnp.maximum(m_old, m_new)
        p = jnp.concatenate([jnp.exp2(s[:, c0:c0 + d] - m_new) for c0 in range(0, s.shape[1], d)], axis=1)
        return p.astype(BF16), m_new, m_old

    def accumulate(group, p, m_new, m_old):
        stage = group[0][3]
        pvs = []
        for g, (_, ks, _, _) in enumerate(group):
            vb = rows_of(v_ref, ks)
            pvs.append(_dot(p[g * qb:(g + 1) * qb, :], jnp.concatenate([vb, jnp.ones_like(vb)], axis=1)))
        pv = cat(pvs)
        acc, l = pv[:, :d], pv[:, d:]
        m_b = m_new
        if stage != "init":
            alpha = jnp.exp2(m_old - m_b)
            acc_ref, _, l_ref = state_in[stage]
            acc = alpha * cat([rows_of(acc_ref, qs) for qs, _, _, _ in group]) + acc
            l = alpha * cat([rows_of(l_ref, qs) for qs, _, _, _ in group]) + l
        if stage == "final":
            out = acc / l
        for g, (qs, _, _, _) in enumerate(group):
            rs = slice(g * qb, (g + 1) * qb)
            if stage == "final":
                (s0, _), = qs
                p_idx, l0 = divmod(s0, seq // DIL_PHASES)
                onat_ref[pl.ds(DIL_PHASES * l0 + p_idx, qb, stride=DIL_PHASES), :] = out[rs, :]
            else:
                acc_out, m_out, l_out = state_out[stage]
                put_rows(acc_out, qs, acc[rs, :])
                put_rows(l_out, qs, l[rs, :])
                put_rows(m_out, qs, m_b[rs, :])

    groups = []
    for tile in _dil_tiles(seq):
        if groups and len(groups[-1]) < group_size and groups[-1][0][2:] == tile[2:]:
            groups[-1].append(tile)
        else:
            groups.append([tile])

    def owned(group):
        return {r for qs, _, _, _ in group for s0, n in qs for r in range(s0, s0 + n)}

    for ga, gb in zip(groups, groups[1:]):
        assert not owned(ga) & owned(gb)
    scored, ready = None, None
    for group in groups + [None, None]:
        s_new = None if group is None else (group, scores(group))
        p_new = None if scored is None else (scored[0],) + probs(*scored)
        if ready is not None:
            accumulate(*ready)
        scored, ready = s_new, p_new
    o_ref[...] = onat_ref[...].astype(o_ref.dtype)


def _dilated(proj, bsz, seq):
    t = bsz * seq
    d = DIL_HEAD_DIM
    masks = _dil_masks(seq)
    names = tuple(sorted(masks))

    def head_block(base):
        return pl.BlockSpec((seq, d), lambda b, h: (b, base // d + h))

    return pl.pallas_call(
        functools.partial(_dil_body, seq=seq, mask_names=names, group_size=1),
        grid=(bsz, DIL_HEADS),
        in_specs=[head_block(0), head_block(DIL_INNER), head_block(2 * DIL_INNER)]
        + [pl.BlockSpec(masks[n].shape, lambda b, h: (0, 0)) for n in names],
        out_specs=pl.BlockSpec((seq, d), lambda b, h: (b, h)),
        out_shape=jax.ShapeDtypeStruct((t, DIL_INNER), BF16),
        scratch_shapes=[pltpu.VMEM((seq, d), F32)] * 7,
        compiler_params=pltpu.CompilerParams(
            dimension_semantics=("parallel", "parallel"), vmem_limit_bytes=VMEM_LIMIT),
        name="dilated_attn",
    )(proj, proj, proj, *[jnp.asarray(masks[n]) for n in names])


def kernel(x, even_norm, even_w_in, ssd_conv_w, ssd_conv_b, ssd_dt_bias, ssd_a_log, ssd_d, ssd_norm,
           even_w_out, odd_norm, odd_w_in, odd_w_out, final_norm):
    bsz, seq, d = x.shape
    t = bsz * seq
    x2d = x.reshape(t, d)

    w = even_w_in[0]
    z_end = SSD_INNER
    xbc_end = z_end + SSD_CONV_DIM
    dt_end = xbc_end + SSD_HEADS
    q_end = dt_end + MOBA_INNER
    v_end = q_end + 2 * MOBA_INNER
    x_end = z_end + SSD_INNER
    moba_scale = MOBA_HEAD_DIM ** -0.5 * math.log2(math.e)
    col_scale = jnp.ones((w.shape[1], 1), F32).at[dt_end:q_end].set(moba_scale)
    w_t = (w.T * col_scale).astype(BF16)
    w_main = jnp.concatenate([w_t[:x_end], w_t[v_end:], w_t[dt_end:v_end], w_t[x_end:xbc_end]],
                             axis=0)
    w_dt = jnp.pad(w_t[xbc_end:dt_end], ((0, LANES - SSD_HEADS), (0, 0)))
    proj0, dt_raw = _norm_proj_t(x2d, even_norm[0], w_main, w_dt, tm=seq, tn=P0_N // 4)
    y_ssd = _ssd(proj0, dt_raw, ssd_conv_w[0], ssd_conv_b[0], ssd_dt_bias[0], ssd_a_log[0],
                 ssd_d[0], ssd_norm[0], bsz, seq)
    o_moba = _moba(proj0, bsz, seq)
    x1 = _out_proj([y_ssd], o_moba, proj0, P0_G, x2d, even_w_out[0].astype(BF16))

    w1 = odd_w_in[0]
    dil_scale = DIL_HEAD_DIM ** -0.5 * math.log2(math.e)
    col_scale1 = jnp.ones((w1.shape[1],), F32).at[:DIL_INNER].set(dil_scale)
    proj1 = _norm_proj_phased(x1, odd_norm[0], w1, col_scale1, tm=seq, tn=1024, phases=DIL_PHASES,
                              phase_cols=3 * DIL_INNER)
    o_dil = _dilated(proj1, bsz, seq)
    out = _out_proj([], o_dil, proj1, 3 * DIL_INNER, x1, odd_w_out[0].astype(BF16), final_norm)
    return out.reshape(bsz, seq, d)
```

```python
import functools
import math

import jax
import jax.numpy as jnp
from jax import lax
from jax.experimental import pallas as pl
from jax.experimental.pallas import tpu as pltpu

F32 = jnp.float32
BF16 = jnp.bfloat16

NORM_EPS = 1e-5
D_MODEL = 1024

SSD_HEADS = 16
SSD_HEAD_DIM = 64
SSD_INNER = SSD_HEADS * SSD_HEAD_DIM
SSD_GROUPS = 2
SSD_STATE = 128
SSD_CONV = 4
SSD_CHUNK = 128
SSD_BC = 2 * SSD_GROUPS * SSD_STATE
SSD_CONV_DIM = SSD_INNER + SSD_BC

MOBA_HEADS = 8
MOBA_HEAD_DIM = 128
MOBA_INNER = MOBA_HEADS * MOBA_HEAD_DIM
MOBA_BLOCK = 256
MOBA_TOPK = 3

DIL_HEADS = 16
DIL_HEAD_DIM = 128
DIL_INNER = DIL_HEADS * DIL_HEAD_DIM
DIL_QBLOCK = 128
DIL_PATTERNS = ((128, 1), (512, 4), (2048, 16))

LANES = 128
NEG_BIG = -1e30
VMEM_LIMIT = 56 * 1024 * 1024

P0_Z = 0
P0_X = P0_Z + SSD_INNER
P0_G = P0_X + SSD_INNER
P0_Q = P0_G + MOBA_INNER
P0_K = P0_Q + MOBA_INNER
P0_V = P0_K + MOBA_INNER
P0_BC = P0_V + MOBA_INNER
P0_N = P0_BC + SSD_BC


def _nt_dot(a, b):
    return lax.dot_general(a, b, (((1,), (1,)), ((), ())), preferred_element_type=F32)


def _dot(a, b):
    return jnp.dot(a, b, preferred_element_type=F32)


def _sigmoid(x):
    return 1.0 / (1.0 + jnp.exp2(x * -math.log2(math.e)))


def _norm_proj_body(*refs, kind, row_chunk, phases, n_phase_tiles):
    if kind == "aux":
        x_ref, w_ref, waux_ref, o_ref, oaux_ref, hn_ref = refs
    else:
        x_ref, w_ref, o_ref, hn_ref, hnp_ref, hnf_ref = refs
    j = pl.program_id(1)
    tm = x_ref.shape[0]

    @pl.when(j == 0)
    def _():
        def norm_rows(c, carry):
            r0 = pl.multiple_of(c * row_chunk, row_chunk)
            x = x_ref[pl.ds(r0, row_chunk), :]
            ms = jnp.mean(x * x, axis=-1, keepdims=True)
            hn = x * lax.rsqrt(ms + NORM_EPS)
            hn_ref[pl.ds(r0, row_chunk), :] = hn.astype(BF16)
            if phases:
                for cc in range(x.shape[1] // LANES):
                    hnf_ref[cc, pl.ds(r0, row_chunk), :] = hn[:, cc * LANES:(cc + 1) * LANES]
            return carry

        lax.fori_loop(0, tm // row_chunk, norm_rows, 0)
        if phases:
            per = tm // phases
            for p in range(phases):
                for cc in range(hnf_ref.shape[0]):
                    hnp_ref[p * per:(p + 1) * per, cc * LANES:(cc + 1) * LANES] = (
                        hnf_ref[cc, pl.ds(p, per, stride=phases), :].astype(BF16))
        if kind == "aux":
            oaux_ref[...] = _nt_dot(hn_ref[...], waux_ref[...])

    def project(src_ref):
        sub = min(tm, 512)
        for r0 in range(0, tm, sub):
            lhs = src_ref[r0:r0 + sub, :]
            prod = _nt_dot(lhs, w_ref[...]) if kind == "aux" else _dot(lhs, w_ref[...])
            o_ref[r0:r0 + sub, :] = prod.astype(o_ref.dtype)

    if phases:
        @pl.when(j < n_phase_tiles)
        def _():
            project(hnp_ref)

        @pl.when(j >= n_phase_tiles)
        def _():
            project(hn_ref)
    else:
        project(hn_ref)


def _norm_proj_t(x2d, w_t, w_aux_t, *, tm, tn):
    t, d = x2d.shape
    n = w_t.shape[0]
    return pl.pallas_call(
        functools.partial(_norm_proj_body, kind="aux", row_chunk=256, phases=0, n_phase_tiles=0),
        grid=(t // tm, n // tn),
        in_specs=[pl.BlockSpec((tm, d), lambda i, j: (i, 0)),
                  pl.BlockSpec((tn, d), lambda i, j: (j, 0)),
                  pl.BlockSpec((LANES, d), lambda i, j: (0, 0))],
        out_specs=[pl.BlockSpec((tm, tn), lambda i, j: (i, j)),
                   pl.BlockSpec((tm, LANES), lambda i, j: (i, 0))],
        out_shape=[jax.ShapeDtypeStruct((t, n), BF16), jax.ShapeDtypeStruct((t, LANES), F32)],
        scratch_shapes=[pltpu.VMEM((tm, d), BF16)],
        compiler_params=pltpu.CompilerParams(
            dimension_semantics=("parallel", "arbitrary"), vmem_limit_bytes=VMEM_LIMIT),
        name="norm_proj_aux",
    )(x2d, w_t, w_aux_t)


def _norm_proj_phased(x2d, w, *, tm, tn, phases, phase_cols):
    t, d = x2d.shape
    n = w.shape[1]
    return pl.pallas_call(
        functools.partial(_norm_proj_body, kind="phased", row_chunk=256, phases=phases,
                          n_phase_tiles=phase_cols // tn),
        grid=(t // tm, n // tn),
        in_specs=[pl.BlockSpec((tm, d), lambda i, j: (i, 0)),
                  pl.BlockSpec((d, tn), lambda i, j: (0, j))],
        out_specs=pl.BlockSpec((tm, tn), lambda i, j: (i, j)),
        out_shape=jax.ShapeDtypeStruct((t, n), BF16),
        scratch_shapes=[pltpu.VMEM((tm, d), BF16), pltpu.VMEM((tm, d), BF16),
                        pltpu.VMEM((d // LANES, tm, LANES), F32)],
        compiler_params=pltpu.CompilerParams(
            dimension_semantics=("parallel", "arbitrary"), vmem_limit_bytes=VMEM_LIMIT),
        name="norm_proj",
    )(x2d, w)


def _split3(a):
    hi = a.astype(BF16)
    r = a - hi.astype(F32)
    mid = r.astype(BF16)
    lo = (r - mid.astype(F32)).astype(BF16)
    return hi, mid, lo


def _expand_heads(a, e):
    hi, mid, lo = _split3(a)
    return _dot(hi, e) + _dot(mid, e) + _dot(lo, e)


def _ssd_body(z_ref, xs_ref, bc_ref, dt_ref, cw_ref, cb_ref, dtb_ref, alog_ref, dskip_ref, nrm_ref,
              e_ref, tri_ref, y_ref, state_ref, ubuf_ref):
    c = pl.program_id(1)
    L = SSD_CHUNK
    tail = 8

    @pl.when(c == 0)
    def _():
        state_ref[...] = jnp.zeros_like(state_ref)
        ubuf_ref[0:tail, :] = jnp.zeros((tail, SSD_CONV_DIM), F32)

    row = lax.broadcasted_iota(jnp.int32, (L, L), 0)
    col = lax.broadcasted_iota(jnp.int32, (L, L), 1)
    causal = col <= row
    lane = lax.broadcasted_iota(jnp.int32, (L, LANES), 1)
    low_half = lane < SSD_HEAD_DIM
    heads_per_group = SSD_HEADS // SSD_GROUPS
    gw = heads_per_group * SSD_HEAD_DIM

    def chunk(r):
        u = jnp.concatenate([xs_ref[r, :], bc_ref[r, :]], axis=1).astype(F32)
        ubuf_ref[tail:tail + L, :] = u
        cw = cw_ref[...]
        acc = cb_ref[...] + cw[SSD_CONV - 1:SSD_CONV, :] * u
        for k in range(SSD_CONV - 1):
            acc = acc + cw[k:k + 1, :] * ubuf_ref[pl.ds(tail - (SSD_CONV - 1) + k, L), :]
        ubuf_ref[0:tail, :] = u[L - tail:L, :]
        act = acc * _sigmoid(acc)

        xs = act[:, :SSD_INNER]

        dtr = dt_ref[r, :] + dtb_ref[...]
        dt = jnp.maximum(dtr, 0.0) + jnp.log1p(jnp.exp(-jnp.abs(dtr)))
        a = -jnp.exp(alog_ref[...]) * math.log2(math.e)
        ac = dt * a
        tri = tri_ref[...]
        hi, mid, lo = _split3(ac)
        a_cum = _dot(tri, hi) + _dot(tri, mid) + _dot(tri, lo)
        a_last = a_cum[L - 1:L, :]
        dec_states = jnp.exp2(a_last - a_cum)
        exp_acum = jnp.exp2(a_cum)
        e = e_ref[...]
        dt_e = _expand_heads(dt, e)
        dec_e = _expand_heads(dec_states, e)
        ea_e = _expand_heads(exp_acum, e)
        a_cum_t = a_cum.T

        xc = xs * dt_e
        xc_b = xc.astype(BF16)
        xdec_b = (xc * dec_e).astype(BF16)

        for g in range(SSD_GROUPS):
            b_g = act[:, SSD_INNER + g * SSD_STATE:SSD_INNER + (g + 1) * SSD_STATE]
            c_g = act[:, SSD_INNER + (SSD_GROUPS + g) * SSD_STATE:
                      SSD_INNER + (SSD_GROUPS + g + 1) * SSD_STATE]
            b_gb = b_g.astype(BF16)
            c_gb = c_g.astype(BF16)
            scores = _nt_dot(c_gb, b_gb)
            gs = slice(g * gw, (g + 1) * gw)

            prev = state_ref[:, gs]
            y_off = _dot(c_gb, prev.astype(BF16))
            st_new = _dot(b_g.T.astype(BF16), xdec_b[:, gs])
            state_ref[:, gs] = prev * ea_e[L - 1:L, gs] + st_new

            pieces = []
            for pair in range(heads_per_group // 2):
                ms = []
                for hh in range(2):
                    h = g * heads_per_group + 2 * pair + hh
                    seg = a_cum[:, h:h + 1] - a_cum_t[h:h + 1, :]
                    lmat = jnp.exp2(jnp.where(causal, seg, NEG_BIG))
                    ms.append((scores * lmat).astype(BF16))
                m_pair = jnp.concatenate(ms, axis=1)
                cs = slice(g * gw + pair * LANES, g * gw + (pair + 1) * LANES)
                x_pair = xc_b[:, cs]
                zero = jnp.zeros_like(x_pair)
                rhs = jnp.concatenate([jnp.where(low_half, x_pair, zero),
                                       jnp.where(low_half, zero, x_pair)], axis=0)
                pieces.append(_dot(m_pair, rhs))
            y_diag = jnp.concatenate(pieces, axis=1)

            y = y_diag + y_off * ea_e[:, gs] + xs[:, gs] * dskip_ref[:, gs]
            zg = z_ref[r, gs].astype(F32)
            ug = y * (zg * _sigmoid(zg))
            ug = ug * lax.rsqrt(jnp.mean(ug * ug, axis=-1, keepdims=True) + NORM_EPS)
            y_ref[r, gs] = (ug * nrm_ref[:, gs]).astype(y_ref.dtype)

    for ci in range(xs_ref.shape[0] // L):
        chunk(slice(ci * L, (ci + 1) * L))


def _ssd(proj, dt_raw, conv_w, conv_b, dt_bias, a_log, d_skip, ssd_norm, bsz, seq, *, chunks_per_step=4):
    t = bsz * seq
    L = SSD_CHUNK
    R = chunks_per_step * L
    nc = seq // R

    def pad_heads(v):
        return jnp.pad(v.astype(F32), (0, LANES - SSD_HEADS)).reshape(1, LANES)

    head_of_chan = jnp.arange(SSD_INNER, dtype=jnp.int32) // SSD_HEAD_DIM
    expand = (jnp.arange(LANES, dtype=jnp.int32)[:, None] == head_of_chan[None, :]).astype(BF16)
    tri = (jnp.arange(L)[:, None] >= jnp.arange(L)[None, :]).astype(BF16)
    d_chan = jnp.repeat(d_skip.astype(F32), SSD_HEAD_DIM).reshape(1, SSD_INNER)

    def rows(b, c):
        return b * nc + c

    const = lambda b, c: (0, 0)
    return pl.pallas_call(
        _ssd_body,
        grid=(bsz, nc),
        in_specs=[
            pl.BlockSpec((R, SSD_INNER), lambda b, c: (rows(b, c), P0_Z // SSD_INNER)),
            pl.BlockSpec((R, SSD_INNER), lambda b, c: (rows(b, c), P0_X // SSD_INNER)),
            pl.BlockSpec((R, SSD_BC), lambda b, c: (rows(b, c), P0_BC // SSD_BC)),
            pl.BlockSpec((R, LANES), lambda b, c: (rows(b, c), 0)),
            pl.BlockSpec((SSD_CONV, SSD_CONV_DIM), const),
            pl.BlockSpec((1, SSD_CONV_DIM), const),
            pl.BlockSpec((1, LANES), const),
            pl.BlockSpec((1, LANES), const),
            pl.BlockSpec((1, SSD_INNER), const),
            pl.BlockSpec((1, SSD_INNER), const),
            pl.BlockSpec((LANES, SSD_INNER), const),
            pl.BlockSpec((L, L), const),
        ],
        out_specs=pl.BlockSpec((R, SSD_INNER), lambda b, c: (rows(b, c), 0)),
        out_shape=jax.ShapeDtypeStruct((t, SSD_INNER), BF16),
        scratch_shapes=[pltpu.VMEM((SSD_STATE, SSD_INNER), F32),
                        pltpu.VMEM((8 + L, SSD_CONV_DIM), F32)],
        compiler_params=pltpu.CompilerParams(
            dimension_semantics=("parallel", "arbitrary"), vmem_limit_bytes=VMEM_LIMIT),
        name="ssd_mixer",
    )(proj, proj, proj, dt_raw, conv_w.astype(F32), conv_b.reshape(1, -1).astype(F32),
      pad_heads(dt_bias), pad_heads(a_log), d_chan, ssd_norm.reshape(1, -1).astype(F32), expand, tri)


def _moba_body(q_ref, k_ref, v_ref, avg_ref, hot_ref, eye_ref, o_ref, *, nb, q_tile):
    blk = MOBA_BLOCK
    n_dense = MOBA_TOPK + 1

    kmean = _dot(avg_ref[...], k_ref[...]).astype(BF16)
    jrow = lax.broadcasted_iota(jnp.int32, (nb, blk), 0)
    row = lax.broadcasted_iota(jnp.int32, (q_tile, blk), 0)
    col = lax.broadcasted_iota(jnp.int32, (q_tile, blk), 1)

    def query_block(i):
        q_i = q_ref[i * blk:(i + 1) * blk, :]
        if i < n_dense:
            return q_i, None
        gate_t = _nt_dot(kmean, q_i)[0:nb, :]
        valid = jrow < i
        gm = jnp.where(valid, gate_t, -jnp.inf)
        rank = jnp.zeros((nb, blk), F32)
        for r in range(1, nb):
            gr = pltpu.roll(gm, r, axis=0)
            lower = ((jrow - r) & (nb - 1)) < jrow
            rank = rank + jnp.where(gr > gm, 1.0, 0.0) + jnp.where((gr == gm) & lower, 1.0, 0.0)
        keep = valid & (rank < float(MOBA_TOPK))
        bias_t = jnp.where(keep, 0.0, NEG_BIG)
        bias_t = jnp.concatenate([bias_t, jnp.zeros((LANES - nb, blk), F32)], axis=0).astype(BF16)
        bias = _nt_dot(eye_ref[...], bias_t).astype(BF16)
        return q_i, jnp.concatenate([q_i, bias], axis=1)

    def scores(i, h, q_i, q_aug):
        rs = slice(h * q_tile, (h + 1) * q_tile)
        s_own = _nt_dot(q_i[rs, :], k_ref[i * blk:(i + 1) * blk, :])
        s_own = jnp.where(col <= row + h * q_tile, s_own, NEG_BIG)
        if i == 0:
            return s_own, None
        if q_aug is None:
            return s_own, _nt_dot(q_i[rs, :], k_ref[0:i * blk, :])
        k_past = jnp.concatenate([k_ref[0:i * blk, :], hot_ref[0:i * blk, :]], axis=1)
        return s_own, _nt_dot(q_aug[rs, :], k_past)

    def finish(i, h, s_own, s_past):
        m = jnp.max(s_own, axis=-1, keepdims=True)
        if s_past is None:
            p = jnp.exp2(s_own - m)
        else:
            m = jnp.maximum(m, jnp.max(s_past, axis=-1, keepdims=True))
            p = jnp.concatenate([jnp.exp2(s_past - m), jnp.exp2(s_own - m)], axis=1)
        vb = v_ref[0:(i + 1) * blk, :]
        pv = _dot(p.astype(BF16), jnp.concatenate([vb, jnp.ones_like(vb)], axis=1))
        d = vb.shape[1]
        o_ref[i * blk + h * q_tile:i * blk + (h + 1) * q_tile, :] = (pv[:, :d] / pv[:, d:]).astype(o_ref.dtype)

    tiles = [(i, h) for i in range(nb) for h in range(blk // q_tile)]
    qcache = {}
    pending = None
    for i, h in tiles:
        if i not in qcache:
            qcache = {i: query_block(i)}
        s_next = scores(i, h, *qcache[i])
        if pending is not None:
            finish(*pending)
        pending = (i, h) + s_next
    finish(*pending)


def _moba(proj, bsz, seq):
    t = bsz * seq
    blk = MOBA_BLOCK
    nb = seq // blk
    d = MOBA_HEAD_DIM
    blk_of_key = jnp.arange(seq, dtype=jnp.int32) // blk
    avg = (jnp.arange(2 * nb, dtype=jnp.int32)[:, None] == blk_of_key[None, :]).astype(BF16) / blk
    hot = (blk_of_key[:, None] == jnp.arange(LANES, dtype=jnp.int32)[None, :]).astype(BF16)
    eye = jnp.eye(blk, dtype=BF16)

    def head_block(base):
        return pl.BlockSpec((seq, d), lambda b, h: (b, base // d + h))

    const = lambda b, h: (0, 0)
    return pl.pallas_call(
        functools.partial(_moba_body, nb=nb, q_tile=256),
        grid=(bsz, MOBA_HEADS),
        in_specs=[head_block(P0_Q), head_block(P0_K), head_block(P0_V),
                  pl.BlockSpec((2 * nb, seq), const), pl.BlockSpec((seq, LANES), const),
                  pl.BlockSpec((blk, blk), const)],
        out_specs=pl.BlockSpec((seq, d), lambda b, h: (b, h)),
        out_shape=jax.ShapeDtypeStruct((t, MOBA_INNER), BF16),
        compiler_params=pltpu.CompilerParams(
            dimension_semantics=("parallel", "parallel"), vmem_limit_bytes=VMEM_LIMIT),
        name="moba_attn",
    )(proj, proj, proj, avg, hot, eye)


def _out_proj_body(*refs, n_plain, final_norm):
    plain = refs[:n_plain]
    a_ref, g_ref, x_ref, w_ref = refs[n_plain:n_plain + 4]
    o_ref = refs[-1]
    g = g_ref[...].astype(F32)
    gated = (a_ref[...].astype(F32) * (g * _sigmoid(g))).astype(BF16)
    k0 = 0
    y = x_ref[...]
    for r in plain:
        kw = r.shape[1]
        y = y + _dot(r[...], w_ref[k0:k0 + kw, :])
        k0 += kw
    y = y + _dot(gated, w_ref[k0:k0 + gated.shape[1], :])
    if final_norm:
        fn_ref = refs[-2]
        y = y * lax.rsqrt(jnp.mean(y * y, axis=-1, keepdims=True) + NORM_EPS) * fn_ref[...]
    o_ref[...] = y


def _out_proj(plain, gated, gate_src, gate_col, x2d, w, final_gain=None, *, tm=1024):
    t, d = x2d.shape
    kg = gated.shape[1]
    in_specs = [pl.BlockSpec((tm, a.shape[1]), lambda i: (i, 0)) for a in plain]
    in_specs += [pl.BlockSpec((tm, kg), lambda i: (i, 0)),
                 pl.BlockSpec((tm, kg), lambda i: (i, gate_col // kg)),
                 pl.BlockSpec((tm, d), lambda i: (i, 0)),
                 pl.BlockSpec(w.shape, lambda i: (0, 0))]
    args = list(plain) + [gated, gate_src, x2d, w]
    if final_gain is not None:
        in_specs.append(pl.BlockSpec((1, d), lambda i: (0, 0)))
        args.append(final_gain.reshape(1, d).astype(F32))
    return pl.pallas_call(
        functools.partial(_out_proj_body, n_plain=len(plain), final_norm=final_gain is not None),
        grid=(t // tm,),
        in_specs=in_specs,
        out_specs=pl.BlockSpec((tm, d), lambda i: (i, 0)),
        out_shape=jax.ShapeDtypeStruct((t, d), F32),
        compiler_params=pltpu.CompilerParams(
            dimension_semantics=("parallel",), vmem_limit_bytes=VMEM_LIMIT),
        name="out_proj_final" if final_gain is not None else "out_proj",
    )(*args)


DIL_PHASES = 8


def _dil_tiles(seq):
    ph, qb = DIL_PHASES, DIL_QBLOCK
    per = seq // ph
    tiles = []
    rows = qb // ph
    for n in range(seq // qb):
        qs = [(p * per + rows * n, rows) for p in range(ph)]
        if n == 0:
            ks, mask = [(p * per, rows) for p in range(ph)], "a_first"
        else:
            ks, mask = [(p * per + rows * (n - 1), 2 * rows) for p in range(ph)], "a"
        tiles.append((qs, ks, mask, "init"))
    seg = ph // 4
    rows = qb // seg
    for p4 in range(4):
        for n in range(seq // 4 // qb):
            qs = [((p4 + 4 * j) * per + rows * n, rows) for j in range(seg)]
            if n == 0:
                ks, mask = [((p4 + 4 * j) * per, rows) for j in range(seg)], "b_first"
            else:
                ks, mask = [((p4 + 4 * j) * per + rows * (n - 1), 2 * rows) for j in range(seg)], "b"
            tiles.append((qs, ks, mask, "merge"))
    for p in range(ph):
        for h in range(per // qb):
            qs = [(p * per + qb * h, qb)]
            ks = [(p * per, qb * (h + 1))]
            tiles.append((qs, ks, "c%d" % h, "final"))
    return tiles


def _dil_masks(seq):
    import numpy as np
    ph, qb = DIL_PHASES, DIL_QBLOCK
    per = seq // ph

    def pos(n_rows, seg_rows, step):
        i = np.arange(n_rows)
        return step * (i % seg_rows) + i // seg_rows

    def band(qpos, kpos, span):
        dist = qpos[:, None] - kpos[None, :]
        return np.where((dist >= 0) & (dist <= span), 0.0, NEG_BIG).astype(np.float32)

    masks = {}
    for name, seg in (("a", ph), ("b", ph // 4)):
        rows = qb // seg
        masks[name] = band(qb + pos(qb, rows, seg), pos(2 * qb, 2 * rows, seg), qb)
        masks[name + "_first"] = band(pos(qb, rows, seg), pos(qb, rows, seg), qb)
    for h in range(per // qb):
        lq = qb * h + np.arange(qb)
        lk = np.arange(qb * (h + 1))
        same = (lq[:, None] - lk[None, :]) % 2 == 0
        masks["c%d" % h] = np.where(same & (lk[None, :] <= lq[:, None]), 0.0, NEG_BIG).astype(np.float32)
    return masks


def _dil_body(*refs, seq, mask_names, group_size):
    q_ref, k_ref, v_ref = refs[:3]
    mask_refs = dict(zip(mask_names, refs[3:3 + len(mask_names)]))
    o_ref, onat_ref = refs[3 + len(mask_names):][:2]
    state_a, state_b = refs[-6:-3], refs[-3:]
    state_in = {"merge": state_a, "final": state_b}
    state_out = {"init": state_a, "merge": state_b}
    qb = DIL_QBLOCK
    d = DIL_HEAD_DIM

    def rows_of(ref, slices):
        parts = [ref[s:s + n, :] for s, n in slices]
        return parts[0] if len(parts) == 1 else jnp.concatenate(parts, axis=0)

    def put_rows(ref, slices, val):
        r0 = 0
        for s, n in slices:
            ref[s:s + n, :] = val[r0:r0 + n, :]
            r0 += n

    def cat(parts):
        return parts[0] if len(parts) == 1 else jnp.concatenate(parts, axis=0)

    def scores(group):
        mask = mask_refs[group[0][2]][...]
        return cat([_nt_dot(rows_of(q_ref, qs), rows_of(k_ref, ks)) + mask for qs, ks, _, _ in group])

    def probs(group, s):
        stage = group[0][3]
        m_new = jnp.broadcast_to(jnp.max(s, axis=-1, keepdims=True), (s.shape[0], d))
        if stage == "init":
            m_old = None
        else:
            m_ref = state_in[stage][1]
            m_old = cat([rows_of(m_ref, qs) for qs, _, _, _ in group])
            m_new = jnp.maximum(m_old, m_new)
        p = jnp.concatenate([jnp.exp2(s[:, c0:c0 + d] - m_new) for c0 in range(0, s.shape[1], d)], axis=1)
        return p.astype(BF16), m_new, m_old

    def accumulate(group, p, m_new, m_old):
        stage = group[0][3]
        pvs = []
        for g, (_, ks, _, _) in enumerate(group):
            vb = rows_of(v_ref, ks)
            pvs.append(_dot(p[g * qb:(g + 1) * qb, :], jnp.concatenate([vb, jnp.ones_like(vb)], axis=1)))
        pv = cat(pvs)
        acc, l = pv[:, :d], pv[:, d:]
        m_b = m_new
        if stage != "init":
            alpha = jnp.exp2(m_old - m_b)
            acc_ref, _, l_ref = state_in[stage]
            acc = alpha * cat([rows_of(acc_ref, qs) for qs, _, _, _ in group]) + acc
            l = alpha * cat([rows_of(l_ref, qs) for qs, _, _, _ in group]) + l
        if stage == "final":
            out = acc / l
        for g, (qs, _, _, _) in enumerate(group):
            rs = slice(g * qb, (g + 1) * qb)
            if stage == "final":
                (s0, _), = qs
                p_idx, l0 = divmod(s0, seq // DIL_PHASES)
                onat_ref[pl.ds(DIL_PHASES * l0 + p_idx, qb, stride=DIL_PHASES), :] = out[rs, :]
            else:
                acc_out, m_out, l_out = state_out[stage]
                put_rows(acc_out, qs, acc[rs, :])
                put_rows(l_out, qs, l[rs, :])
                put_rows(m_out, qs, m_b[rs, :])

    groups = []
    for tile in _dil_tiles(seq):
        if groups and len(groups[-1]) < group_size and groups[-1][0][2:] == tile[2:]:
            groups[-1].append(tile)
        else:
            groups.append([tile])

    def owned(group):
        return {r for qs, _, _, _ in group for s0, n in qs for r in range(s0, s0 + n)}

    for ga, gb in zip(groups, groups[1:]):
        assert not owned(ga) & owned(gb)
    scored, ready = None, None
    for group in groups + [None, None]:
        s_new = None if group is None else (group, scores(group))
        p_new = None if scored is None else (scored[0],) + probs(*scored)
        if ready is not None:
            accumulate(*ready)
        scored, ready = s_new, p_new
    o_ref[...] = onat_ref[...].astype(o_ref.dtype)


def _dilated(proj, bsz, seq):
    t = bsz * seq
    d = DIL_HEAD_DIM
    masks = _dil_masks(seq)
    names = tuple(sorted(masks))

    def head_block(base):
        return pl.BlockSpec((seq, d), lambda b, h: (b, base // d + h))

    return pl.pallas_call(
        functools.partial(_dil_body, seq=seq, mask_names=names, group_size=1),
        grid=(bsz, DIL_HEADS),
        in_specs=[head_block(0), head_block(DIL_INNER), head_block(2 * DIL_INNER)]
        + [pl.BlockSpec(masks[n].shape, lambda b, h: (0, 0)) for n in names],
        out_specs=pl.BlockSpec((seq, d), lambda b, h: (b, h)),
        out_shape=jax.ShapeDtypeStruct((t, DIL_INNER), BF16),
        scratch_shapes=[pltpu.VMEM((seq, d), F32)] * 7,
        compiler_params=pltpu.CompilerParams(
            dimension_semantics=("parallel", "parallel"), vmem_limit_bytes=VMEM_LIMIT),
        name="dilated_attn",
    )(proj, proj, proj, *[jnp.asarray(masks[n]) for n in names])


def kernel(x, even_norm, even_w_in, ssd_conv_w, ssd_conv_b, ssd_dt_bias, ssd_a_log, ssd_d, ssd_norm,
           even_w_out, odd_norm, odd_w_in, odd_w_out, final_norm):
    bsz, seq, d = x.shape
    t = bsz * seq
    x2d = x.reshape(t, d)

    w = even_w_in[0]
    z_end = SSD_INNER
    xbc_end = z_end + SSD_CONV_DIM
    dt_end = xbc_end + SSD_HEADS
    q_end = dt_end + MOBA_INNER
    v_end = q_end + 2 * MOBA_INNER
    x_end = z_end + SSD_INNER
    moba_scale = MOBA_HEAD_DIM ** -0.5 * math.log2(math.e)
    col_scale = jnp.ones((w.shape[1], 1), F32).at[dt_end:q_end].set(moba_scale)
    w_t = (w.T * col_scale * even_norm[0].astype(F32)[None, :]).astype(BF16)
    w_main = jnp.concatenate([w_t[:x_end], w_t[v_end:], w_t[dt_end:v_end], w_t[x_end:xbc_end]],
                             axis=0)
    w_dt = jnp.pad(w_t[xbc_end:dt_end], ((0, LANES - SSD_HEADS), (0, 0)))
    proj0, dt_raw = _norm_proj_t(x2d, w_main, w_dt, tm=seq, tn=P0_N // 4)
    y_ssd = _ssd(proj0, dt_raw, ssd_conv_w[0], ssd_conv_b[0], ssd_dt_bias[0], ssd_a_log[0],
                 ssd_d[0], ssd_norm[0], bsz, seq)
    o_moba = _moba(proj0, bsz, seq)
    x1 = _out_proj([y_ssd], o_moba, proj0, P0_G, x2d, even_w_out[0].astype(BF16))

    w1 = odd_w_in[0]
    dil_scale = DIL_HEAD_DIM ** -0.5 * math.log2(math.e)
    col_scale1 = jnp.ones((w1.shape[1],), F32).at[:DIL_INNER].set(dil_scale)
    w1_b = (w1 * col_scale1[None, :] * odd_norm[0].astype(F32)[:, None]).astype(BF16)
    proj1 = _norm_proj_phased(x1, w1_b, tm=seq, tn=1024, phases=DIL_PHASES, phase_cols=3 * DIL_INNER)
    o_dil = _dilated(proj1, bsz, seq)
    out = _out_proj([], o_dil, proj1, 3 * DIL_INNER, x1, odd_w_out[0].astype(BF16), final_norm)
    return out.reshape(bsz, seq, d)
```

```python
import functools
import math

import jax
import jax.numpy as jnp
from jax import lax
from jax.experimental import pallas as pl
from jax.experimental.pallas import tpu as pltpu

F32 = jnp.float32
BF16 = jnp.bfloat16

NORM_EPS = 1e-5
D_MODEL = 1024

SSD_HEADS = 16
SSD_HEAD_DIM = 64
SSD_INNER = SSD_HEADS * SSD_HEAD_DIM
SSD_GROUPS = 2
SSD_STATE = 128
SSD_CONV = 4
SSD_CHUNK = 128
SSD_BC = 2 * SSD_GROUPS * SSD_STATE
SSD_CONV_DIM = SSD_INNER + SSD_BC

MOBA_HEADS = 8
MOBA_HEAD_DIM = 128
MOBA_INNER = MOBA_HEADS * MOBA_HEAD_DIM
MOBA_BLOCK = 256
MOBA_TOPK = 3

DIL_HEADS = 16
DIL_HEAD_DIM = 128
DIL_INNER = DIL_HEADS * DIL_HEAD_DIM
DIL_QBLOCK = 128
DIL_PATTERNS = ((128, 1), (512, 4), (2048, 16))

LANES = 128
NEG_BIG = -1e30
VMEM_LIMIT = 56 * 1024 * 1024

P0_Z = 0
P0_X = P0_Z + SSD_INNER
P0_G = P0_X + SSD_INNER
P0_Q = P0_G + MOBA_INNER
P0_K = P0_Q + MOBA_INNER
P0_V = P0_K + MOBA_INNER
P0_BC = P0_V + MOBA_INNER
P0_N = P0_BC + SSD_BC


def _nt_dot(a, b):
    return lax.dot_general(a, b, (((1,), (1,)), ((), ())), preferred_element_type=F32)


def _dot(a, b):
    return jnp.dot(a, b, preferred_element_type=F32)


def _sigmoid(x):
    return 1.0 / (1.0 + jnp.exp2(x * -math.log2(math.e)))


def _norm_proj_body(*refs, kind, row_chunk, phases, n_phase_tiles):
    if kind == "aux":
        x_ref, w_ref, waux_ref, o_ref, oaux_ref, hn_ref = refs
    else:
        x_ref, w_ref, o_ref, hn_ref, hnp_ref, hnf_ref = refs
    j = pl.program_id(1)
    tm = x_ref.shape[0]

    @pl.when(j == 0)
    def _():
        def norm_rows(c, carry):
            r0 = pl.multiple_of(c * row_chunk, row_chunk)
            x = x_ref[pl.ds(r0, row_chunk), :]
            ms = jnp.mean(x * x, axis=-1, keepdims=True)
            hn = x * lax.rsqrt(ms + NORM_EPS)
            hn_ref[pl.ds(r0, row_chunk), :] = hn.astype(BF16)
            if phases:
                for cc in range(x.shape[1] // LANES):
                    hnf_ref[cc, pl.ds(r0, row_chunk), :] = hn[:, cc * LANES:(cc + 1) * LANES]
            return carry

        lax.fori_loop(0, tm // row_chunk, norm_rows, 0)
        if phases:
            per = tm // phases
            for p in range(phases):
                for cc in range(hnf_ref.shape[0]):
                    hnp_ref[p * per:(p + 1) * per, cc * LANES:(cc + 1) * LANES] = (
                        hnf_ref[cc, pl.ds(p, per, stride=phases), :].astype(BF16))
        if kind == "aux":
            oaux_ref[...] = _nt_dot(hn_ref[...], waux_ref[...])

    def project(src_ref):
        sub = min(tm, 512)
        for r0 in range(0, tm, sub):
            lhs = src_ref[r0:r0 + sub, :]
            prod = _nt_dot(lhs, w_ref[...]) if kind == "aux" else _dot(lhs, w_ref[...])
            o_ref[r0:r0 + sub, :] = prod.astype(o_ref.dtype)

    if phases:
        @pl.when(j < n_phase_tiles)
        def _():
            project(hnp_ref)

        @pl.when(j >= n_phase_tiles)
        def _():
            project(hn_ref)
    else:
        project(hn_ref)


def _norm_proj_t(x2d, w_t, w_aux_t, *, tm, tn):
    t, d = x2d.shape
    n = w_t.shape[0]
    return pl.pallas_call(
        functools.partial(_norm_proj_body, kind="aux", row_chunk=256, phases=0, n_phase_tiles=0),
        grid=(t // tm, n // tn),
        in_specs=[pl.BlockSpec((tm, d), lambda i, j: (i, 0)),
                  pl.BlockSpec((tn, d), lambda i, j: (j, 0)),
                  pl.BlockSpec((LANES, d), lambda i, j: (0, 0))],
        out_specs=[pl.BlockSpec((tm, tn), lambda i, j: (i, j)),
                   pl.BlockSpec((tm, LANES), lambda i, j: (i, 0))],
        out_shape=[jax.ShapeDtypeStruct((t, n), BF16), jax.ShapeDtypeStruct((t, LANES), F32)],
        scratch_shapes=[pltpu.VMEM((tm, d), BF16)],
        compiler_params=pltpu.CompilerParams(
            dimension_semantics=("parallel", "arbitrary"), vmem_limit_bytes=VMEM_LIMIT),
        name="norm_proj_aux",
    )(x2d, w_t, w_aux_t)


def _norm_proj_phased(x2d, w, *, tm, tn, phases, phase_cols):
    t, d = x2d.shape
    n = w.shape[1]
    return pl.pallas_call(
        functools.partial(_norm_proj_body, kind="phased", row_chunk=256, phases=phases,
                          n_phase_tiles=phase_cols // tn),
        grid=(t // tm, n // tn),
        in_specs=[pl.BlockSpec((tm, d), lambda i, j: (i, 0)),
                  pl.BlockSpec((d, tn), lambda i, j: (0, j))],
        out_specs=pl.BlockSpec((tm, tn), lambda i, j: (i, j)),
        out_shape=jax.ShapeDtypeStruct((t, n), BF16),
        scratch_shapes=[pltpu.VMEM((tm, d), BF16), pltpu.VMEM((tm, d), BF16),
                        pltpu.VMEM((d // LANES, tm, LANES), F32)],
        compiler_params=pltpu.CompilerParams(
            dimension_semantics=("parallel", "arbitrary"), vmem_limit_bytes=VMEM_LIMIT),
        name="norm_proj",
    )(x2d, w)


def _split3(a):
    hi = a.astype(BF16)
    r = a - hi.astype(F32)
    mid = r.astype(BF16)
    lo = (r - mid.astype(F32)).astype(BF16)
    return hi, mid, lo


def _expand_heads(a, e):
    hi, mid, lo = _split3(a)
    return _dot(hi, e) + _dot(mid, e) + _dot(lo, e)


def _ssd_body(z_ref, xs_ref, bc_ref, dt_ref, cw_ref, cb_ref, dtb_ref, alog_ref, dskip_ref, nrm_ref,
              e_ref, tri_ref, y_ref, state_ref, ubuf_ref):
    c = pl.program_id(1)
    L = SSD_CHUNK
    tail = 8

    @pl.when(c == 0)
    def _():
        state_ref[...] = jnp.zeros_like(state_ref)
        ubuf_ref[0:tail, :] = jnp.zeros((tail, SSD_CONV_DIM), F32)

    row = lax.broadcasted_iota(jnp.int32, (L, L), 0)
    col = lax.broadcasted_iota(jnp.int32, (L, L), 1)
    causal = col <= row
    lane = lax.broadcasted_iota(jnp.int32, (L, LANES), 1)
    low_half = lane < SSD_HEAD_DIM
    heads_per_group = SSD_HEADS // SSD_GROUPS
    gw = heads_per_group * SSD_HEAD_DIM

    def chunk(r):
        u = jnp.concatenate([xs_ref[r, :], bc_ref[r, :]], axis=1).astype(F32)
        ubuf_ref[tail:tail + L, :] = u
        cw = cw_ref[...]
        acc = cb_ref[...] + cw[SSD_CONV - 1:SSD_CONV, :] * u
        for k in range(SSD_CONV - 1):
            acc = acc + cw[k:k + 1, :] * ubuf_ref[pl.ds(tail - (SSD_CONV - 1) + k, L), :]
        ubuf_ref[0:tail, :] = u[L - tail:L, :]
        act = acc * _sigmoid(acc)

        xs = act[:, :SSD_INNER]

        dtr = dt_ref[r, :] + dtb_ref[...]
        dt = jnp.maximum(dtr, 0.0) + jnp.log1p(jnp.exp(-jnp.abs(dtr)))
        a = -jnp.exp(alog_ref[...]) * math.log2(math.e)
        ac = dt * a
        tri = tri_ref[...]
        hi, mid, lo = _split3(ac)
        a_cum = _dot(tri, hi) + _dot(tri, mid) + _dot(tri, lo)
        a_last = a_cum[L - 1:L, :]
        dec_states = jnp.exp2(a_last - a_cum)
        exp_acum = jnp.exp2(a_cum)
        e = e_ref[...]
        dt_e = _expand_heads(dt, e)
        dec_e = _expand_heads(dec_states, e)
        ea_e = _expand_heads(exp_acum, e)
        a_cum_t = a_cum.T

        xc = xs * dt_e
        xc_b = xc.astype(BF16)
        xdec_b = (xc * dec_e).astype(BF16)

        for g in range(SSD_GROUPS):
            b_g = act[:, SSD_INNER + g * SSD_STATE:SSD_INNER + (g + 1) * SSD_STATE]
            c_g = act[:, SSD_INNER + (SSD_GROUPS + g) * SSD_STATE:
                      SSD_INNER + (SSD_GROUPS + g + 1) * SSD_STATE]
            b_gb = b_g.astype(BF16)
            c_gb = c_g.astype(BF16)
            scores = _nt_dot(c_gb, b_gb)
            gs = slice(g * gw, (g + 1) * gw)

            prev = state_ref[:, gs]
            y_off = _dot(c_gb, prev.astype(BF16))
            st_new = _dot(b_g.T.astype(BF16), xdec_b[:, gs])
            state_ref[:, gs] = prev * ea_e[L - 1:L, gs] + st_new

            pieces = []
            for pair in range(heads_per_group // 2):
                ms = []
                for hh in range(2):
                    h = g * heads_per_group + 2 * pair + hh
                    seg = a_cum[:, h:h + 1] - a_cum_t[h:h + 1, :]
                    lmat = jnp.exp2(jnp.where(causal, seg, NEG_BIG))
                    ms.append((scores * lmat).astype(BF16))
                m_pair = jnp.concatenate(ms, axis=1)
                cs = slice(g * gw + pair * LANES, g * gw + (pair + 1) * LANES)
                x_pair = xc_b[:, cs]
                zero = jnp.zeros_like(x_pair)
                rhs = jnp.concatenate([jnp.where(low_half, x_pair, zero),
                                       jnp.where(low_half, zero, x_pair)], axis=0)
                pieces.append(_dot(m_pair, rhs))
            y_diag = jnp.concatenate(pieces, axis=1)

            y = y_diag + y_off * ea_e[:, gs] + xs[:, gs] * dskip_ref[:, gs]
            zg = z_ref[r, gs].astype(F32)
            ug = y * (zg * _sigmoid(zg))
            ug = ug * lax.rsqrt(jnp.mean(ug * ug, axis=-1, keepdims=True) + NORM_EPS)
            y_ref[r, gs] = (ug * nrm_ref[:, gs]).astype(y_ref.dtype)

    for ci in range(xs_ref.shape[0] // L):
        chunk(slice(ci * L, (ci + 1) * L))


def _ssd(proj, dt_raw, conv_w, conv_b, dt_bias, a_log, d_skip, ssd_norm, bsz, seq, *, chunks_per_step=4):
    t = bsz * seq
    L = SSD_CHUNK
    R = chunks_per_step * L
    nc = seq // R

    def pad_heads(v):
        return jnp.pad(v.astype(F32), (0, LANES - SSD_HEADS)).reshape(1, LANES)

    head_of_chan = jnp.arange(SSD_INNER, dtype=jnp.int32) // SSD_HEAD_DIM
    expand = (jnp.arange(LANES, dtype=jnp.int32)[:, None] == head_of_chan[None, :]).astype(BF16)
    tri = (jnp.arange(L)[:, None] >= jnp.arange(L)[None, :]).astype(BF16)
    d_chan = jnp.repeat(d_skip.astype(F32), SSD_HEAD_DIM).reshape(1, SSD_INNER)

    def rows(b, c):
        return b * nc + c

    const = lambda b, c: (0, 0)
    return pl.pallas_call(
        _ssd_body,
        grid=(bsz, nc),
        in_specs=[
            pl.BlockSpec((R, SSD_INNER), lambda b, c: (rows(b, c), P0_Z // SSD_INNER)),
            pl.BlockSpec((R, SSD_INNER), lambda b, c: (rows(b, c), P0_X // SSD_INNER)),
            pl.BlockSpec((R, SSD_BC), lambda b, c: (rows(b, c), P0_BC // SSD_BC)),
            pl.BlockSpec((R, LANES), lambda b, c: (rows(b, c), 0)),
            pl.BlockSpec((SSD_CONV, SSD_CONV_DIM), const),
            pl.BlockSpec((1, SSD_CONV_DIM), const),
            pl.BlockSpec((1, LANES), const),
            pl.BlockSpec((1, LANES), const),
            pl.BlockSpec((1, SSD_INNER), const),
            pl.BlockSpec((1, SSD_INNER), const),
            pl.BlockSpec((LANES, SSD_INNER), const),
            pl.BlockSpec((L, L), const),
        ],
        out_specs=pl.BlockSpec((R, SSD_INNER), lambda b, c: (rows(b, c), 0)),
        out_shape=jax.ShapeDtypeStruct((t, SSD_INNER), BF16),
        scratch_shapes=[pltpu.VMEM((SSD_STATE, SSD_INNER), F32),
                        pltpu.VMEM((8 + L, SSD_CONV_DIM), F32)],
        compiler_params=pltpu.CompilerParams(
            dimension_semantics=("parallel", "arbitrary"), vmem_limit_bytes=VMEM_LIMIT),
        name="ssd_mixer",
    )(proj, proj, proj, dt_raw, conv_w.astype(F32), conv_b.reshape(1, -1).astype(F32),
      pad_heads(dt_bias), pad_heads(a_log), d_chan, ssd_norm.reshape(1, -1).astype(F32), expand, tri)


def _moba_body(q_ref, k_ref, v_ref, avg_ref, hot_ref, eye_ref, o_ref, *, nb, q_tile):
    blk = MOBA_BLOCK
    n_dense = MOBA_TOPK + 1

    kmean = _dot(avg_ref[...], k_ref[...]).astype(BF16)
    jrow = lax.broadcasted_iota(jnp.int32, (nb, blk), 0)
    row = lax.broadcasted_iota(jnp.int32, (q_tile, blk), 0)
    col = lax.broadcasted_iota(jnp.int32, (q_tile, blk), 1)

    def query_block(i):
        q_i = q_ref[i * blk:(i + 1) * blk, :]
        if i < n_dense:
            return q_i, None
        gate_t = _nt_dot(kmean, q_i)[0:nb, :]
        valid = jrow < i
        gm = jnp.where(valid, gate_t, -jnp.inf)
        rank = jnp.zeros((nb, blk), F32)
        for r in range(1, nb):
            gr = pltpu.roll(gm, r, axis=0)
            lower = ((jrow - r) & (nb - 1)) < jrow
            rank = rank + jnp.where(gr > gm, 1.0, 0.0) + jnp.where((gr == gm) & lower, 1.0, 0.0)
        keep = valid & (rank < float(MOBA_TOPK))
        bias_t = jnp.where(keep, 0.0, NEG_BIG)
        bias_t = jnp.concatenate([bias_t, jnp.zeros((LANES - nb, blk), F32)], axis=0).astype(BF16)
        bias = _nt_dot(eye_ref[...], bias_t).astype(BF16)
        return q_i, jnp.concatenate([q_i, bias], axis=1)

    def scores(i, h, q_i, q_aug):
        rs = slice(h * q_tile, (h + 1) * q_tile)
        s_own = _nt_dot(q_i[rs, :], k_ref[i * blk:(i + 1) * blk, :])
        s_own = jnp.where(col <= row + h * q_tile, s_own, NEG_BIG)
        if i == 0:
            return s_own, None
        if q_aug is None:
            return s_own, _nt_dot(q_i[rs, :], k_ref[0:i * blk, :])
        k_past = jnp.concatenate([k_ref[0:i * blk, :], hot_ref[0:i * blk, :]], axis=1)
        return s_own, _nt_dot(q_aug[rs, :], k_past)

    def finish(i, h, s_own, s_past):
        m = jnp.max(s_own, axis=-1, keepdims=True)
        if s_past is None:
            p = jnp.exp2(s_own - m)
        else:
            m = jnp.maximum(m, jnp.max(s_past, axis=-1, keepdims=True))
            p = jnp.concatenate([jnp.exp2(s_past - m), jnp.exp2(s_own - m)], axis=1)
        vb = v_ref[0:(i + 1) * blk, :]
        pv = _dot(p.astype(BF16), jnp.concatenate([vb, jnp.ones_like(vb)], axis=1))
        d = vb.shape[1]
        o_ref[i * blk + h * q_tile:i * blk + (h + 1) * q_tile, :] = (pv[:, :d] / pv[:, d:]).astype(o_ref.dtype)

    tiles = [(i, h) for i in range(nb) for h in range(blk // q_tile)]
    qcache = {}
    pending = None
    for i, h in tiles:
        if i not in qcache:
            qcache = {i: query_block(i)}
        s_next = scores(i, h, *qcache[i])
        if pending is not None:
            finish(*pending)
        pending = (i, h) + s_next
    finish(*pending)


def _moba(proj, bsz, seq):
    t = bsz * seq
    blk = MOBA_BLOCK
    nb = seq // blk
    d = MOBA_HEAD_DIM
    blk_of_key = jnp.arange(seq, dtype=jnp.int32) // blk
    avg = (jnp.arange(2 * nb, dtype=jnp.int32)[:, None] == blk_of_key[None, :]).astype(BF16) / blk
    hot = (blk_of_key[:, None] == jnp.arange(LANES, dtype=jnp.int32)[None, :]).astype(BF16)
    eye = jnp.eye(blk, dtype=BF16)

    def head_block(base):
        return pl.BlockSpec((seq, d), lambda b, h: (b, base // d + h))

    const = lambda b, h: (0, 0)
    return pl.pallas_call(
        functools.partial(_moba_body, nb=nb, q_tile=256),
        grid=(bsz, MOBA_HEADS),
        in_specs=[head_block(P0_Q), head_block(P0_K), head_block(P0_V),
                  pl.BlockSpec((2 * nb, seq), const), pl.BlockSpec((seq, LANES), const),
                  pl.BlockSpec((blk, blk), const)],
        out_specs=pl.BlockSpec((seq, d), lambda b, h: (b, h)),
        out_shape=jax.ShapeDtypeStruct((t, MOBA_INNER), BF16),
        compiler_params=pltpu.CompilerParams(
            dimension_semantics=("parallel", "parallel"), vmem_limit_bytes=VMEM_LIMIT),
        name="moba_attn",
    )(proj, proj, proj, avg, hot, eye)


def _out_proj_body(*refs, n_plain, final_norm):
    plain = refs[:n_plain]
    a_ref, g_ref, x_ref, w_ref = refs[n_plain:n_plain + 4]
    o_ref = refs[-1]
    g = g_ref[...].astype(F32)
    gated = (a_ref[...].astype(F32) * (g * _sigmoid(g))).astype(BF16)
    k0 = 0
    y = x_ref[...]
    for r in plain:
        kw = r.shape[1]
        y = y + _dot(r[...], w_ref[k0:k0 + kw, :])
        k0 += kw
    y = y + _dot(gated, w_ref[k0:k0 + gated.shape[1], :])
    if final_norm:
        fn_ref = refs[-2]
        y = y * lax.rsqrt(jnp.mean(y * y, axis=-1, keepdims=True) + NORM_EPS) * fn_ref[...]
    o_ref[...] = y


def _out_proj(plain, gated, gate_src, gate_col, x2d, w, final_gain=None, *, tm=1024):
    t, d = x2d.shape
    kg = gated.shape[1]
    in_specs = [pl.BlockSpec((tm, a.shape[1]), lambda i: (i, 0)) for a in plain]
    in_specs += [pl.BlockSpec((tm, kg), lambda i: (i, 0)),
                 pl.BlockSpec((tm, kg), lambda i: (i, gate_col // kg)),
                 pl.BlockSpec((tm, d), lambda i: (i, 0)),
                 pl.BlockSpec(w.shape, lambda i: (0, 0))]
    args = list(plain) + [gated, gate_src, x2d, w]
    if final_gain is not None:
        in_specs.append(pl.BlockSpec((1, d), lambda i: (0, 0)))
        args.append(final_gain.reshape(1, d).astype(F32))
    return pl.pallas_call(
        functools.partial(_out_proj_body, n_plain=len(plain), final_norm=final_gain is not None),
        grid=(t // tm,),
        in_specs=in_specs,
        out_specs=pl.BlockSpec((tm, d), lambda i: (i, 0)),
        out_shape=jax.ShapeDtypeStruct((t, d), F32),
        compiler_params=pltpu.CompilerParams(
            dimension_semantics=("parallel",), vmem_limit_bytes=VMEM_LIMIT),
        name="out_proj_final" if final_gain is not None else "out_proj",
    )(*args)


DIL_PHASES = 8


def _dil_tiles(seq):
    ph, qb = DIL_PHASES, DIL_QBLOCK
    per = seq // ph
    tiles = []
    rows = qb // ph
    for n in range(seq // qb):
        qs = [(p * per + rows * n, rows) for p in range(ph)]
        if n == 0:
            ks, mask = [(p * per, rows) for p in range(ph)], "a_first"
        else:
            ks, mask = [(p * per + rows * (n - 1), 2 * rows) for p in range(ph)], "a"
        tiles.append((qs, ks, mask, "init"))
    seg = ph // 4
    rows = qb // seg
    for p4 in range(4):
        for n in range(seq // 4 // qb):
            qs = [((p4 + 4 * j) * per + rows * n, rows) for j in range(seg)]
            if n == 0:
                ks, mask = [((p4 + 4 * j) * per, rows) for j in range(seg)], "b_first"
            else:
                ks, mask = [((p4 + 4 * j) * per + rows * (n - 1), 2 * rows) for j in range(seg)], "b"
            tiles.append((qs, ks, mask, "merge"))
    for p in range(ph):
        for h in range(per // qb):
            qs = [(p * per + qb * h, qb)]
            ks = [(p * per, qb * (h + 1))]
            tiles.append((qs, ks, "c%d" % h, "final"))
    return tiles


def _dil_masks(seq):
    import numpy as np
    ph, qb = DIL_PHASES, DIL_QBLOCK
    per = seq // ph

    def pos(n_rows, seg_rows, step):
        i = np.arange(n_rows)
        return step * (i % seg_rows) + i // seg_rows

    def band(qpos, kpos, span):
        dist = qpos[:, None] - kpos[None, :]
        return np.where((dist >= 0) & (dist <= span), 0.0, NEG_BIG).astype(np.float32)

    masks = {}
    for name, seg in (("a", ph), ("b", ph // 4)):
        rows = qb // seg
        masks[name] = band(qb + pos(qb, rows, seg), pos(2 * qb, 2 * rows, seg), qb)
        masks[name + "_first"] = band(pos(qb, rows, seg), pos(qb, rows, seg), qb)
    for h in range(per // qb):
        lq = qb * h + np.arange(qb)
        lk = np.arange(qb * (h + 1))
        same = (lq[:, None] - lk[None, :]) % 2 == 0
        masks["c%d" % h] = np.where(same & (lk[None, :] <= lq[:, None]), 0.0, NEG_BIG).astype(np.float32)
    return masks


def _dil_body(*refs, seq, mask_names, group_size):
    q_ref, k_ref, v_ref = refs[:3]
    mask_refs = dict(zip(mask_names, refs[3:3 + len(mask_names)]))
    o_ref, onat_ref = refs[3 + len(mask_names):][:2]
    state_a, state_b = refs[-6:-3], refs[-3:]
    state_in = {"merge": state_a, "final": state_b}
    state_out = {"init": state_a, "merge": state_b}
    qb = DIL_QBLOCK
    d = DIL_HEAD_DIM

    def rows_of(ref, slices):
        parts = [ref[s:s + n, :] for s, n in slices]
        return parts[0] if len(parts) == 1 else jnp.concatenate(parts, axis=0)

    def put_rows(ref, slices, val):
        r0 = 0
        for s, n in slices:
            ref[s:s + n, :] = val[r0:r0 + n, :]
            r0 += n

    def cat(parts):
        return parts[0] if len(parts) == 1 else jnp.concatenate(parts, axis=0)

    def scores(group):
        mask = mask_refs[group[0][2]][...]
        return cat([_nt_dot(rows_of(q_ref, qs), rows_of(k_ref, ks)) + mask for qs, ks, _, _ in group])

    def probs(group, s):
        stage = group[0][3]
        m_new = jnp.broadcast_to(jnp.max(s, axis=-1, keepdims=True), (s.shape[0], d))
        if stage == "init":
            m_old = None
        else:
            m_ref = state_in[stage][1]
            m_old = cat([rows_of(m_ref, qs) for qs, _, _, _ in group])
            m_new = jnp.maximum(m_old, m_new)
        p = jnp.concatenate([jnp.exp2(s[:, c0:c0 + d] - m_new) for c0 in range(0, s.shape[1], d)], axis=1)
        return p.astype(BF16), m_new, m_old

    def accumulate(group, p, m_new, m_old):
        stage = group[0][3]
        pvs = []
        for g, (_, ks, _, _) in enumerate(group):
            vb = rows_of(v_ref, ks)
            pvs.append(_dot(p[g * qb:(g + 1) * qb, :], jnp.concatenate([vb, jnp.ones_like(vb)], axis=1)))
        pv = cat(pvs)
        acc, l = pv[:, :d], pv[:, d:]
        m_b = m_new
        if stage != "init":
            alpha = jnp.exp2(m_old - m_b)
            acc_ref, _, l_ref = state_in[stage]
            acc = alpha * cat([rows_of(acc_ref, qs) for qs, _, _, _ in group]) + acc
            l = alpha * cat([rows_of(l_ref, qs) for qs, _, _, _ in group]) + l
        if stage == "final":
            out = acc / l
        for g, (qs, _, _, _) in enumerate(group):
            rs = slice(g * qb, (g + 1) * qb)
            if stage == "final":
                (s0, _), = qs
                p_idx, l0 = divmod(s0, seq // DIL_PHASES)
                onat_ref[pl.ds(DIL_PHASES * l0 + p_idx, qb, stride=DIL_PHASES), :] = out[rs, :]
            else:
                acc_out, m_out, l_out = state_out[stage]
                put_rows(acc_out, qs, acc[rs, :])
                put_rows(l_out, qs, l[rs, :])
                put_rows(m_out, qs, m_b[rs, :])

    groups = []
    for tile in _dil_tiles(seq):
        if groups and len(groups[-1]) < group_size and groups[-1][0][2:] == tile[2:]:
            groups[-1].append(tile)
        else:
            groups.append([tile])

    def owned(group):
        return {r for qs, _, _, _ in group for s0, n in qs for r in range(s0, s0 + n)}

    for ga, gb in zip(groups, groups[1:]):
        assert not owned(ga) & owned(gb)
    scored, ready = None, None
    for group in groups + [None, None]:
        s_new = None if group is None else (group, scores(group))
        p_new = None if scored is None else (scored[0],) + probs(*scored)
        if ready is not None:
            accumulate(*ready)
        scored, ready = s_new, p_new
    o_ref[...] = onat_ref[...].astype(o_ref.dtype)


def _dilated(proj, bsz, seq):
    t = bsz * seq
    d = DIL_HEAD_DIM
    masks = _dil_masks(seq)
    names = tuple(sorted(masks))

    def head_block(base):
        return pl.BlockSpec((seq, d), lambda b, h: (b, base // d + h))

    return pl.pallas_call(
        functools.partial(_dil_body, seq=seq, mask_names=names, group_size=1),
        grid=(bsz, DIL_HEADS),
        in_specs=[head_block(0), head_block(DIL_INNER), head_block(2 * DIL_INNER)]
        + [pl.BlockSpec(masks[n].shape, lambda b, h: (0, 0)) for n in names],
        out_specs=pl.BlockSpec((seq, d), lambda b, h: (b, h)),
        out_shape=jax.ShapeDtypeStruct((t, DIL_INNER), BF16),
        scratch_shapes=[pltpu.VMEM((seq, d), F32)] * 7,
        compiler_params=pltpu.CompilerParams(
            dimension_semantics=("parallel", "parallel"), vmem_limit_bytes=VMEM_LIMIT),
        name="dilated_attn",
    )(proj, proj, proj, *[jnp.asarray(masks[n]) for n in names])


def kernel(x, even_norm, even_w_in, ssd_conv_w, ssd_conv_b, ssd_dt_bias, ssd_a_log, ssd_d, ssd_norm,
           even_w_out, odd_norm, odd_w_in, odd_w_out, final_norm):
    bsz, seq, d = x.shape
    t = bsz * seq
    x2d = x.reshape(t, d)

    w = even_w_in[0]
    z_end = SSD_INNER
    xbc_end = z_end + SSD_CONV_DIM
    dt_end = xbc_end + SSD_HEADS
    q_end = dt_end + MOBA_INNER
    v_end = q_end + 2 * MOBA_INNER
    x_end = z_end + SSD_INNER
    moba_scale = MOBA_HEAD_DIM ** -0.5 * math.log2(math.e)
    g0 = even_norm[0].astype(F32)[None, :]
    w_t = w.T

    def rows(a, b, scale=1.0):
        return (w_t[a:b] * (g0 * scale)).astype(BF16)

    w_main = jnp.concatenate([rows(0, x_end), rows(v_end, w.shape[1]), rows(dt_end, q_end, moba_scale),
                              rows(q_end, v_end), rows(x_end, xbc_end)], axis=0)
    w_dt = jnp.pad(rows(xbc_end, dt_end), ((0, LANES - SSD_HEADS), (0, 0)))
    proj0, dt_raw = _norm_proj_t(x2d, w_main, w_dt, tm=seq // 2, tn=P0_N // 2)
    y_ssd = _ssd(proj0, dt_raw, ssd_conv_w[0], ssd_conv_b[0], ssd_dt_bias[0], ssd_a_log[0],
                 ssd_d[0], ssd_norm[0], bsz, seq)
    o_moba = _moba(proj0, bsz, seq)
    x1 = _out_proj([y_ssd], o_moba, proj0, P0_G, x2d, even_w_out[0].astype(BF16))

    w1 = odd_w_in[0]
    dil_scale = DIL_HEAD_DIM ** -0.5 * math.log2(math.e)
    col_scale1 = jnp.ones((w1.shape[1],), F32).at[:DIL_INNER].set(dil_scale)
    w1_b = (w1 * col_scale1[None, :] * odd_norm[0].astype(F32)[:, None]).astype(BF16)
    proj1 = _norm_proj_phased(x1, w1_b, tm=seq, tn=1024, phases=DIL_PHASES, phase_cols=3 * DIL_INNER)
    o_dil = _dilated(proj1, bsz, seq)
    out = _out_proj([], o_dil, proj1, 3 * DIL_INNER, x1, odd_w_out[0].astype(BF16), final_norm)
    return out.reshape(bsz, seq, d)
```

```python
import functools
import math

import jax
import jax.numpy as jnp
from jax import lax
from jax.experimental import pallas as pl
from jax.experimental.pallas import tpu as pltpu

F32 = jnp.float32
BF16 = jnp.bfloat16

NORM_EPS = 1e-5
D_MODEL = 1024

SSD_HEADS = 16
SSD_HEAD_DIM = 64
SSD_INNER = SSD_HEADS * SSD_HEAD_DIM
SSD_GROUPS = 2
SSD_STATE = 128
SSD_CONV = 4
SSD_CHUNK = 128
SSD_BC = 2 * SSD_GROUPS * SSD_STATE
SSD_CONV_DIM = SSD_INNER + SSD_BC

MOBA_HEADS = 8
MOBA_HEAD_DIM = 128
MOBA_INNER = MOBA_HEADS * MOBA_HEAD_DIM
MOBA_BLOCK = 256
MOBA_TOPK = 3

DIL_HEADS = 16
DIL_HEAD_DIM = 128
DIL_INNER = DIL_HEADS * DIL_HEAD_DIM
DIL_QBLOCK = 128
DIL_PATTERNS = ((128, 1), (512, 4), (2048, 16))

LANES = 128
NEG_BIG = -1e30
VMEM_LIMIT = 56 * 1024 * 1024

P0_Z = 0
P0_X = P0_Z + SSD_INNER
P0_G = P0_X + SSD_INNER
P0_Q = P0_G + MOBA_INNER
P0_K = P0_Q + MOBA_INNER
P0_V = P0_K + MOBA_INNER
P0_BC = P0_V + MOBA_INNER
P0_N = P0_BC + SSD_BC


def _nt_dot(a, b):
    return lax.dot_general(a, b, (((1,), (1,)), ((), ())), preferred_element_type=F32)


def _dot(a, b):
    return jnp.dot(a, b, preferred_element_type=F32)


def _sigmoid(x):
    return 1.0 / (1.0 + jnp.exp2(x * -math.log2(math.e)))


def _norm_proj_body(*refs, kind, row_chunk, phases, n_phase_tiles):
    if kind == "aux":
        x_ref, w_ref, waux_ref, o_ref, oaux_ref, hn_ref = refs
    else:
        x_ref, w_ref, o_ref, hn_ref, hnp_ref, hnf_ref = refs
    j = pl.program_id(1)
    tm = x_ref.shape[0]

    @pl.when(j == 0)
    def _():
        def norm_rows(c, carry):
            r0 = pl.multiple_of(c * row_chunk, row_chunk)
            x = x_ref[pl.ds(r0, row_chunk), :]
            ms = jnp.mean(x * x, axis=-1, keepdims=True)
            hn = x * lax.rsqrt(ms + NORM_EPS)
            hn_ref[pl.ds(r0, row_chunk), :] = hn.astype(BF16)
            if phases:
                for cc in range(x.shape[1] // LANES):
                    hnf_ref[cc, pl.ds(r0, row_chunk), :] = hn[:, cc * LANES:(cc + 1) * LANES]
            return carry

        lax.fori_loop(0, tm // row_chunk, norm_rows, 0)
        if phases:
            per = tm // phases
            for p in range(phases):
                for cc in range(hnf_ref.shape[0]):
                    hnp_ref[p * per:(p + 1) * per, cc * LANES:(cc + 1) * LANES] = (
                        hnf_ref[cc, pl.ds(p, per, stride=phases), :].astype(BF16))
        if kind == "aux":
            oaux_ref[...] = _nt_dot(hn_ref[...], waux_ref[...])

    def project(src_ref):
        sub = min(tm, 512)
        for r0 in range(0, tm, sub):
            lhs = src_ref[r0:r0 + sub, :]
            prod = _nt_dot(lhs, w_ref[...]) if kind == "aux" else _dot(lhs, w_ref[...])
            o_ref[r0:r0 + sub, :] = prod.astype(o_ref.dtype)

    if phases:
        @pl.when(j < n_phase_tiles)
        def _():
            project(hnp_ref)

        @pl.when(j >= n_phase_tiles)
        def _():
            project(hn_ref)
    else:
        project(hn_ref)


def _norm_proj_t(x2d, w_t, w_aux_t, *, tm, tn):
    t, d = x2d.shape
    n = w_t.shape[0]
    return pl.pallas_call(
        functools.partial(_norm_proj_body, kind="aux", row_chunk=256, phases=0, n_phase_tiles=0),
        grid=(t // tm, n // tn),
        in_specs=[pl.BlockSpec((tm, d), lambda i, j: (i, 0)),
                  pl.BlockSpec((tn, d), lambda i, j: (j, 0)),
                  pl.BlockSpec((LANES, d), lambda i, j: (0, 0))],
        out_specs=[pl.BlockSpec((tm, tn), lambda i, j: (i, j)),
                   pl.BlockSpec((tm, LANES), lambda i, j: (i, 0))],
        out_shape=[jax.ShapeDtypeStruct((t, n), BF16), jax.ShapeDtypeStruct((t, LANES), F32)],
        scratch_shapes=[pltpu.VMEM((tm, d), BF16)],
        compiler_params=pltpu.CompilerParams(
            dimension_semantics=("parallel", "arbitrary"), vmem_limit_bytes=VMEM_LIMIT),
        name="norm_proj_aux",
    )(x2d, w_t, w_aux_t)


def _norm_proj_phased(x2d, w, *, tm, tn, phases, phase_cols):
    t, d = x2d.shape
    n = w.shape[1]
    return pl.pallas_call(
        functools.partial(_norm_proj_body, kind="phased", row_chunk=256, phases=phases,
                          n_phase_tiles=phase_cols // tn),
        grid=(t // tm, n // tn),
        in_specs=[pl.BlockSpec((tm, d), lambda i, j: (i, 0)),
                  pl.BlockSpec((d, tn), lambda i, j: (0, j))],
        out_specs=pl.BlockSpec((tm, tn), lambda i, j: (i, j)),
        out_shape=jax.ShapeDtypeStruct((t, n), BF16),
        scratch_shapes=[pltpu.VMEM((tm, d), BF16), pltpu.VMEM((tm, d), BF16),
                        pltpu.VMEM((d // LANES, tm, LANES), F32)],
        compiler_params=pltpu.CompilerParams(
            dimension_semantics=("parallel", "arbitrary"), vmem_limit_bytes=VMEM_LIMIT),
        name="norm_proj",
    )(x2d, w)


def _split3(a):
    hi = a.astype(BF16)
    r = a - hi.astype(F32)
    mid = r.astype(BF16)
    lo = (r - mid.astype(F32)).astype(BF16)
    return hi, mid, lo


def _expand_heads(a, e):
    hi, mid, lo = _split3(a)
    return _dot(hi, e) + _dot(mid, e) + _dot(lo, e)


def _ssd_steps(z_ref, xs_ref, bc_ref, dt_ref, cw_ref, cb_ref, dtb_ref, alog_ref, dskip_ref, nrm_ref,
              e_ref, tri_ref, y_ref, state_ref, ubuf_ref):
    c = pl.program_id(1)
    L = SSD_CHUNK
    tail = 8

    @pl.when(c == 0)
    def _():
        state_ref[...] = jnp.zeros_like(state_ref)
        ubuf_ref[0:tail, :] = jnp.zeros((tail, SSD_CONV_DIM), F32)

    row = lax.broadcasted_iota(jnp.int32, (L, L), 0)
    col = lax.broadcasted_iota(jnp.int32, (L, L), 1)
    causal = col <= row
    lane = lax.broadcasted_iota(jnp.int32, (L, LANES), 1)
    low_half = lane < SSD_HEAD_DIM
    heads_per_group = SSD_HEADS // SSD_GROUPS
    gw = heads_per_group * SSD_HEAD_DIM

    def chunk(r):
        acts = []
        cw = cw_ref[...]
        for c0 in range(0, SSD_CONV_DIM, SSD_BC):
            cols = slice(c0, c0 + SSD_BC)
            src = xs_ref[r, cols] if c0 < SSD_INNER else bc_ref[r, :]
            u = src.astype(F32)
            ubuf_ref[tail:tail + L, cols] = u
            acc = cb_ref[:, cols] + cw[SSD_CONV - 1:SSD_CONV, cols] * u
            for k in range(SSD_CONV - 1):
                acc = acc + cw[k:k + 1, cols] * ubuf_ref[pl.ds(tail - (SSD_CONV - 1) + k, L), cols]
            ubuf_ref[0:tail, cols] = u[L - tail:L, :]
            acts.append(acc * _sigmoid(acc))
            yield
        act = jnp.concatenate(acts, axis=1)

        xs = act[:, :SSD_INNER]

        dtr = dt_ref[r, :] + dtb_ref[...]
        dt = jnp.maximum(dtr, 0.0) + jnp.log1p(jnp.exp(-jnp.abs(dtr)))
        a = -jnp.exp(alog_ref[...]) * math.log2(math.e)
        ac = dt * a
        tri = tri_ref[...]
        hi, mid, lo = _split3(ac)
        a_cum = _dot(tri, hi) + _dot(tri, mid) + _dot(tri, lo)
        a_last = a_cum[L - 1:L, :]
        dec_states = jnp.exp2(a_last - a_cum)
        exp_acum = jnp.exp2(a_cum)
        e = e_ref[...]
        yield
        dt_e = _expand_heads(dt, e)
        dec_e = _expand_heads(dec_states, e)
        yield
        ea_e = _expand_heads(exp_acum, e)
        a_cum_t = a_cum.T

        xc = xs * dt_e
        xc_b = xc.astype(BF16)
        xdec_b = (xc * dec_e).astype(BF16)
        yield

        for g in range(SSD_GROUPS):
            b_g = act[:, SSD_INNER + g * SSD_STATE:SSD_INNER + (g + 1) * SSD_STATE]
            c_g = act[:, SSD_INNER + (SSD_GROUPS + g) * SSD_STATE:
                      SSD_INNER + (SSD_GROUPS + g + 1) * SSD_STATE]
            b_gb = b_g.astype(BF16)
            c_gb = c_g.astype(BF16)
            scores = _nt_dot(c_gb, b_gb)
            gs = slice(g * gw, (g + 1) * gw)

            prev = state_ref[:, gs]
            y_off = _dot(c_gb, prev.astype(BF16))
            st_new = _dot(b_g.T.astype(BF16), xdec_b[:, gs])
            state_ref[:, gs] = prev * ea_e[L - 1:L, gs] + st_new
            yield

            pieces = []
            for pair in range(heads_per_group // 2):
                ms = []
                for hh in range(2):
                    h = g * heads_per_group + 2 * pair + hh
                    seg = a_cum[:, h:h + 1] - a_cum_t[h:h + 1, :]
                    lmat = jnp.exp2(jnp.where(causal, seg, NEG_BIG))
                    ms.append((scores * lmat).astype(BF16))
                m_pair = jnp.concatenate(ms, axis=1)
                cs = slice(g * gw + pair * LANES, g * gw + (pair + 1) * LANES)
                x_pair = xc_b[:, cs]
                zero = jnp.zeros_like(x_pair)
                rhs = jnp.concatenate([jnp.where(low_half, x_pair, zero),
                                       jnp.where(low_half, zero, x_pair)], axis=0)
                pieces.append(_dot(m_pair, rhs))
                yield
            y_diag = jnp.concatenate(pieces, axis=1)

            y = y_diag + y_off * ea_e[:, gs] + xs[:, gs] * dskip_ref[:, gs]
            zg = z_ref[r, gs].astype(F32)
            ug = y * (zg * _sigmoid(zg))
            ug = ug * lax.rsqrt(jnp.mean(ug * ug, axis=-1, keepdims=True) + NORM_EPS)
            y_ref[r, gs] = (ug * nrm_ref[:, gs]).astype(y_ref.dtype)
            yield

    for ci in range(xs_ref.shape[0] // L):
        yield from chunk(slice(ci * L, (ci + 1) * L))


def _ssd_operands(proj, dt_raw, conv_w, conv_b, dt_bias, a_log, d_skip, ssd_norm, seq, R):
    L = SSD_CHUNK
    nc = seq // R

    def pad_heads(v):
        return jnp.pad(v.astype(F32), (0, LANES - SSD_HEADS)).reshape(1, LANES)

    head_of_chan = jnp.arange(SSD_INNER, dtype=jnp.int32) // SSD_HEAD_DIM
    expand = (jnp.arange(LANES, dtype=jnp.int32)[:, None] == head_of_chan[None, :]).astype(BF16)
    tri = (jnp.arange(L)[:, None] >= jnp.arange(L)[None, :]).astype(BF16)
    d_chan = jnp.repeat(d_skip.astype(F32), SSD_HEAD_DIM).reshape(1, SSD_INNER)

    def rows(b, c):
        return b * nc + c

    const = lambda b, c: (0, 0)
    in_specs = [
            pl.BlockSpec((R, SSD_INNER), lambda b, c: (rows(b, c), P0_Z // SSD_INNER)),
            pl.BlockSpec((R, SSD_INNER), lambda b, c: (rows(b, c), P0_X // SSD_INNER)),
            pl.BlockSpec((R, SSD_BC), lambda b, c: (rows(b, c), P0_BC // SSD_BC)),
            pl.BlockSpec((R, LANES), lambda b, c: (rows(b, c), 0)),
            pl.BlockSpec((SSD_CONV, SSD_CONV_DIM), const),
            pl.BlockSpec((1, SSD_CONV_DIM), const),
            pl.BlockSpec((1, LANES), const),
            pl.BlockSpec((1, LANES), const),
            pl.BlockSpec((1, SSD_INNER), const),
            pl.BlockSpec((1, SSD_INNER), const),
            pl.BlockSpec((LANES, SSD_INNER), const),
            pl.BlockSpec((L, L), const),
    ]
    args = (proj, proj, proj, dt_raw, conv_w.astype(F32), conv_b.reshape(1, -1).astype(F32),
            pad_heads(dt_bias), pad_heads(a_log), d_chan, ssd_norm.reshape(1, -1).astype(F32), expand, tri)
    return in_specs, args


def _moba_steps(q_ref, k_ref, v_ref, avg_ref, hot_ref, eye_ref, o_ref, *, nb, q_tile):
    blk = MOBA_BLOCK
    n_dense = MOBA_TOPK + 1

    kmean = _dot(avg_ref[...], k_ref[...]).astype(BF16)
    jrow = lax.broadcasted_iota(jnp.int32, (nb, blk), 0)
    row = lax.broadcasted_iota(jnp.int32, (q_tile, blk), 0)
    col = lax.broadcasted_iota(jnp.int32, (q_tile, blk), 1)

    def query_block(i):
        q_i = q_ref[i * blk:(i + 1) * blk, :]
        if i < n_dense:
            return q_i, None
        gate_t = _nt_dot(kmean, q_i)[0:nb, :]
        valid = jrow < i
        gm = jnp.where(valid, gate_t, -jnp.inf)
        rank = jnp.zeros((nb, blk), F32)
        for r in range(1, nb):
            gr = pltpu.roll(gm, r, axis=0)
            lower = ((jrow - r) & (nb - 1)) < jrow
            rank = rank + jnp.where(gr > gm, 1.0, 0.0) + jnp.where((gr == gm) & lower, 1.0, 0.0)
        keep = valid & (rank < float(MOBA_TOPK))
        bias_t = jnp.where(keep, 0.0, NEG_BIG)
        bias_t = jnp.concatenate([bias_t, jnp.zeros((LANES - nb, blk), F32)], axis=0).astype(BF16)
        bias = _nt_dot(eye_ref[...], bias_t).astype(BF16)
        return q_i, jnp.concatenate([q_i, bias], axis=1)

    def scores(i, h, q_i, q_aug):
        rs = slice(h * q_tile, (h + 1) * q_tile)
        s_own = _nt_dot(q_i[rs, :], k_ref[i * blk:(i + 1) * blk, :])
        s_own = jnp.where(col <= row + h * q_tile, s_own, NEG_BIG)
        if i == 0:
            return s_own, None
        if q_aug is None:
            return s_own, _nt_dot(q_i[rs, :], k_ref[0:i * blk, :])
        k_past = jnp.concatenate([k_ref[0:i * blk, :], hot_ref[0:i * blk, :]], axis=1)
        return s_own, _nt_dot(q_aug[rs, :], k_past)

    def finish(i, h, s_own, s_past):
        m = jnp.max(s_own, axis=-1, keepdims=True)
        if s_past is None:
            p = jnp.exp2(s_own - m)
        else:
            m = jnp.maximum(m, jnp.max(s_past, axis=-1, keepdims=True))
            p = jnp.concatenate([jnp.exp2(s_past - m), jnp.exp2(s_own - m)], axis=1)
        vb = v_ref[0:(i + 1) * blk, :]
        pv = _dot(p.astype(BF16), jnp.concatenate([vb, jnp.ones_like(vb)], axis=1))
        d = vb.shape[1]
        o_ref[i * blk + h * q_tile:i * blk + (h + 1) * q_tile, :] = (pv[:, :d] / pv[:, d:]).astype(o_ref.dtype)

    tiles = [(i, h) for i in range(nb) for h in range(blk // q_tile)]
    qcache = {}
    pending = None
    for i, h in tiles:
        if i not in qcache:
            qcache = {i: query_block(i)}
        s_next = scores(i, h, *qcache[i])
        yield
        if pending is not None:
            finish(*pending)
        pending = (i, h) + s_next
        yield
    finish(*pending)
    yield


def _ssd_moba_body(*refs, nb, q_tile, n_ssd_in, n_moba_in):
    ssd_in = refs[:n_ssd_in]
    moba_in = refs[n_ssd_in:n_ssd_in + n_moba_in]
    y_ref, o_ref, state_ref, ubuf_ref = refs[n_ssd_in + n_moba_in:]
    streams = [(_ssd_steps(*ssd_in, y_ref, state_ref, ubuf_ref), 2),
               (_moba_steps(*moba_in, o_ref, nb=nb, q_tile=q_tile), 1)]
    while streams:
        for entry in list(streams):
            stream, per_round = entry
            if any(next(stream, "done") == "done" for _ in range(per_round)):
                streams.remove(entry)


def _moba_operands(proj, seq):
    blk = MOBA_BLOCK
    nb = seq // blk
    d = MOBA_HEAD_DIM
    blk_of_key = jnp.arange(seq, dtype=jnp.int32) // blk
    avg = (jnp.arange(2 * nb, dtype=jnp.int32)[:, None] == blk_of_key[None, :]).astype(BF16) / blk
    hot = (blk_of_key[:, None] == jnp.arange(LANES, dtype=jnp.int32)[None, :]).astype(BF16)
    eye = jnp.eye(blk, dtype=BF16)

    def head_block(base):
        return pl.BlockSpec((seq, d), lambda b, h: (b, base // d + h))

    const = lambda b, h: (0, 0)
    in_specs = [head_block(P0_Q), head_block(P0_K), head_block(P0_V),
                pl.BlockSpec((2 * nb, seq), const), pl.BlockSpec((seq, LANES), const),
                pl.BlockSpec((blk, blk), const)]
    return in_specs, (proj, proj, proj, avg, hot, eye)


def _ssd_moba(proj, dt_raw, conv_w, conv_b, dt_bias, a_log, d_skip, ssd_norm, bsz, seq):
    t = bsz * seq
    R = seq // MOBA_HEADS
    ssd_specs, ssd_args = _ssd_operands(proj, dt_raw, conv_w, conv_b, dt_bias, a_log, d_skip, ssd_norm,
                                        seq, R)
    moba_specs, moba_args = _moba_operands(proj, seq)
    return pl.pallas_call(
        functools.partial(_ssd_moba_body, nb=seq // MOBA_BLOCK, q_tile=256,
                          n_ssd_in=len(ssd_specs), n_moba_in=len(moba_specs)),
        grid=(bsz, MOBA_HEADS),
        in_specs=ssd_specs + moba_specs,
        out_specs=[pl.BlockSpec((R, SSD_INNER), lambda b, h: (b * MOBA_HEADS + h, 0)),
                   pl.BlockSpec((seq, MOBA_HEAD_DIM), lambda b, h: (b, h))],
        out_shape=[jax.ShapeDtypeStruct((t, SSD_INNER), BF16),
                   jax.ShapeDtypeStruct((t, MOBA_INNER), BF16)],
        scratch_shapes=[pltpu.VMEM((SSD_STATE, SSD_INNER), F32),
                        pltpu.VMEM((8 + SSD_CHUNK, SSD_CONV_DIM), F32)],
        compiler_params=pltpu.CompilerParams(
            dimension_semantics=("parallel", "arbitrary"), vmem_limit_bytes=VMEM_LIMIT),
        name="ssd_moba",
    )(*ssd_args, *moba_args)


def _out_proj_body(*refs, n_plain, final_norm):
    plain = refs[:n_plain]
    a_ref, g_ref, x_ref, w_ref = refs[n_plain:n_plain + 4]
    o_ref = refs[-1]
    g = g_ref[...].astype(F32)
    gated = (a_ref[...].astype(F32) * (g * _sigmoid(g))).astype(BF16)
    k0 = 0
    y = x_ref[...]
    for r in plain:
        kw = r.shape[1]
        y = y + _dot(r[...], w_ref[k0:k0 + kw, :])
        k0 += kw
    y = y + _dot(gated, w_ref[k0:k0 + gated.shape[1], :])
    if final_norm:
        fn_ref = refs[-2]
        y = y * lax.rsqrt(jnp.mean(y * y, axis=-1, keepdims=True) + NORM_EPS) * fn_ref[...]
    o_ref[...] = y


def _out_proj(plain, gated, gate_src, gate_col, x2d, w, final_gain=None, *, tm=1024):
    t, d = x2d.shape
    kg = gated.shape[1]
    in_specs = [pl.BlockSpec((tm, a.shape[1]), lambda i: (i, 0)) for a in plain]
    in_specs += [pl.BlockSpec((tm, kg), lambda i: (i, 0)),
                 pl.BlockSpec((tm, kg), lambda i: (i, gate_col // kg)),
                 pl.BlockSpec((tm, d), lambda i: (i, 0)),
                 pl.BlockSpec(w.shape, lambda i: (0, 0))]
    args = list(plain) + [gated, gate_src, x2d, w]
    if final_gain is not None:
        in_specs.append(pl.BlockSpec((1, d), lambda i: (0, 0)))
        args.append(final_gain.reshape(1, d).astype(F32))
    return pl.pallas_call(
        functools.partial(_out_proj_body, n_plain=len(plain), final_norm=final_gain is not None),
        grid=(t // tm,),
        in_specs=in_specs,
        out_specs=pl.BlockSpec((tm, d), lambda i: (i, 0)),
        out_shape=jax.ShapeDtypeStruct((t, d), F32),
        compiler_params=pltpu.CompilerParams(
            dimension_semantics=("parallel",), vmem_limit_bytes=VMEM_LIMIT),
        name="out_proj_final" if final_gain is not None else "out_proj",
    )(*args)


DIL_PHASES = 8


def _dil_tiles(seq):
    ph, qb = DIL_PHASES, DIL_QBLOCK
    per = seq // ph
    tiles = []
    rows = qb // ph
    for n in range(seq // qb):
        qs = [(p * per + rows * n, rows) for p in range(ph)]
        if n == 0:
            ks, mask = [(p * per, rows) for p in range(ph)], "a_first"
        else:
            ks, mask = [(p * per + rows * (n - 1), 2 * rows) for p in range(ph)], "a"
        tiles.append((qs, ks, mask, "init"))
    seg = ph // 4
    rows = qb // seg
    for p4 in range(4):
        for n in range(seq // 4 // qb):
            qs = [((p4 + 4 * j) * per + rows * n, rows) for j in range(seg)]
            if n == 0:
                ks, mask = [((p4 + 4 * j) * per, rows) for j in range(seg)], "b_first"
            else:
                ks, mask = [((p4 + 4 * j) * per + rows * (n - 1), 2 * rows) for j in range(seg)], "b"
            tiles.append((qs, ks, mask, "merge"))
    for p in range(ph):
        for h in range(per // qb):
            qs = [(p * per + qb * h, qb)]
            ks = [(p * per, qb * (h + 1))]
            tiles.append((qs, ks, "c%d" % h, "final"))
    return tiles


def _dil_masks(seq):
    import numpy as np
    ph, qb = DIL_PHASES, DIL_QBLOCK
    per = seq // ph

    def pos(n_rows, seg_rows, step):
        i = np.arange(n_rows)
        return step * (i % seg_rows) + i // seg_rows

    def band(qpos, kpos, span):
        dist = qpos[:, None] - kpos[None, :]
        return np.where((dist >= 0) & (dist <= span), 0.0, NEG_BIG).astype(np.float32)

    masks = {}
    for name, seg in (("a", ph), ("b", ph // 4)):
        rows = qb // seg
        masks[name] = band(qb + pos(qb, rows, seg), pos(2 * qb, 2 * rows, seg), qb)
        masks[name + "_first"] = band(pos(qb, rows, seg), pos(qb, rows, seg), qb)
    for h in range(per // qb):
        lq = qb * h + np.arange(qb)
        lk = np.arange(qb * (h + 1))
        same = (lq[:, None] - lk[None, :]) % 2 == 0
        masks["c%d" % h] = np.where(same & (lk[None, :] <= lq[:, None]), 0.0, NEG_BIG).astype(np.float32)
    return masks


def _dil_body(*refs, seq, mask_names, group_size):
    q_ref, k_ref, v_ref = refs[:3]
    mask_refs = dict(zip(mask_names, refs[3:3 + len(mask_names)]))
    o_ref, onat_ref = refs[3 + len(mask_names):][:2]
    state_a, state_b = refs[-6:-3], refs[-3:]
    state_in = {"merge": state_a, "final": state_b}
    state_out = {"init": state_a, "merge": state_b}
    qb = DIL_QBLOCK
    d = DIL_HEAD_DIM

    def rows_of(ref, slices):
        parts = [ref[s:s + n, :] for s, n in slices]
        return parts[0] if len(parts) == 1 else jnp.concatenate(parts, axis=0)

    def put_rows(ref, slices, val):
        r0 = 0
        for s, n in slices:
            ref[s:s + n, :] = val[r0:r0 + n, :]
            r0 += n

    def cat(parts):
        return parts[0] if len(parts) == 1 else jnp.concatenate(parts, axis=0)

    def scores(group):
        mask = mask_refs[group[0][2]][...]
        return cat([_nt_dot(rows_of(q_ref, qs), rows_of(k_ref, ks)) + mask for qs, ks, _, _ in group])

    def probs(group, s):
        stage = group[0][3]
        m_new = jnp.broadcast_to(jnp.max(s, axis=-1, keepdims=True), (s.shape[0], d))
        if stage == "init":
            m_old = None
        else:
            m_ref = state_in[stage][1]
            m_old = cat([rows_of(m_ref, qs) for qs, _, _, _ in group])
            m_new = jnp.maximum(m_old, m_new)
        p = jnp.concatenate([jnp.exp2(s[:, c0:c0 + d] - m_new) for c0 in range(0, s.shape[1], d)], axis=1)
        return p.astype(BF16), m_new, m_old

    def accumulate(group, p, m_new, m_old):
        stage = group[0][3]
        pvs = []
        for g, (_, ks, _, _) in enumerate(group):
            vb = rows_of(v_ref, ks)
            pvs.append(_dot(p[g * qb:(g + 1) * qb, :], jnp.concatenate([vb, jnp.ones_like(vb)], axis=1)))
        pv = cat(pvs)
        acc, l = pv[:, :d], pv[:, d:]
        m_b = m_new
        if stage != "init":
            alpha = jnp.exp2(m_old - m_b)
            acc_ref, _, l_ref = state_in[stage]
            acc = alpha * cat([rows_of(acc_ref, qs) for qs, _, _, _ in group]) + acc
            l = alpha * cat([rows_of(l_ref, qs) for qs, _, _, _ in group]) + l
        if stage == "final":
            out = acc / l
        for g, (qs, _, _, _) in enumerate(group):
            rs = slice(g * qb, (g + 1) * qb)
            if stage == "final":
                (s0, _), = qs
                p_idx, l0 = divmod(s0, seq // DIL_PHASES)
                onat_ref[pl.ds(DIL_PHASES * l0 + p_idx, qb, stride=DIL_PHASES), :] = out[rs, :]
            else:
                acc_out, m_out, l_out = state_out[stage]
                put_rows(acc_out, qs, acc[rs, :])
                put_rows(l_out, qs, l[rs, :])
                put_rows(m_out, qs, m_b[rs, :])

    groups = []
    for tile in _dil_tiles(seq):
        if groups and len(groups[-1]) < group_size and groups[-1][0][2:] == tile[2:]:
            groups[-1].append(tile)
        else:
            groups.append([tile])

    def owned(group):
        return {r for qs, _, _, _ in group for s0, n in qs for r in range(s0, s0 + n)}

    for ga, gb in zip(groups, groups[1:]):
        assert not owned(ga) & owned(gb)
    scored, ready = None, None
    for group in groups + [None, None]:
        s_new = None if group is None else (group, scores(group))
        p_new = None if scored is None else (scored[0],) + probs(*scored)
        if ready is not None:
            accumulate(*ready)
        scored, ready = s_new, p_new
    o_ref[...] = onat_ref[...].astype(o_ref.dtype)


def _dilated(proj, bsz, seq):
    t = bsz * seq
    d = DIL_HEAD_DIM
    masks = _dil_masks(seq)
    names = tuple(sorted(masks))

    def head_block(base):
        return pl.BlockSpec((seq, d), lambda b, h: (b, base // d + h))

    return pl.pallas_call(
        functools.partial(_dil_body, seq=seq, mask_names=names, group_size=1),
        grid=(bsz, DIL_HEADS),
        in_specs=[head_block(0), head_block(DIL_INNER), head_block(2 * DIL_INNER)]
        + [pl.BlockSpec(masks[n].shape, lambda b, h: (0, 0)) for n in names],
        out_specs=pl.BlockSpec((seq, d), lambda b, h: (b, h)),
        out_shape=jax.ShapeDtypeStruct((t, DIL_INNER), BF16),
        scratch_shapes=[pltpu.VMEM((seq, d), F32)] * 7,
        compiler_params=pltpu.CompilerParams(
            dimension_semantics=("parallel", "parallel"), vmem_limit_bytes=VMEM_LIMIT),
        name="dilated_attn",
    )(proj, proj, proj, *[jnp.asarray(masks[n]) for n in names])


def kernel(x, even_norm, even_w_in, ssd_conv_w, ssd_conv_b, ssd_dt_bias, ssd_a_log, ssd_d, ssd_norm,
           even_w_out, odd_norm, odd_w_in, odd_w_out, final_norm):
    bsz, seq, d = x.shape
    t = bsz * seq
    x2d = x.reshape(t, d)

    w = even_w_in[0]
    z_end = SSD_INNER
    xbc_end = z_end + SSD_CONV_DIM
    dt_end = xbc_end + SSD_HEADS
    q_end = dt_end + MOBA_INNER
    v_end = q_end + 2 * MOBA_INNER
    x_end = z_end + SSD_INNER
    moba_scale = MOBA_HEAD_DIM ** -0.5 * math.log2(math.e)
    g0 = even_norm[0].astype(F32)[None, :]
    w_t = w.T

    def rows(a, b, scale=1.0):
        return (w_t[a:b] * (g0 * scale)).astype(BF16)

    w_main = jnp.concatenate([rows(0, x_end), rows(v_end, w.shape[1]), rows(dt_end, q_end, moba_scale),
                              rows(q_end, v_end), rows(x_end, xbc_end)], axis=0)
    w_dt = jnp.pad(rows(xbc_end, dt_end), ((0, LANES - SSD_HEADS), (0, 0)))
    proj0, dt_raw = _norm_proj_t(x2d, w_main, w_dt, tm=seq // 2, tn=P0_N // 2)
    y_ssd, o_moba = _ssd_moba(proj0, dt_raw, ssd_conv_w[0], ssd_conv_b[0], ssd_dt_bias[0], ssd_a_log[0],
                              ssd_d[0], ssd_norm[0], bsz, seq)
    x1 = _out_proj([y_ssd], o_moba, proj0, P0_G, x2d, even_w_out[0].astype(BF16))

    w1 = odd_w_in[0]
    dil_scale = DIL_HEAD_DIM ** -0.5 * math.log2(math.e)
    col_scale1 = jnp.ones((w1.shape[1],), F32).at[:DIL_INNER].set(dil_scale)
    w1_b = (w1 * col_scale1[None, :] * odd_norm[0].astype(F32)[:, None]).astype(BF16)
    proj1 = _norm_proj_phased(x1, w1_b, tm=seq, tn=1024, phases=DIL_PHASES, phase_cols=3 * DIL_INNER)
    o_dil = _dilated(proj1, bsz, seq)
    out = _out_proj([], o_dil, proj1, 3 * DIL_INNER, x1, odd_w_out[0].astype(BF16), final_norm)
    return out.reshape(bsz, seq, d)
```

```python
import functools
import math

import jax
import jax.numpy as jnp
from jax import lax
from jax.experimental import pallas as pl
from jax.experimental.pallas import tpu as pltpu

F32 = jnp.float32
BF16 = jnp.bfloat16

NORM_EPS = 1e-5
D_MODEL = 1024

SSD_HEADS = 16
SSD_HEAD_DIM = 64
SSD_INNER = SSD_HEADS * SSD_HEAD_DIM
SSD_GROUPS = 2
SSD_STATE = 128
SSD_CONV = 4
SSD_CHUNK = 128
SSD_BC = 2 * SSD_GROUPS * SSD_STATE
SSD_CONV_DIM = SSD_INNER + SSD_BC

MOBA_HEADS = 8
MOBA_HEAD_DIM = 128
MOBA_INNER = MOBA_HEADS * MOBA_HEAD_DIM
MOBA_BLOCK = 256
MOBA_TOPK = 3

DIL_HEADS = 16
DIL_HEAD_DIM = 128
DIL_INNER = DIL_HEADS * DIL_HEAD_DIM
DIL_QBLOCK = 128
DIL_PATTERNS = ((128, 1), (512, 4), (2048, 16))

LANES = 128
NEG_BIG = -1e30
VMEM_LIMIT = 56 * 1024 * 1024

P0_Z = 0
P0_X = P0_Z + SSD_INNER
P0_G = P0_X + SSD_INNER
P0_Q = P0_G + MOBA_INNER
P0_K = P0_Q + MOBA_INNER
P0_V = P0_K + MOBA_INNER
P0_BC = P0_V + MOBA_INNER
P0_N = P0_BC + SSD_BC


def _nt_dot(a, b):
    return lax.dot_general(a, b, (((1,), (1,)), ((), ())), preferred_element_type=F32)


def _dot(a, b):
    return jnp.dot(a, b, preferred_element_type=F32)


def _sigmoid(x):
    return 1.0 / (1.0 + jnp.exp2(x * -math.log2(math.e)))


def _norm_proj_body(*refs, kind, phases, n_token_tiles):
    if kind == "aux":
        x_ref, w_ref, waux_ref, o_ref, oaux_ref, hn_ref = refs
    else:
        x_ref, w_ref, o_ref, hn_ref, hnp_ref, hnf_ref = refs
    j = pl.program_id(1)
    tm = x_ref.shape[0]

    if kind == "aux":
        sub = min(tm, 512)

        def normed(r0):
            x = x_ref[r0:r0 + sub, :]
            ms = jnp.mean(x * x, axis=-1, keepdims=True)
            return (x * lax.rsqrt(ms + NORM_EPS)).astype(BF16)

        starts = list(range(0, tm, sub))
        nxt = normed(starts[0])
        for k, r0 in enumerate(starts):
            cur = nxt
            if k + 1 < len(starts):
                nxt = normed(starts[k + 1])
            hn_ref[r0:r0 + sub, :] = cur
            o_ref[r0:r0 + sub, :] = _nt_dot(cur, w_ref[...]).astype(o_ref.dtype)

        @pl.when(j == 0)
        def _():
            oaux_ref[...] = _nt_dot(hn_ref[...], waux_ref[...])

        return

    d = x_ref.shape[1]
    sub = min(tm, 512)
    starts = list(range(0, tm, sub))
    per = tm // phases
    assert n_token_tiles >= 2 and phases % len(starts) == 0

    def normed(r0):
        x = x_ref[r0:r0 + sub, :]
        ms = jnp.mean(x * x, axis=-1, keepdims=True)
        return x * lax.rsqrt(ms + NORM_EPS)

    def token_rows(side_work):
        nxt = normed(starts[0])
        for k, r0 in enumerate(starts):
            cur = nxt
            if k + 1 < len(starts):
                nxt = normed(starts[k + 1])
            side_work(k, r0, cur)
            o_ref[r0:r0 + sub, :] = _dot(cur.astype(BF16), w_ref[...]).astype(o_ref.dtype)

    def keep(k, r0, hn):
        hn_ref[r0:r0 + sub, :] = hn.astype(BF16)
        for cc in range(d // LANES):
            hnf_ref[cc, r0:r0 + sub, :] = hn[:, cc * LANES:(cc + 1) * LANES]

    def permute(first, k0):
        def side_work(k):
            lo = first + 2 * (k - k0)
            for p in range(max(lo, first), min(lo + 2, phases)):
                for cc in range(d // LANES):
                    hnp_ref[p * per:(p + 1) * per, cc * LANES:(cc + 1) * LANES] = (
                        hnf_ref[cc, pl.ds(p, per, stride=phases), :].astype(BF16))
        return side_work

    @pl.when(j == 0)
    def _():
        token_rows(keep)

    def stored_rows(src_ref, side_work=None):
        for k, r0 in enumerate(starts):
            if side_work is not None:
                side_work(k)
            o_ref[r0:r0 + sub, :] = _dot(src_ref[r0:r0 + sub, :], w_ref[...]).astype(o_ref.dtype)

    assert n_token_tiles == 2 and phases == 2 * len(starts)

    @pl.when(j == 1)
    def _():
        stored_rows(hn_ref)

    @pl.when(j == 2)
    def _():
        build = permute(0, 0)
        build(0)
        for k, r0 in enumerate(starts):
            if k + 1 < len(starts):
                build(k + 1)
            o_ref[r0:r0 + sub, :] = _dot(hnp_ref[r0:r0 + sub, :], w_ref[...]).astype(o_ref.dtype)

    @pl.when(j > 2)
    def _():
        stored_rows(hnp_ref)


def _norm_proj_t(x2d, w_t, w_aux_t, *, tm, tn):
    t, d = x2d.shape
    n = w_t.shape[0]
    return pl.pallas_call(
        functools.partial(_norm_proj_body, kind="aux", phases=0, n_token_tiles=0),
        grid=(t // tm, n // tn),
        in_specs=[pl.BlockSpec((tm, d), lambda i, j: (i, 0)),
                  pl.BlockSpec((tn, d), lambda i, j: (j, 0)),
                  pl.BlockSpec((LANES, d), lambda i, j: (0, 0))],
        out_specs=[pl.BlockSpec((tm, tn), lambda i, j: (i, j)),
                   pl.BlockSpec((tm, LANES), lambda i, j: (i, 0))],
        out_shape=[jax.ShapeDtypeStruct((t, n), BF16), jax.ShapeDtypeStruct((t, LANES), F32)],
        scratch_shapes=[pltpu.VMEM((tm, d), BF16)],
        compiler_params=pltpu.CompilerParams(
            dimension_semantics=("parallel", "arbitrary"), vmem_limit_bytes=VMEM_LIMIT),
        name="norm_proj_aux",
    )(x2d, w_t, w_aux_t)


def _norm_proj_phased(x2d, w, *, tm, tn, phases, token_cols):
    t, d = x2d.shape
    n = w.shape[1]
    return pl.pallas_call(
        functools.partial(_norm_proj_body, kind="phased", phases=phases, n_token_tiles=token_cols // tn),
        grid=(t // tm, n // tn),
        in_specs=[pl.BlockSpec((tm, d), lambda i, j: (i, 0)),
                  pl.BlockSpec((d, tn), lambda i, j: (0, j))],
        out_specs=pl.BlockSpec((tm, tn), lambda i, j: (i, j)),
        out_shape=jax.ShapeDtypeStruct((t, n), BF16),
        scratch_shapes=[pltpu.VMEM((tm, d), BF16), pltpu.VMEM((tm, d), BF16),
                        pltpu.VMEM((d // LANES, tm, LANES), F32)],
        compiler_params=pltpu.CompilerParams(
            dimension_semantics=("parallel", "arbitrary"), vmem_limit_bytes=VMEM_LIMIT),
        name="norm_proj",
    )(x2d, w)


def _split3(a):
    hi = a.astype(BF16)
    r = a - hi.astype(F32)
    mid = r.astype(BF16)
    lo = (r - mid.astype(F32)).astype(BF16)
    return hi, mid, lo


def _expand_heads(a, e):
    hi, mid, lo = _split3(a)
    return _dot(hi, e) + _dot(mid, e) + _dot(lo, e)


def _ssd_steps(z_ref, xs_ref, bc_ref, dt_ref, cw_ref, cb_ref, dtb_ref, alog_ref, dskip_ref, nrm_ref,
              e_ref, tri_ref, y_ref, state_ref, ubuf_ref):
    c = pl.program_id(1)
    L = SSD_CHUNK
    tail = 8

    @pl.when(c == 0)
    def _():
        state_ref[...] = jnp.zeros_like(state_ref)
        ubuf_ref[0:tail, :] = jnp.zeros((tail, SSD_CONV_DIM), F32)

    row = lax.broadcasted_iota(jnp.int32, (L, L), 0)
    col = lax.broadcasted_iota(jnp.int32, (L, L), 1)
    causal = col <= row
    lane = lax.broadcasted_iota(jnp.int32, (L, LANES), 1)
    low_half = lane < SSD_HEAD_DIM
    heads_per_group = SSD_HEADS // SSD_GROUPS
    gw = heads_per_group * SSD_HEAD_DIM

    def chunk(r):
        acts = []
        cw = cw_ref[...]
        for c0 in range(0, SSD_CONV_DIM, SSD_BC):
            cols = slice(c0, c0 + SSD_BC)
            src = xs_ref[r, cols] if c0 < SSD_INNER else bc_ref[r, :]
            u = src.astype(F32)
            ubuf_ref[tail:tail + L, cols] = u
            acc = cb_ref[:, cols] + cw[SSD_CONV - 1:SSD_CONV, cols] * u
            for k in range(SSD_CONV - 1):
                acc = acc + cw[k:k + 1, cols] * ubuf_ref[pl.ds(tail - (SSD_CONV - 1) + k, L), cols]
            ubuf_ref[0:tail, cols] = u[L - tail:L, :]
            acts.append(acc * _sigmoid(acc))
            yield
        act = jnp.concatenate(acts, axis=1)

        xs = act[:, :SSD_INNER]

        dtr = dt_ref[r, :] + dtb_ref[...]
        dt = jnp.maximum(dtr, 0.0) + jnp.log1p(jnp.exp(-jnp.abs(dtr)))
        a = -jnp.exp(alog_ref[...]) * math.log2(math.e)
        ac = dt * a
        tri = tri_ref[...]
        hi, mid, lo = _split3(ac)
        a_cum = _dot(tri, hi) + _dot(tri, mid) + _dot(tri, lo)
        a_last = a_cum[L - 1:L, :]
        dec_states = jnp.exp2(a_last - a_cum)
        exp_acum = jnp.exp2(a_cum)
        e = e_ref[...]
        yield
        dt_e = _expand_heads(dt, e)
        dec_e = _expand_heads(dec_states, e)
        yield
        ea_e = _expand_heads(exp_acum, e)
        a_cum_t = a_cum.T

        xc = xs * dt_e
        xc_b = xc.astype(BF16)
        xdec_b = (xc * dec_e).astype(BF16)
        yield

        for g in range(SSD_GROUPS):
            b_g = act[:, SSD_INNER + g * SSD_STATE:SSD_INNER + (g + 1) * SSD_STATE]
            c_g = act[:, SSD_INNER + (SSD_GROUPS + g) * SSD_STATE:
                      SSD_INNER + (SSD_GROUPS + g + 1) * SSD_STATE]
            b_gb = b_g.astype(BF16)
            c_gb = c_g.astype(BF16)
            scores = _nt_dot(c_gb, b_gb)
            gs = slice(g * gw, (g + 1) * gw)

            prev = state_ref[:, gs]
            y_off = _dot(c_gb, prev.astype(BF16))
            st_new = _dot(b_g.T.astype(BF16), xdec_b[:, gs])
            state_ref[:, gs] = prev * ea_e[L - 1:L, gs] + st_new
            yield

            pieces = []
            for pair in range(heads_per_group // 2):
                ms = []
                for hh in range(2):
                    h = g * heads_per_group + 2 * pair + hh
                    seg = a_cum[:, h:h + 1] - a_cum_t[h:h + 1, :]
                    lmat = jnp.exp2(jnp.where(causal, seg, NEG_BIG))
                    ms.append((scores * lmat).astype(BF16))
                m_pair = jnp.concatenate(ms, axis=1)
                cs = slice(g * gw + pair * LANES, g * gw + (pair + 1) * LANES)
                x_pair = xc_b[:, cs]
                zero = jnp.zeros_like(x_pair)
                rhs = jnp.concatenate([jnp.where(low_half, x_pair, zero),
                                       jnp.where(low_half, zero, x_pair)], axis=0)
                pieces.append(_dot(m_pair, rhs))
                yield
            y_diag = jnp.concatenate(pieces, axis=1)

            y = y_diag + y_off * ea_e[:, gs] + xs[:, gs] * dskip_ref[:, gs]
            zg = z_ref[r, gs].astype(F32)
            ug = y * (zg * _sigmoid(zg))
            ug = ug * lax.rsqrt(jnp.mean(ug * ug, axis=-1, keepdims=True) + NORM_EPS)
            y_ref[r, gs] = (ug * nrm_ref[:, gs]).astype(y_ref.dtype)
            yield

    for ci in range(xs_ref.shape[0] // L):
        yield from chunk(slice(ci * L, (ci + 1) * L))


def _ssd_operands(proj, dt_raw, conv_w, conv_b, dt_bias, a_log, d_skip, ssd_norm, seq, R):
    L = SSD_CHUNK
    nc = seq // R

    def pad_heads(v):
        return jnp.pad(v.astype(F32), (0, LANES - SSD_HEADS)).reshape(1, LANES)

    head_of_chan = jnp.arange(SSD_INNER, dtype=jnp.int32) // SSD_HEAD_DIM
    expand = (jnp.arange(LANES, dtype=jnp.int32)[:, None] == head_of_chan[None, :]).astype(BF16)
    tri = (jnp.arange(L)[:, None] >= jnp.arange(L)[None, :]).astype(BF16)
    d_chan = jnp.repeat(d_skip.astype(F32), SSD_HEAD_DIM).reshape(1, SSD_INNER)

    def rows(b, c):
        return b * nc + c

    const = lambda b, c: (0, 0)
    in_specs = [
            pl.BlockSpec((R, SSD_INNER), lambda b, c: (rows(b, c), P0_Z // SSD_INNER)),
            pl.BlockSpec((R, SSD_INNER), lambda b, c: (rows(b, c), P0_X // SSD_INNER)),
            pl.BlockSpec((R, SSD_BC), lambda b, c: (rows(b, c), P0_BC // SSD_BC)),
            pl.BlockSpec((R, LANES), lambda b, c: (rows(b, c), 0)),
            pl.BlockSpec((SSD_CONV, SSD_CONV_DIM), const),
            pl.BlockSpec((1, SSD_CONV_DIM), const),
            pl.BlockSpec((1, LANES), const),
            pl.BlockSpec((1, LANES), const),
            pl.BlockSpec((1, SSD_INNER), const),
            pl.BlockSpec((1, SSD_INNER), const),
            pl.BlockSpec((LANES, SSD_INNER), const),
            pl.BlockSpec((L, L), const),
    ]
    args = (proj, proj, proj, dt_raw, conv_w.astype(F32), conv_b.reshape(1, -1).astype(F32),
            pad_heads(dt_bias), pad_heads(a_log), d_chan, ssd_norm.reshape(1, -1).astype(F32), expand, tri)
    return in_specs, args


def _moba_steps(q_ref, k_ref, v_ref, avg_ref, hot_ref, eye_ref, o_ref, *, nb, q_tile):
    blk = MOBA_BLOCK
    n_dense = MOBA_TOPK + 1

    kmean = _dot(avg_ref[...], k_ref[...]).astype(BF16)
    jrow = lax.broadcasted_iota(jnp.int32, (nb, blk), 0)
    row = lax.broadcasted_iota(jnp.int32, (q_tile, blk), 0)
    col = lax.broadcasted_iota(jnp.int32, (q_tile, blk), 1)

    def query_block(i):
        q_i = q_ref[i * blk:(i + 1) * blk, :]
        if i < n_dense:
            return q_i, None
        gate_t = _nt_dot(kmean, q_i)[0:nb, :]
        valid = jrow < i
        gm = jnp.where(valid, gate_t, -jnp.inf)
        rank = jnp.zeros((nb, blk), F32)
        for r in range(1, nb):
            gr = pltpu.roll(gm, r, axis=0)
            lower = ((jrow - r) & (nb - 1)) < jrow
            rank = rank + jnp.where(gr > gm, 1.0, 0.0) + jnp.where((gr == gm) & lower, 1.0, 0.0)
        keep = valid & (rank < float(MOBA_TOPK))
        bias_t = jnp.where(keep, 0.0, NEG_BIG)
        bias_t = jnp.concatenate([bias_t, jnp.zeros((LANES - nb, blk), F32)], axis=0).astype(BF16)
        bias = _nt_dot(eye_ref[...], bias_t).astype(BF16)
        return q_i, jnp.concatenate([q_i, bias], axis=1)

    def scores(i, h, q_i, q_aug):
        rs = slice(h * q_tile, (h + 1) * q_tile)
        s_own = _nt_dot(q_i[rs, :], k_ref[i * blk:(i + 1) * blk, :])
        s_own = jnp.where(col <= row + h * q_tile, s_own, NEG_BIG)
        if i == 0:
            return s_own, None
        if q_aug is None:
            return s_own, _nt_dot(q_i[rs, :], k_ref[0:i * blk, :])
        k_past = jnp.concatenate([k_ref[0:i * blk, :], hot_ref[0:i * blk, :]], axis=1)
        return s_own, _nt_dot(q_aug[rs, :], k_past)

    def finish(i, h, s_own, s_past):
        m = jnp.max(s_own, axis=-1, keepdims=True)
        if s_past is None:
            p = jnp.exp2(s_own - m)
        else:
            m = jnp.maximum(m, jnp.max(s_past, axis=-1, keepdims=True))
            p = jnp.concatenate([jnp.exp2(s_past - m), jnp.exp2(s_own - m)], axis=1)
        vb = v_ref[0:(i + 1) * blk, :]
        pv = _dot(p.astype(BF16), jnp.concatenate([vb, jnp.ones_like(vb)], axis=1))
        d = vb.shape[1]
        o_ref[i * blk + h * q_tile:i * blk + (h + 1) * q_tile, :] = (pv[:, :d] / pv[:, d:]).astype(o_ref.dtype)

    tiles = [(i, h) for i in range(nb) for h in range(blk // q_tile)]
    qcache = {}
    pending = None
    for i, h in tiles:
        if i not in qcache:
            qcache = {i: query_block(i)}
        s_next = scores(i, h, *qcache[i])
        yield
        if pending is not None:
            finish(*pending)
        pending = (i, h) + s_next
        yield
    finish(*pending)
    yield


def _drain_body(*refs, steps, **kwargs):
    for _ in steps(*refs, **kwargs):
        pass


def _moba_operands(proj, seq):
    blk = MOBA_BLOCK
    nb = seq // blk
    d = MOBA_HEAD_DIM
    blk_of_key = jnp.arange(seq, dtype=jnp.int32) // blk
    avg = (jnp.arange(2 * nb, dtype=jnp.int32)[:, None] == blk_of_key[None, :]).astype(BF16) / blk
    hot = (blk_of_key[:, None] == jnp.arange(LANES, dtype=jnp.int32)[None, :]).astype(BF16)
    eye = jnp.eye(blk, dtype=BF16)

    def head_block(base):
        return pl.BlockSpec((seq, d), lambda b, h: (b, base // d + h))

    const = lambda b, h: (0, 0)
    in_specs = [head_block(P0_Q), head_block(P0_K), head_block(P0_V),
                pl.BlockSpec((2 * nb, seq), const), pl.BlockSpec((seq, LANES), const),
                pl.BlockSpec((blk, blk), const)]
    return in_specs, (proj, proj, proj, avg, hot, eye)


def _ssd(proj, dt_raw, conv_w, conv_b, dt_bias, a_log, d_skip, ssd_norm, bsz, seq, *, chunks_per_step=4):
    t = bsz * seq
    R = chunks_per_step * SSD_CHUNK
    in_specs, args = _ssd_operands(proj, dt_raw, conv_w, conv_b, dt_bias, a_log, d_skip, ssd_norm, seq, R)
    nc = seq // R
    return pl.pallas_call(
        functools.partial(_drain_body, steps=_ssd_steps),
        grid=(bsz, nc),
        in_specs=in_specs,
        out_specs=pl.BlockSpec((R, SSD_INNER), lambda b, c: (b * nc + c, 0)),
        out_shape=jax.ShapeDtypeStruct((t, SSD_INNER), BF16),
        scratch_shapes=[pltpu.VMEM((SSD_STATE, SSD_INNER), F32),
                        pltpu.VMEM((8 + SSD_CHUNK, SSD_CONV_DIM), F32)],
        compiler_params=pltpu.CompilerParams(
            dimension_semantics=("parallel", "arbitrary"), vmem_limit_bytes=VMEM_LIMIT),
        name="ssd_mixer",
    )(*args)


def _moba(proj, bsz, seq):
    t = bsz * seq
    in_specs, args = _moba_operands(proj, seq)
    return pl.pallas_call(
        functools.partial(_drain_body, steps=_moba_steps, nb=seq // MOBA_BLOCK, q_tile=256),
        grid=(bsz, MOBA_HEADS),
        in_specs=in_specs,
        out_specs=pl.BlockSpec((seq, MOBA_HEAD_DIM), lambda b, h: (b, h)),
        out_shape=jax.ShapeDtypeStruct((t, MOBA_INNER), BF16),
        compiler_params=pltpu.CompilerParams(
            dimension_semantics=("parallel", "parallel"), vmem_limit_bytes=VMEM_LIMIT),
        name="moba_attn",
    )(*args)


def _out_proj_body(*refs, n_plain, final_norm):
    plain = refs[:n_plain]
    a_ref, g_ref, x_ref, w_ref = refs[n_plain:n_plain + 4]
    o_ref = refs[-1]
    g = g_ref[...].astype(F32)
    gated = (a_ref[...].astype(F32) * (g * _sigmoid(g))).astype(BF16)
    k0 = 0
    y = x_ref[...]
    for r in plain:
        kw = r.shape[1]
        y = y + _dot(r[...], w_ref[k0:k0 + kw, :])
        k0 += kw
    y = y + _dot(gated, w_ref[k0:k0 + gated.shape[1], :])
    if final_norm:
        fn_ref = refs[-2]
        y = y * lax.rsqrt(jnp.mean(y * y, axis=-1, keepdims=True) + NORM_EPS) * fn_ref[...]
    o_ref[...] = y


def _out_proj(plain, gated, gate_src, gate_col, x2d, w, final_gain=None, *, tm=1024):
    t, d = x2d.shape
    kg = gated.shape[1]
    in_specs = [pl.BlockSpec((tm, a.shape[1]), lambda i: (i, 0)) for a in plain]
    in_specs += [pl.BlockSpec((tm, kg), lambda i: (i, 0)),
                 pl.BlockSpec((tm, kg), lambda i: (i, gate_col // kg)),
                 pl.BlockSpec((tm, d), lambda i: (i, 0)),
                 pl.BlockSpec(w.shape, lambda i: (0, 0))]
    args = list(plain) + [gated, gate_src, x2d, w]
    if final_gain is not None:
        in_specs.append(pl.BlockSpec((1, d), lambda i: (0, 0)))
        args.append(final_gain.reshape(1, d).astype(F32))
    return pl.pallas_call(
        functools.partial(_out_proj_body, n_plain=len(plain), final_norm=final_gain is not None),
        grid=(t // tm,),
        in_specs=in_specs,
        out_specs=pl.BlockSpec((tm, d), lambda i: (i, 0)),
        out_shape=jax.ShapeDtypeStruct((t, d), F32),
        compiler_params=pltpu.CompilerParams(
            dimension_semantics=("parallel",), vmem_limit_bytes=VMEM_LIMIT),
        name="out_proj_final" if final_gain is not None else "out_proj",
    )(*args)


DIL_PHASES = 8


def _dil_tiles(seq):
    ph, qb = DIL_PHASES, DIL_QBLOCK
    per = seq // ph
    tiles = []
    rows = qb // ph
    for n in range(seq // qb):
        qs = [(p * per + rows * n, rows) for p in range(ph)]
        if n == 0:
            ks, mask = [(p * per, rows) for p in range(ph)], "a_first"
        else:
            ks, mask = [(p * per + rows * (n - 1), 2 * rows) for p in range(ph)], "a"
        tiles.append((qs, ks, mask, "init"))
    seg = ph // 4
    rows = qb // seg
    for p4 in range(4):
        for n in range(seq // 4 // qb):
            qs = [((p4 + 4 * j) * per + rows * n, rows) for j in range(seg)]
            if n == 0:
                ks, mask = [((p4 + 4 * j) * per, rows) for j in range(seg)], "b_first"
            else:
                ks, mask = [((p4 + 4 * j) * per + rows * (n - 1), 2 * rows) for j in range(seg)], "b"
            tiles.append((qs, ks, mask, "merge"))
    for p in range(ph):
        for h in range(per // qb):
            qs = [(p * per + qb * h, qb)]
            ks = [(p * per, qb * (h + 1))]
            tiles.append((qs, ks, "c%d" % h, "final"))
    return tiles


def _dil_masks(seq):
    import numpy as np
    ph, qb = DIL_PHASES, DIL_QBLOCK
    per = seq // ph

    def pos(n_rows, seg_rows, step):
        i = np.arange(n_rows)
        return step * (i % seg_rows) + i // seg_rows

    def band(qpos, kpos, span):
        dist = qpos[:, None] - kpos[None, :]
        return np.where((dist >= 0) & (dist <= span), 0.0, NEG_BIG).astype(np.float32)

    masks = {}
    for name, seg in (("a", ph), ("b", ph // 4)):
        rows = qb // seg
        masks[name] = band(qb + pos(qb, rows, seg), pos(2 * qb, 2 * rows, seg), qb)
        masks[name + "_first"] = band(pos(qb, rows, seg), pos(qb, rows, seg), qb)
    for h in range(per // qb):
        lq = qb * h + np.arange(qb)
        lk = np.arange(qb * (h + 1))
        same = (lq[:, None] - lk[None, :]) % 2 == 0
        masks["c%d" % h] = np.where(same & (lk[None, :] <= lq[:, None]), 0.0, NEG_BIG).astype(np.float32)
    return masks


def _dil_body(*refs, seq, mask_names, group_size):
    q_ref, k_ref, v_ref = refs[:3]
    mask_refs = dict(zip(mask_names, refs[3:3 + len(mask_names)]))
    o_ref, onat_ref = refs[3 + len(mask_names):][:2]
    state_a, state_b = refs[-6:-3], refs[-3:]
    state_in = {"merge": state_a, "final": state_b}
    state_out = {"init": state_a, "merge": state_b}
    qb = DIL_QBLOCK
    d = DIL_HEAD_DIM

    def rows_of(ref, slices):
        parts = [ref[s:s + n, :] for s, n in slices]
        return parts[0] if len(parts) == 1 else jnp.concatenate(parts, axis=0)

    def put_rows(ref, slices, val):
        r0 = 0
        for s, n in slices:
            ref[s:s + n, :] = val[r0:r0 + n, :]
            r0 += n

    def cat(parts):
        return parts[0] if len(parts) == 1 else jnp.concatenate(parts, axis=0)

    def scores(group):
        mask = mask_refs[group[0][2]][...]
        return cat([_nt_dot(rows_of(q_ref, qs), rows_of(k_ref, ks)) + mask for qs, ks, _, _ in group])

    def probs(group, s):
        stage = group[0][3]
        m_new = jnp.broadcast_to(jnp.max(s, axis=-1, keepdims=True), (s.shape[0], d))
        if stage == "init":
            m_old = None
        else:
            m_ref = state_in[stage][1]
            m_old = cat([rows_of(m_ref, qs) for qs, _, _, _ in group])
            m_new = jnp.maximum(m_old, m_new)
        p = jnp.concatenate([jnp.exp2(s[:, c0:c0 + d] - m_new) for c0 in range(0, s.shape[1], d)], axis=1)
        return p.astype(BF16), m_new, m_old

    def accumulate(group, p, m_new, m_old):
        stage = group[0][3]
        pvs = []
        for g, (_, ks, _, _) in enumerate(group):
            vb = rows_of(v_ref, ks)
            pvs.append(_dot(p[g * qb:(g + 1) * qb, :], jnp.concatenate([vb, jnp.ones_like(vb)], axis=1)))
        pv = cat(pvs)
        acc, l = pv[:, :d], pv[:, d:]
        m_b = m_new
        if stage != "init":
            alpha = jnp.exp2(m_old - m_b)
            acc_ref, _, l_ref = state_in[stage]
            acc = alpha * cat([rows_of(acc_ref, qs) for qs, _, _, _ in group]) + acc
            l = alpha * cat([rows_of(l_ref, qs) for qs, _, _, _ in group]) + l
        if stage == "final":
            out = acc / l
        for g, (qs, _, _, _) in enumerate(group):
            rs = slice(g * qb, (g + 1) * qb)
            if stage == "final":
                (s0, _), = qs
                p_idx, l0 = divmod(s0, seq // DIL_PHASES)
                onat_ref[pl.ds(DIL_PHASES * l0 + p_idx, qb, stride=DIL_PHASES), :] = out[rs, :]
            else:
                acc_out, m_out, l_out = state_out[stage]
                put_rows(acc_out, qs, acc[rs, :])
                put_rows(l_out, qs, l[rs, :])
                put_rows(m_out, qs, m_b[rs, :])

    groups = []
    for tile in _dil_tiles(seq):
        if groups and len(groups[-1]) < group_size and groups[-1][0][2:] == tile[2:]:
            groups[-1].append(tile)
        else:
            groups.append([tile])

    def owned(group):
        return {r for qs, _, _, _ in group for s0, n in qs for r in range(s0, s0 + n)}

    for ga, gb in zip(groups, groups[1:]):
        assert not owned(ga) & owned(gb)
    scored, ready = None, None
    for group in groups + [None, None]:
        s_new = None if group is None else (group, scores(group))
        p_new = None if scored is None else (scored[0],) + probs(*scored)
        if ready is not None:
            accumulate(*ready)
        scored, ready = s_new, p_new
    o_ref[...] = onat_ref[...].astype(o_ref.dtype)


def _dilated(proj, bsz, seq):
    t = bsz * seq
    d = DIL_HEAD_DIM
    masks = _dil_masks(seq)
    names = tuple(sorted(masks))

    def head_block(base):
        return pl.BlockSpec((seq, d), lambda b, h: (b, base // d + h))

    return pl.pallas_call(
        functools.partial(_dil_body, seq=seq, mask_names=names, group_size=1),
        grid=(bsz, DIL_HEADS),
        in_specs=[head_block(DIL_INNER), head_block(2 * DIL_INNER), head_block(3 * DIL_INNER)]
        + [pl.BlockSpec(masks[n].shape, lambda b, h: (0, 0)) for n in names],
        out_specs=pl.BlockSpec((seq, d), lambda b, h: (b, h)),
        out_shape=jax.ShapeDtypeStruct((t, DIL_INNER), BF16),
        scratch_shapes=[pltpu.VMEM((seq, d), F32)] * 7,
        compiler_params=pltpu.CompilerParams(
            dimension_semantics=("parallel", "parallel"), vmem_limit_bytes=VMEM_LIMIT),
        name="dilated_attn",
    )(proj, proj, proj, *[jnp.asarray(masks[n]) for n in names])


def kernel(x, even_norm, even_w_in, ssd_conv_w, ssd_conv_b, ssd_dt_bias, ssd_a_log, ssd_d, ssd_norm,
           even_w_out, odd_norm, odd_w_in, odd_w_out, final_norm):
    bsz, seq, d = x.shape
    t = bsz * seq
    x2d = x.reshape(t, d)

    w = even_w_in[0]
    z_end = SSD_INNER
    xbc_end = z_end + SSD_CONV_DIM
    dt_end = xbc_end + SSD_HEADS
    q_end = dt_end + MOBA_INNER
    v_end = q_end + 2 * MOBA_INNER
    x_end = z_end + SSD_INNER
    moba_scale = MOBA_HEAD_DIM ** -0.5 * math.log2(math.e)
    g0 = even_norm[0].astype(F32)[None, :]
    w_t = w.T

    def rows(a, b, scale=1.0):
        return (w_t[a:b] * (g0 * scale)).astype(BF16)

    w_main = jnp.concatenate([rows(0, x_end), rows(v_end, w.shape[1]), rows(dt_end, q_end, moba_scale),
                              rows(q_end, v_end), rows(x_end, xbc_end)], axis=0)
    w_dt = jnp.pad(rows(xbc_end, dt_end), ((0, LANES - SSD_HEADS), (0, 0)))
    proj0, dt_raw = _norm_proj_t(x2d, w_main, w_dt, tm=seq // 2, tn=P0_N // 2)
    y_ssd = _ssd(proj0, dt_raw, ssd_conv_w[0], ssd_conv_b[0], ssd_dt_bias[0], ssd_a_log[0],
                 ssd_d[0], ssd_norm[0], bsz, seq)
    o_moba = _moba(proj0, bsz, seq)
    x1 = _out_proj([y_ssd], o_moba, proj0, P0_G, x2d, even_w_out[0].astype(BF16))

    w1 = odd_w_in[0]
    dil_scale = DIL_HEAD_DIM ** -0.5 * math.log2(math.e)
    g1 = odd_norm[0].astype(F32)[:, None]
    w1_b = jnp.concatenate([(w1[:, 3 * DIL_INNER:] * g1).astype(BF16),
                            (w1[:, :DIL_INNER] * (g1 * dil_scale)).astype(BF16),
                            (w1[:, DIL_INNER:3 * DIL_INNER] * g1).astype(BF16)], axis=1)
    proj1 = _norm_proj_phased(x1, w1_b, tm=seq, tn=1024, phases=DIL_PHASES, token_cols=DIL_INNER)
    o_dil = _dilated(proj1, bsz, seq)
    out = _out_proj([], o_dil, proj1, 0, x1, odd_w_out[0].astype(BF16), final_norm)
    return out.reshape(bsz, seq, d)
```

```python
import functools
import math

import jax
import jax.numpy as jnp
import numpy as np
from jax import lax
from jax.experimental import pallas as pl
from jax.experimental.pallas import tpu as pltpu

F32 = jnp.float32
BF16 = jnp.bfloat16

NORM_EPS = 1e-5
D_MODEL = 1024

SSD_HEADS = 16
SSD_HEAD_DIM = 64
SSD_INNER = SSD_HEADS * SSD_HEAD_DIM
SSD_GROUPS = 2
SSD_STATE = 128
SSD_CONV = 4
SSD_CHUNK = 128
SSD_BC = 2 * SSD_GROUPS * SSD_STATE
SSD_CONV_DIM = SSD_INNER + SSD_BC

MOBA_HEADS = 8
MOBA_HEAD_DIM = 128
MOBA_INNER = MOBA_HEADS * MOBA_HEAD_DIM
MOBA_BLOCK = 256
MOBA_TOPK = 3

DIL_HEADS = 16
DIL_HEAD_DIM = 128
DIL_INNER = DIL_HEADS * DIL_HEAD_DIM
DIL_QBLOCK = 128
DIL_PATTERNS = ((128, 1), (512, 4), (2048, 16))

LANES = 128
MXU_WIDTH = 256
NEG_BIG = -1e30
VMEM_LIMIT = 56 * 1024 * 1024

PROJ_ROW_SUB = 512
PROJ0_COL_TILES = 2
PROJ1_TN = 4 * MXU_WIDTH
OUT_PROJ_TM = 1024
SSD_CHUNKS_PER_STEP = 4
MOBA_Q_TILE = 256

P0_Z = 0
P0_X = P0_Z + SSD_INNER
P0_G = P0_X + SSD_INNER
P0_Q = P0_G + MOBA_INNER
P0_K = P0_Q + MOBA_INNER
P0_V = P0_K + MOBA_INNER
P0_BC = P0_V + MOBA_INNER
P0_N = P0_BC + SSD_BC


def _nt_dot(a, b):
    return lax.dot_general(a, b, (((1,), (1,)), ((), ())), preferred_element_type=F32)


def _dot(a, b):
    return jnp.dot(a, b, preferred_element_type=F32)


def _sigmoid(x):
    return 1.0 / (1.0 + jnp.exp2(x * -math.log2(math.e)))


def _norm_proj_body(*refs, kind, phases, n_token_tiles):
    if kind == "aux":
        x_ref, w_ref, waux_ref, o_ref, oaux_ref, hn_ref = refs
    else:
        x_ref, w_ref, o_ref, hn_ref, hnp_ref, hnf_ref = refs
    j = pl.program_id(1)
    tm = x_ref.shape[0]

    if kind == "aux":
        sub = min(tm, PROJ_ROW_SUB)

        def normed(r0):
            x = x_ref[r0:r0 + sub, :]
            ms = jnp.mean(x * x, axis=-1, keepdims=True)
            return (x * lax.rsqrt(ms + NORM_EPS)).astype(BF16)

        starts = list(range(0, tm, sub))
        nxt = normed(starts[0])
        for k, r0 in enumerate(starts):
            cur = nxt
            if k + 1 < len(starts):
                nxt = normed(starts[k + 1])
            hn_ref[r0:r0 + sub, :] = cur
            o_ref[r0:r0 + sub, :] = _nt_dot(cur, w_ref[...]).astype(o_ref.dtype)

        @pl.when(j == 0)
        def _():
            oaux_ref[...] = _nt_dot(hn_ref[...], waux_ref[...])

        return

    d = x_ref.shape[1]
    sub = min(tm, PROJ_ROW_SUB)
    starts = list(range(0, tm, sub))
    per = tm // phases
    assert n_token_tiles >= 2 and phases % len(starts) == 0

    def normed(r0):
        x = x_ref[r0:r0 + sub, :]
        ms = jnp.mean(x * x, axis=-1, keepdims=True)
        return x * lax.rsqrt(ms + NORM_EPS)

    def token_rows(side_work):
        nxt = normed(starts[0])
        for k, r0 in enumerate(starts):
            cur = nxt
            if k + 1 < len(starts):
                nxt = normed(starts[k + 1])
            side_work(k, r0, cur)
            o_ref[r0:r0 + sub, :] = _dot(cur.astype(BF16), w_ref[...]).astype(o_ref.dtype)

    def keep(k, r0, hn):
        hn_ref[r0:r0 + sub, :] = hn.astype(BF16)
        for cc in range(d // LANES):
            hnf_ref[cc, r0:r0 + sub, :] = hn[:, cc * LANES:(cc + 1) * LANES]

    def permute(first, k0):
        def side_work(k):
            lo = first + 2 * (k - k0)
            for p in range(max(lo, first), min(lo + 2, phases)):
                for cc in range(d // LANES):
                    hnp_ref[p * per:(p + 1) * per, cc * LANES:(cc + 1) * LANES] = (
                        hnf_ref[cc, pl.ds(p, per, stride=phases), :].astype(BF16))
        return side_work

    @pl.when(j == 0)
    def _():
        token_rows(keep)

    def stored_rows(src_ref, side_work=None):
        for k, r0 in enumerate(starts):
            if side_work is not None:
                side_work(k)
            o_ref[r0:r0 + sub, :] = _dot(src_ref[r0:r0 + sub, :], w_ref[...]).astype(o_ref.dtype)

    assert n_token_tiles == 2 and phases == 2 * len(starts)

    @pl.when(j == 1)
    def _():
        stored_rows(hn_ref)

    @pl.when(j == 2)
    def _():
        build = permute(0, 0)
        build(0)
        for k, r0 in enumerate(starts):
            if k + 1 < len(starts):
                build(k + 1)
            o_ref[r0:r0 + sub, :] = _dot(hnp_ref[r0:r0 + sub, :], w_ref[...]).astype(o_ref.dtype)

    @pl.when(j > 2)
    def _():
        stored_rows(hnp_ref)


def _norm_proj_t(x2d, w_t, w_aux_t, *, tm, tn):
    t, d = x2d.shape
    n = w_t.shape[0]
    return pl.pallas_call(
        functools.partial(_norm_proj_body, kind="aux", phases=0, n_token_tiles=0),
        grid=(t // tm, n // tn),
        in_specs=[pl.BlockSpec((tm, d), lambda i, j: (i, 0)),
                  pl.BlockSpec((tn, d), lambda i, j: (j, 0)),
                  pl.BlockSpec((LANES, d), lambda i, j: (0, 0))],
        out_specs=[pl.BlockSpec((tm, tn), lambda i, j: (i, j)),
                   pl.BlockSpec((tm, LANES), lambda i, j: (i, 0))],
        out_shape=[jax.ShapeDtypeStruct((t, n), BF16), jax.ShapeDtypeStruct((t, LANES), F32)],
        scratch_shapes=[pltpu.VMEM((tm, d), BF16)],
        compiler_params=pltpu.CompilerParams(
            dimension_semantics=("parallel", "arbitrary"), vmem_limit_bytes=VMEM_LIMIT),
        name="norm_proj_aux",
    )(x2d, w_t, w_aux_t)


def _norm_proj_phased(x2d, w, *, tm, tn, phases, token_cols):
    t, d = x2d.shape
    n = w.shape[1]
    return pl.pallas_call(
        functools.partial(_norm_proj_body, kind="phased", phases=phases, n_token_tiles=token_cols // tn),
        grid=(t // tm, n // tn),
        in_specs=[pl.BlockSpec((tm, d), lambda i, j: (i, 0)),
                  pl.BlockSpec((d, tn), lambda i, j: (0, j))],
        out_specs=pl.BlockSpec((tm, tn), lambda i, j: (i, j)),
        out_shape=jax.ShapeDtypeStruct((t, n), BF16),
        scratch_shapes=[pltpu.VMEM((tm, d), BF16), pltpu.VMEM((tm, d), BF16),
                        pltpu.VMEM((d // LANES, tm, LANES), F32)],
        compiler_params=pltpu.CompilerParams(
            dimension_semantics=("parallel", "arbitrary"), vmem_limit_bytes=VMEM_LIMIT),
        name="norm_proj",
    )(x2d, w)


def _split3(a):
    hi = a.astype(BF16)
    r = a - hi.astype(F32)
    mid = r.astype(BF16)
    lo = (r - mid.astype(F32)).astype(BF16)
    return hi, mid, lo


def _expand_heads(a, e):
    hi, mid, lo = _split3(a)
    return _dot(hi, e) + _dot(mid, e) + _dot(lo, e)


def _ssd_body(z_ref, xs_ref, bc_ref, dt_ref, cw_ref, cb_ref, dtb_ref, alog_ref, dskip_ref, nrm_ref,
              e_ref, tri_ref, y_ref, state_ref, ubuf_ref):
    c = pl.program_id(1)
    L = SSD_CHUNK
    tail = 8

    @pl.when(c == 0)
    def _():
        state_ref[...] = jnp.zeros_like(state_ref)
        ubuf_ref[0:tail, :] = jnp.zeros((tail, SSD_CONV_DIM), F32)

    row = lax.broadcasted_iota(jnp.int32, (L, L), 0)
    col = lax.broadcasted_iota(jnp.int32, (L, L), 1)
    causal = col <= row
    lane = lax.broadcasted_iota(jnp.int32, (L, LANES), 1)
    low_half = lane < SSD_HEAD_DIM
    heads_per_group = SSD_HEADS // SSD_GROUPS
    gw = heads_per_group * SSD_HEAD_DIM

    def chunk(r):
        acts = []
        cw = cw_ref[...]
        for c0 in range(0, SSD_CONV_DIM, SSD_BC):
            cols = slice(c0, c0 + SSD_BC)
            src = xs_ref[r, cols] if c0 < SSD_INNER else bc_ref[r, :]
            u = src.astype(F32)
            ubuf_ref[tail:tail + L, cols] = u
            acc = cb_ref[:, cols] + cw[SSD_CONV - 1:SSD_CONV, cols] * u
            for k in range(SSD_CONV - 1):
                acc = acc + cw[k:k + 1, cols] * ubuf_ref[pl.ds(tail - (SSD_CONV - 1) + k, L), cols]
            ubuf_ref[0:tail, cols] = u[L - tail:L, :]
            acts.append(acc * _sigmoid(acc))
        act = jnp.concatenate(acts, axis=1)

        xs = act[:, :SSD_INNER]

        dtr = dt_ref[r, :] + dtb_ref[...]
        dt = jnp.maximum(dtr, 0.0) + jnp.log1p(jnp.exp(-jnp.abs(dtr)))
        a = -jnp.exp(alog_ref[...]) * math.log2(math.e)
        ac = dt * a
        tri = tri_ref[...]
        hi, mid, lo = _split3(ac)
        a_cum = _dot(tri, hi) + _dot(tri, mid) + _dot(tri, lo)
        a_last = a_cum[L - 1:L, :]
        dec_states = jnp.exp2(a_last - a_cum)
        exp_acum = jnp.exp2(a_cum)
        e = e_ref[...]
        dt_e = _expand_heads(dt, e)
        dec_e = _expand_heads(dec_states, e)
        ea_e = _expand_heads(exp_acum, e)
        a_cum_t = a_cum.T

        xc = xs * dt_e
        xc_b = xc.astype(BF16)
        xdec_b = (xc * dec_e).astype(BF16)

        for g in range(SSD_GROUPS):
            b_g = act[:, SSD_INNER + g * SSD_STATE:SSD_INNER + (g + 1) * SSD_STATE]
            c_g = act[:, SSD_INNER + (SSD_GROUPS + g) * SSD_STATE:
                      SSD_INNER + (SSD_GROUPS + g + 1) * SSD_STATE]
            b_gb = b_g.astype(BF16)
            c_gb = c_g.astype(BF16)
            scores = _nt_dot(c_gb, b_gb)
            gs = slice(g * gw, (g + 1) * gw)

            prev = state_ref[:, gs]
            y_off = _dot(c_gb, prev.astype(BF16))
            st_new = _dot(b_g.T.astype(BF16), xdec_b[:, gs])
            state_ref[:, gs] = prev * ea_e[L - 1:L, gs] + st_new

            pieces = []
            for pair in range(heads_per_group // 2):
                ms = []
                for hh in range(2):
                    h = g * heads_per_group + 2 * pair + hh
                    seg = a_cum[:, h:h + 1] - a_cum_t[h:h + 1, :]
                    lmat = jnp.exp2(jnp.where(causal, seg, NEG_BIG))
                    ms.append((scores * lmat).astype(BF16))
                m_pair = jnp.concatenate(ms, axis=1)
                cs = slice(g * gw + pair * LANES, g * gw + (pair + 1) * LANES)
                x_pair = xc_b[:, cs]
                zero = jnp.zeros_like(x_pair)
                rhs = jnp.concatenate([jnp.where(low_half, x_pair, zero),
                                       jnp.where(low_half, zero, x_pair)], axis=0)
                pieces.append(_dot(m_pair, rhs))
            y_diag = jnp.concatenate(pieces, axis=1)

            y = y_diag + y_off * ea_e[:, gs] + xs[:, gs] * dskip_ref[:, gs]
            zg = z_ref[r, gs].astype(F32)
            ug = y * (zg * _sigmoid(zg))
            ug = ug * lax.rsqrt(jnp.mean(ug * ug, axis=-1, keepdims=True) + NORM_EPS)
            y_ref[r, gs] = (ug * nrm_ref[:, gs]).astype(y_ref.dtype)

    for ci in range(xs_ref.shape[0] // L):
        chunk(slice(ci * L, (ci + 1) * L))


def _ssd_operands(proj, dt_raw, conv_w, conv_b, dt_bias, a_log, d_skip, ssd_norm, seq, R):
    L = SSD_CHUNK
    nc = seq // R

    def pad_heads(v):
        return jnp.pad(v.astype(F32), (0, LANES - SSD_HEADS)).reshape(1, LANES)

    head_of_chan = jnp.arange(SSD_INNER, dtype=jnp.int32) // SSD_HEAD_DIM
    expand = (jnp.arange(LANES, dtype=jnp.int32)[:, None] == head_of_chan[None, :]).astype(BF16)
    tri = (jnp.arange(L)[:, None] >= jnp.arange(L)[None, :]).astype(BF16)
    d_chan = jnp.repeat(d_skip.astype(F32), SSD_HEAD_DIM).reshape(1, SSD_INNER)

    def rows(b, c):
        return b * nc + c

    const = lambda b, c: (0, 0)
    in_specs = [
            pl.BlockSpec((R, SSD_INNER), lambda b, c: (rows(b, c), P0_Z // SSD_INNER)),
            pl.BlockSpec((R, SSD_INNER), lambda b, c: (rows(b, c), P0_X // SSD_INNER)),
            pl.BlockSpec((R, SSD_BC), lambda b, c: (rows(b, c), P0_BC // SSD_BC)),
            pl.BlockSpec((R, LANES), lambda b, c: (rows(b, c), 0)),
            pl.BlockSpec((SSD_CONV, SSD_CONV_DIM), const),
            pl.BlockSpec((1, SSD_CONV_DIM), const),
            pl.BlockSpec((1, LANES), const),
            pl.BlockSpec((1, LANES), const),
            pl.BlockSpec((1, SSD_INNER), const),
            pl.BlockSpec((1, SSD_INNER), const),
            pl.BlockSpec((LANES, SSD_INNER), const),
            pl.BlockSpec((L, L), const),
    ]
    args = (proj, proj, proj, dt_raw, conv_w.astype(F32), conv_b.reshape(1, -1).astype(F32),
            pad_heads(dt_bias), pad_heads(a_log), d_chan, ssd_norm.reshape(1, -1).astype(F32), expand, tri)
    return in_specs, args


def _moba_body(q_ref, k_ref, v_ref, avg_ref, hot_ref, eye_ref, o_ref, *, nb, q_tile):
    blk = MOBA_BLOCK
    n_dense = MOBA_TOPK + 1

    kmean = _dot(avg_ref[...], k_ref[...]).astype(BF16)
    jrow = lax.broadcasted_iota(jnp.int32, (nb, blk), 0)
    row = lax.broadcasted_iota(jnp.int32, (q_tile, blk), 0)
    col = lax.broadcasted_iota(jnp.int32, (q_tile, blk), 1)

    def query_block(i):
        q_i = q_ref[i * blk:(i + 1) * blk, :]
        if i < n_dense:
            return q_i, None
        gate_t = _nt_dot(kmean, q_i)[0:nb, :]
        valid = jrow < i
        gm = jnp.where(valid, gate_t, -jnp.inf)
        rank = jnp.zeros((nb, blk), F32)
        for r in range(1, nb):
            gr = pltpu.roll(gm, r, axis=0)
            lower = ((jrow - r) & (nb - 1)) < jrow
            rank = rank + jnp.where(gr > gm, 1.0, 0.0) + jnp.where((gr == gm) & lower, 1.0, 0.0)
        keep = valid & (rank < float(MOBA_TOPK))
        bias_t = jnp.where(keep, 0.0, NEG_BIG)
        bias_t = jnp.concatenate([bias_t, jnp.zeros((LANES - nb, blk), F32)], axis=0).astype(BF16)
        bias = _nt_dot(eye_ref[...], bias_t).astype(BF16)
        return q_i, jnp.concatenate([q_i, bias], axis=1)

    def scores(i, h, q_i, q_aug):
        rs = slice(h * q_tile, (h + 1) * q_tile)
        s_own = _nt_dot(q_i[rs, :], k_ref[i * blk:(i + 1) * blk, :])
        s_own = jnp.where(col <= row + h * q_tile, s_own, NEG_BIG)
        if i == 0:
            return s_own, None
        if q_aug is None:
            return s_own, _nt_dot(q_i[rs, :], k_ref[0:i * blk, :])
        k_past = jnp.concatenate([k_ref[0:i * blk, :], hot_ref[0:i * blk, :]], axis=1)
        return s_own, _nt_dot(q_aug[rs, :], k_past)

    def finish(i, h, s_own, s_past):
        m = jnp.max(s_own, axis=-1, keepdims=True)
        if s_past is None:
            p = jnp.exp2(s_own - m)
        else:
            m = jnp.maximum(m, jnp.max(s_past, axis=-1, keepdims=True))
            p = jnp.concatenate([jnp.exp2(s_past - m), jnp.exp2(s_own - m)], axis=1)
        vb = v_ref[0:(i + 1) * blk, :]
        pv = _dot(p.astype(BF16), jnp.concatenate([vb, jnp.ones_like(vb)], axis=1))
        d = vb.shape[1]
        o_ref[i * blk + h * q_tile:i * blk + (h + 1) * q_tile, :] = (pv[:, :d] / pv[:, d:]).astype(o_ref.dtype)

    tiles = [(i, h) for i in range(nb) for h in range(blk // q_tile)]
    qcache = {}
    pending = None
    for i, h in tiles:
        if i not in qcache:
            qcache = {i: query_block(i)}
        s_next = scores(i, h, *qcache[i])
        if pending is not None:
            finish(*pending)
        pending = (i, h) + s_next
    finish(*pending)


def _moba_operands(proj, seq):
    blk = MOBA_BLOCK
    nb = seq // blk
    d = MOBA_HEAD_DIM
    blk_of_key = jnp.arange(seq, dtype=jnp.int32) // blk
    avg = (jnp.arange(2 * nb, dtype=jnp.int32)[:, None] == blk_of_key[None, :]).astype(BF16) / blk
    hot = (blk_of_key[:, None] == jnp.arange(LANES, dtype=jnp.int32)[None, :]).astype(BF16)
    eye = jnp.eye(blk, dtype=BF16)

    def head_block(base):
        return pl.BlockSpec((seq, d), lambda b, h: (b, base // d + h))

    const = lambda b, h: (0, 0)
    in_specs = [head_block(P0_Q), head_block(P0_K), head_block(P0_V),
                pl.BlockSpec((2 * nb, seq), const), pl.BlockSpec((seq, LANES), const),
                pl.BlockSpec((blk, blk), const)]
    return in_specs, (proj, proj, proj, avg, hot, eye)


def _ssd(proj, dt_raw, conv_w, conv_b, dt_bias, a_log, d_skip, ssd_norm, bsz, seq, *, chunks_per_step=SSD_CHUNKS_PER_STEP):
    t = bsz * seq
    R = chunks_per_step * SSD_CHUNK
    in_specs, args = _ssd_operands(proj, dt_raw, conv_w, conv_b, dt_bias, a_log, d_skip, ssd_norm, seq, R)
    nc = seq // R
    return pl.pallas_call(
        _ssd_body,
        grid=(bsz, nc),
        in_specs=in_specs,
        out_specs=pl.BlockSpec((R, SSD_INNER), lambda b, c: (b * nc + c, 0)),
        out_shape=jax.ShapeDtypeStruct((t, SSD_INNER), BF16),
        scratch_shapes=[pltpu.VMEM((SSD_STATE, SSD_INNER), F32),
                        pltpu.VMEM((8 + SSD_CHUNK, SSD_CONV_DIM), F32)],
        compiler_params=pltpu.CompilerParams(
            dimension_semantics=("parallel", "arbitrary"), vmem_limit_bytes=VMEM_LIMIT),
        name="ssd_mixer",
    )(*args)


def _moba(proj, bsz, seq):
    t = bsz * seq
    in_specs, args = _moba_operands(proj, seq)
    return pl.pallas_call(
        functools.partial(_moba_body, nb=seq // MOBA_BLOCK, q_tile=MOBA_Q_TILE),
        grid=(bsz, MOBA_HEADS),
        in_specs=in_specs,
        out_specs=pl.BlockSpec((seq, MOBA_HEAD_DIM), lambda b, h: (b, h)),
        out_shape=jax.ShapeDtypeStruct((t, MOBA_INNER), BF16),
        compiler_params=pltpu.CompilerParams(
            dimension_semantics=("parallel", "parallel"), vmem_limit_bytes=VMEM_LIMIT),
        name="moba_attn",
    )(*args)


def _out_proj_body(*refs, n_plain, final_norm):
    plain = refs[:n_plain]
    a_ref, g_ref, x_ref, w_ref = refs[n_plain:n_plain + 4]
    o_ref = refs[-1]
    g = g_ref[...].astype(F32)
    gated = (a_ref[...].astype(F32) * (g * _sigmoid(g))).astype(BF16)
    k0 = 0
    y = x_ref[...]
    for r in plain:
        kw = r.shape[1]
        y = y + _dot(r[...], w_ref[k0:k0 + kw, :])
        k0 += kw
    y = y + _dot(gated, w_ref[k0:k0 + gated.shape[1], :])
    if final_norm:
        fn_ref = refs[-2]
        y = y * lax.rsqrt(jnp.mean(y * y, axis=-1, keepdims=True) + NORM_EPS) * fn_ref[...]
    o_ref[...] = y


def _out_proj(plain, gated, gate_src, gate_col, x2d, w, final_gain=None, *, tm=OUT_PROJ_TM):
    t, d = x2d.shape
    kg = gated.shape[1]
    in_specs = [pl.BlockSpec((tm, a.shape[1]), lambda i: (i, 0)) for a in plain]
    in_specs += [pl.BlockSpec((tm, kg), lambda i: (i, 0)),
                 pl.BlockSpec((tm, kg), lambda i: (i, gate_col // kg)),
                 pl.BlockSpec((tm, d), lambda i: (i, 0)),
                 pl.BlockSpec(w.shape, lambda i: (0, 0))]
    args = list(plain) + [gated, gate_src, x2d, w]
    if final_gain is not None:
        in_specs.append(pl.BlockSpec((1, d), lambda i: (0, 0)))
        args.append(final_gain.reshape(1, d).astype(F32))
    return pl.pallas_call(
        functools.partial(_out_proj_body, n_plain=len(plain), final_norm=final_gain is not None),
        grid=(t // tm,),
        in_specs=in_specs,
        out_specs=pl.BlockSpec((tm, d), lambda i: (i, 0)),
        out_shape=jax.ShapeDtypeStruct((t, d), F32),
        compiler_params=pltpu.CompilerParams(
            dimension_semantics=("parallel",), vmem_limit_bytes=VMEM_LIMIT),
        name="out_proj_final" if final_gain is not None else "out_proj",
    )(*args)


DIL_PHASES = 8


def _dil_tiles(seq):
    ph, qb = DIL_PHASES, DIL_QBLOCK
    per = seq // ph
    tiles = []
    rows = qb // ph
    for n in range(seq // qb):
        qs = [(p * per + rows * n, rows) for p in range(ph)]
        if n == 0:
            ks, mask = [(p * per, rows) for p in range(ph)], "a_first"
        else:
            ks, mask = [(p * per + rows * (n - 1), 2 * rows) for p in range(ph)], "a"
        tiles.append((qs, ks, mask, "init"))
    seg = ph // 4
    rows = qb // seg
    for p4 in range(4):
        for n in range(seq // 4 // qb):
            qs = [((p4 + 4 * j) * per + rows * n, rows) for j in range(seg)]
            if n == 0:
                ks, mask = [((p4 + 4 * j) * per, rows) for j in range(seg)], "b_first"
            else:
                ks, mask = [((p4 + 4 * j) * per + rows * (n - 1), 2 * rows) for j in range(seg)], "b"
            tiles.append((qs, ks, mask, "merge"))
    for p in range(ph):
        for h in range(per // qb):
            qs = [(p * per + qb * h, qb)]
            ks = [(p * per, qb * (h + 1))]
            tiles.append((qs, ks, "c%d" % h, "final"))
    return tiles


def _dil_masks(seq):
    ph, qb = DIL_PHASES, DIL_QBLOCK
    per = seq // ph

    def pos(n_rows, seg_rows, step):
        i = np.arange(n_rows)
        return step * (i % seg_rows) + i // seg_rows

    def band(qpos, kpos, span):
        dist = qpos[:, None] - kpos[None, :]
        return np.where((dist >= 0) & (dist <= span), 0.0, NEG_BIG).astype(np.float32)

    masks = {}
    for name, seg in (("a", ph), ("b", ph // 4)):
        rows = qb // seg
        masks[name] = band(qb + pos(qb, rows, seg), pos(2 * qb, 2 * rows, seg), qb)
        masks[name + "_first"] = band(pos(qb, rows, seg), pos(qb, rows, seg), qb)
    for h in range(per // qb):
        lq = qb * h + np.arange(qb)
        lk = np.arange(qb * (h + 1))
        same = (lq[:, None] - lk[None, :]) % 2 == 0
        masks["c%d" % h] = np.where(same & (lk[None, :] <= lq[:, None]), 0.0, NEG_BIG).astype(np.float32)
    return masks


def _dil_body(*refs, seq, mask_names):
    q_ref, k_ref, v_ref = refs[:3]
    mask_refs = dict(zip(mask_names, refs[3:3 + len(mask_names)]))
    o_ref, onat_ref = refs[3 + len(mask_names):][:2]
    state_a, state_b = refs[-6:-3], refs[-3:]
    state_in = {"merge": state_a, "final": state_b}
    state_out = {"init": state_a, "merge": state_b}
    qb = DIL_QBLOCK
    d = DIL_HEAD_DIM

    def rows_of(ref, slices):
        parts = [ref[s:s + n, :] for s, n in slices]
        return parts[0] if len(parts) == 1 else jnp.concatenate(parts, axis=0)

    def put_rows(ref, slices, val):
        r0 = 0
        for s, n in slices:
            ref[s:s + n, :] = val[r0:r0 + n, :]
            r0 += n

    def scores(tile):
        qs, ks, mask, _ = tile
        return _nt_dot(rows_of(q_ref, qs), rows_of(k_ref, ks)) + mask_refs[mask][...]

    def probs(tile, s):
        qs, _, _, stage = tile
        m_new = jnp.broadcast_to(jnp.max(s, axis=-1, keepdims=True), (s.shape[0], d))
        m_old = None
        if stage != "init":
            m_old = rows_of(state_in[stage][1], qs)
            m_new = jnp.maximum(m_old, m_new)
        p = jnp.concatenate([jnp.exp2(s[:, c0:c0 + d] - m_new) for c0 in range(0, s.shape[1], d)], axis=1)
        return p.astype(BF16), m_new, m_old

    def accumulate(tile, p, m_new, m_old):
        qs, ks, _, stage = tile
        vb = rows_of(v_ref, ks)
        pv = _dot(p, jnp.concatenate([vb, jnp.ones_like(vb)], axis=1))
        acc, l = pv[:, :d], pv[:, d:]
        if stage != "init":
            alpha = jnp.exp2(m_old - m_new)
            acc_ref, _, l_ref = state_in[stage]
            acc = alpha * rows_of(acc_ref, qs) + acc
            l = alpha * rows_of(l_ref, qs) + l
        if stage == "final":
            (s0, _), = qs
            p_idx, l0 = divmod(s0, seq // DIL_PHASES)
            onat_ref[pl.ds(DIL_PHASES * l0 + p_idx, qb, stride=DIL_PHASES), :] = acc / l
        else:
            acc_out, m_out, l_out = state_out[stage]
            put_rows(acc_out, qs, acc)
            put_rows(l_out, qs, l)
            put_rows(m_out, qs, m_new)

    tiles = _dil_tiles(seq)

    def owned(tile):
        return {r for s0, n in tile[0] for r in range(s0, s0 + n)}

    for ta, tb in zip(tiles, tiles[1:]):
        assert not owned(ta) & owned(tb)
    scored, ready = None, None
    for tile in tiles + [None, None]:
        s_new = None if tile is None else (tile, scores(tile))
        p_new = None if scored is None else (scored[0],) + probs(*scored)
        if ready is not None:
            accumulate(*ready)
        scored, ready = s_new, p_new
    o_ref[...] = onat_ref[...].astype(o_ref.dtype)


def _dilated(proj, bsz, seq):
    t = bsz * seq
    d = DIL_HEAD_DIM
    masks = _dil_masks(seq)
    names = tuple(sorted(masks))

    def head_block(base):
        return pl.BlockSpec((seq, d), lambda b, h: (b, base // d + h))

    return pl.pallas_call(
        functools.partial(_dil_body, seq=seq, mask_names=names),
        grid=(bsz, DIL_HEADS),
        in_specs=[head_block(DIL_INNER), head_block(2 * DIL_INNER), head_block(3 * DIL_INNER)]
        + [pl.BlockSpec(masks[n].shape, lambda b, h: (0, 0)) for n in names],
        out_specs=pl.BlockSpec((seq, d), lambda b, h: (b, h)),
        out_shape=jax.ShapeDtypeStruct((t, DIL_INNER), BF16),
        scratch_shapes=[pltpu.VMEM((seq, d), F32)] * 7,
        compiler_params=pltpu.CompilerParams(
            dimension_semantics=("parallel", "parallel"), vmem_limit_bytes=VMEM_LIMIT),
        name="dilated_attn",
    )(proj, proj, proj, *[jnp.asarray(masks[n]) for n in names])


def kernel(x, even_norm, even_w_in, ssd_conv_w, ssd_conv_b, ssd_dt_bias, ssd_a_log, ssd_d, ssd_norm,
           even_w_out, odd_norm, odd_w_in, odd_w_out, final_norm):
    bsz, seq, d = x.shape
    t = bsz * seq
    x2d = x.reshape(t, d)

    w = even_w_in[0]
    z_end = SSD_INNER
    xbc_end = z_end + SSD_CONV_DIM
    dt_end = xbc_end + SSD_HEADS
    q_end = dt_end + MOBA_INNER
    v_end = q_end + 2 * MOBA_INNER
    x_end = z_end + SSD_INNER
    moba_scale = MOBA_HEAD_DIM ** -0.5 * math.log2(math.e)
    g0 = even_norm[0].astype(F32)[None, :]
    w_t = w.T

    def rows(a, b, scale=1.0):
        return (w_t[a:b] * (g0 * scale)).astype(BF16)

    w_main = jnp.concatenate([rows(0, x_end), rows(v_end, w.shape[1]), rows(dt_end, q_end, moba_scale),
                              rows(q_end, v_end), rows(x_end, xbc_end)], axis=0)
    w_dt = jnp.pad(rows(xbc_end, dt_end), ((0, LANES - SSD_HEADS), (0, 0)))
    proj0, dt_raw = _norm_proj_t(x2d, w_main, w_dt, tm=seq // 2, tn=P0_N // PROJ0_COL_TILES)
    y_ssd = _ssd(proj0, dt_raw, ssd_conv_w[0], ssd_conv_b[0], ssd_dt_bias[0], ssd_a_log[0],
                 ssd_d[0], ssd_norm[0], bsz, seq)
    o_moba = _moba(proj0, bsz, seq)
    x1 = _out_proj([y_ssd], o_moba, proj0, P0_G, x2d, even_w_out[0].astype(BF16))

    w1 = odd_w_in[0]
    dil_scale = DIL_HEAD_DIM ** -0.5 * math.log2(math.e)
    g1 = odd_norm[0].astype(F32)[:, None]
    w1_b = jnp.concatenate([(w1[:, 3 * DIL_INNER:] * g1).astype(BF16),
                            (w1[:, :DIL_INNER] * (g1 * dil_scale)).astype(BF16),
                            (w1[:, DIL_INNER:3 * DIL_INNER] * g1).astype(BF16)], axis=1)
    proj1 = _norm_proj_phased(x1, w1_b, tm=seq, tn=PROJ1_TN, phases=DIL_PHASES, token_cols=DIL_INNER)
    o_dil = _dilated(proj1, bsz, seq)
    out = _out_proj([], o_dil, proj1, 0, x1, odd_w_out[0].astype(BF16), final_norm)
    return out.reshape(bsz, seq, d)
```

```python
import functools
import math

import jax
import jax.numpy as jnp
import numpy as np
from jax import lax
from jax.experimental import pallas as pl
from jax.experimental.pallas import tpu as pltpu

F32 = jnp.float32
BF16 = jnp.bfloat16

NORM_EPS = 1e-5
D_MODEL = 1024

SSD_HEADS = 16
SSD_HEAD_DIM = 64
SSD_INNER = SSD_HEADS * SSD_HEAD_DIM
SSD_GROUPS = 2
SSD_STATE = 128
SSD_CONV = 4
SSD_CHUNK = 128
SSD_BC = 2 * SSD_GROUPS * SSD_STATE
SSD_CONV_DIM = SSD_INNER + SSD_BC

MOBA_HEADS = 8
MOBA_HEAD_DIM = 128
MOBA_INNER = MOBA_HEADS * MOBA_HEAD_DIM
MOBA_BLOCK = 256
MOBA_TOPK = 3

DIL_HEADS = 16
DIL_HEAD_DIM = 128
DIL_INNER = DIL_HEADS * DIL_HEAD_DIM
DIL_QBLOCK = 128
DIL_PATTERNS = ((128, 1), (512, 4), (2048, 16))

LANES = 128
MXU_WIDTH = 256
NEG_BIG = -1e30
VMEM_LIMIT = 56 * 1024 * 1024

PROJ_ROW_SUB = 512
PROJ0_COL_TILES = 2
PROJ1_TN = 4 * MXU_WIDTH
OUT_PROJ_TM = 1024
SSD_CHUNKS_PER_STEP = 8
MOBA_Q_TILE = 256

P0_Z = 0
P0_X = P0_Z + SSD_INNER
P0_G = P0_X + SSD_INNER
P0_Q = P0_G + MOBA_INNER
P0_K = P0_Q + MOBA_INNER
P0_V = P0_K + MOBA_INNER
P0_BC = P0_V + MOBA_INNER
P0_N = P0_BC + SSD_BC


def _nt_dot(a, b):
    return lax.dot_general(a, b, (((1,), (1,)), ((), ())), preferred_element_type=F32)


def _dot(a, b):
    return jnp.dot(a, b, preferred_element_type=F32)


def _sigmoid(x):
    return 1.0 / (1.0 + jnp.exp2(x * -math.log2(math.e)))


def _norm_proj_body(*refs, kind, phases, n_token_tiles):
    if kind == "aux":
        x_ref, w_ref, waux_ref, o_ref, oaux_ref, hn_ref = refs
    else:
        x_ref, w_ref, o_ref, hn_ref, hnp_ref, hnf_ref = refs
    j = pl.program_id(1)
    tm = x_ref.shape[0]

    if kind == "aux":
        sub = min(tm, PROJ_ROW_SUB)

        def normed(r0):
            x = x_ref[r0:r0 + sub, :]
            ms = jnp.mean(x * x, axis=-1, keepdims=True)
            return (x * lax.rsqrt(ms + NORM_EPS)).astype(BF16)

        starts = list(range(0, tm, sub))
        nxt = normed(starts[0])
        for k, r0 in enumerate(starts):
            cur = nxt
            if k + 1 < len(starts):
                nxt = normed(starts[k + 1])
            hn_ref[r0:r0 + sub, :] = cur
            o_ref[r0:r0 + sub, :] = _nt_dot(cur, w_ref[...]).astype(o_ref.dtype)

        @pl.when(j == 0)
        def _():
            oaux_ref[...] = _nt_dot(hn_ref[...], waux_ref[...])

        return

    d = x_ref.shape[1]
    sub = min(tm, PROJ_ROW_SUB)
    starts = list(range(0, tm, sub))
    per = tm // phases
    assert n_token_tiles >= 2 and phases % len(starts) == 0

    def normed(r0):
        x = x_ref[r0:r0 + sub, :]
        ms = jnp.mean(x * x, axis=-1, keepdims=True)
        return x * lax.rsqrt(ms + NORM_EPS)

    def token_rows(side_work):
        nxt = normed(starts[0])
        for k, r0 in enumerate(starts):
            cur = nxt
            if k + 1 < len(starts):
                nxt = normed(starts[k + 1])
            side_work(k, r0, cur)
            o_ref[r0:r0 + sub, :] = _dot(cur.astype(BF16), w_ref[...]).astype(o_ref.dtype)

    def keep(k, r0, hn):
        hn_ref[r0:r0 + sub, :] = hn.astype(BF16)
        for cc in range(d // LANES):
            hnf_ref[cc, r0:r0 + sub, :] = hn[:, cc * LANES:(cc + 1) * LANES]

    def permute(first, k0):
        def side_work(k):
            lo = first + 2 * (k - k0)
            for p in range(max(lo, first), min(lo + 2, phases)):
                for cc in range(d // LANES):
                    hnp_ref[p * per:(p + 1) * per, cc * LANES:(cc + 1) * LANES] = (
                        hnf_ref[cc, pl.ds(p, per, stride=phases), :].astype(BF16))
        return side_work

    @pl.when(j == 0)
    def _():
        token_rows(keep)

    def stored_rows(src_ref, side_work=None):
        for k, r0 in enumerate(starts):
            if side_work is not None:
                side_work(k)
            o_ref[r0:r0 + sub, :] = _dot(src_ref[r0:r0 + sub, :], w_ref[...]).astype(o_ref.dtype)

    assert n_token_tiles == 2 and phases == 2 * len(starts)

    @pl.when(j == 1)
    def _():
        stored_rows(hn_ref)

    @pl.when(j == 2)
    def _():
        build = permute(0, 0)
        build(0)
        for k, r0 in enumerate(starts):
            if k + 1 < len(starts):
                build(k + 1)
            o_ref[r0:r0 + sub, :] = _dot(hnp_ref[r0:r0 + sub, :], w_ref[...]).astype(o_ref.dtype)

    @pl.when(j > 2)
    def _():
        stored_rows(hnp_ref)


def _norm_proj_t(x2d, w_t, w_aux_t, *, tm, tn):
    t, d = x2d.shape
    n = w_t.shape[0]
    return pl.pallas_call(
        functools.partial(_norm_proj_body, kind="aux", phases=0, n_token_tiles=0),
        grid=(t // tm, n // tn),
        in_specs=[pl.BlockSpec((tm, d), lambda i, j: (i, 0)),
                  pl.BlockSpec((tn, d), lambda i, j: (j, 0)),
                  pl.BlockSpec((LANES, d), lambda i, j: (0, 0))],
        out_specs=[pl.BlockSpec((tm, tn), lambda i, j: (i, j)),
                   pl.BlockSpec((tm, LANES), lambda i, j: (i, 0))],
        out_shape=[jax.ShapeDtypeStruct((t, n), BF16), jax.ShapeDtypeStruct((t, LANES), F32)],
        scratch_shapes=[pltpu.VMEM((tm, d), BF16)],
        compiler_params=pltpu.CompilerParams(
            dimension_semantics=("parallel", "arbitrary"), vmem_limit_bytes=VMEM_LIMIT),
        name="norm_proj_aux",
    )(x2d, w_t, w_aux_t)


def _norm_proj_phased(x2d, w, *, tm, tn, phases, token_cols):
    t, d = x2d.shape
    n = w.shape[1]
    return pl.pallas_call(
        functools.partial(_norm_proj_body, kind="phased", phases=phases, n_token_tiles=token_cols // tn),
        grid=(t // tm, n // tn),
        in_specs=[pl.BlockSpec((tm, d), lambda i, j: (i, 0)),
                  pl.BlockSpec((d, tn), lambda i, j: (0, j))],
        out_specs=pl.BlockSpec((tm, tn), lambda i, j: (i, j)),
        out_shape=jax.ShapeDtypeStruct((t, n), BF16),
        scratch_shapes=[pltpu.VMEM((tm, d), BF16), pltpu.VMEM((tm, d), BF16),
                        pltpu.VMEM((d // LANES, tm, LANES), F32)],
        compiler_params=pltpu.CompilerParams(
            dimension_semantics=("parallel", "arbitrary"), vmem_limit_bytes=VMEM_LIMIT),
        name="norm_proj",
    )(x2d, w)


def _split3(a):
    hi = a.astype(BF16)
    r = a - hi.astype(F32)
    mid = r.astype(BF16)
    lo = (r - mid.astype(F32)).astype(BF16)
    return hi, mid, lo


def _expand_heads(a, e):
    hi, mid, lo = _split3(a)
    return _dot(hi, e) + _dot(mid, e) + _dot(lo, e)


def _ssd_body(z_ref, xs_ref, bc_ref, dt_ref, cw_ref, cb_ref, dtb_ref, alog_ref, dskip_ref, nrm_ref,
              e_ref, tri_ref, y_ref, state_ref, ubuf_ref):
    c = pl.program_id(1)
    L = SSD_CHUNK
    tail = 8

    @pl.when(c == 0)
    def _():
        state_ref[...] = jnp.zeros_like(state_ref)
        ubuf_ref[0:tail, :] = jnp.zeros((tail, SSD_CONV_DIM), F32)

    row = lax.broadcasted_iota(jnp.int32, (L, L), 0)
    col = lax.broadcasted_iota(jnp.int32, (L, L), 1)
    causal = col <= row
    lane = lax.broadcasted_iota(jnp.int32, (L, LANES), 1)
    low_half = lane < SSD_HEAD_DIM
    heads_per_group = SSD_HEADS // SSD_GROUPS
    gw = heads_per_group * SSD_HEAD_DIM

    def chunk(r):
        acts = []
        cw = cw_ref[...]
        for c0 in range(0, SSD_CONV_DIM, SSD_BC):
            cols = slice(c0, c0 + SSD_BC)
            src = xs_ref[r, cols] if c0 < SSD_INNER else bc_ref[r, :]
            u = src.astype(F32)
            ubuf_ref[tail:tail + L, cols] = u
            acc = cb_ref[:, cols] + cw[SSD_CONV - 1:SSD_CONV, cols] * u
            for k in range(SSD_CONV - 1):
                acc = acc + cw[k:k + 1, cols] * ubuf_ref[pl.ds(tail - (SSD_CONV - 1) + k, L), cols]
            ubuf_ref[0:tail, cols] = u[L - tail:L, :]
            acts.append(acc * _sigmoid(acc))
        act = jnp.concatenate(acts, axis=1)

        xs = act[:, :SSD_INNER]

        dtr = dt_ref[r, :] + dtb_ref[...]
        dt = jnp.maximum(dtr, 0.0) + jnp.log1p(jnp.exp(-jnp.abs(dtr)))
        a = -jnp.exp(alog_ref[...]) * math.log2(math.e)
        ac = dt * a
        tri = tri_ref[...]
        hi, mid, lo = _split3(ac)
        a_cum = _dot(tri, hi) + _dot(tri, mid) + _dot(tri, lo)
        a_last = a_cum[L - 1:L, :]
        dec_states = jnp.exp2(a_last - a_cum)
        exp_acum = jnp.exp2(a_cum)
        e = e_ref[...]
        dt_e = _expand_heads(dt, e)
        dec_e = _expand_heads(dec_states, e)
        ea_e = _expand_heads(exp_acum, e)
        a_cum_t = a_cum.T

        xc = xs * dt_e
        xc_b = xc.astype(BF16)
        xdec_b = (xc * dec_e).astype(BF16)

        for g in range(SSD_GROUPS):
            b_g = act[:, SSD_INNER + g * SSD_STATE:SSD_INNER + (g + 1) * SSD_STATE]
            c_g = act[:, SSD_INNER + (SSD_GROUPS + g) * SSD_STATE:
                      SSD_INNER + (SSD_GROUPS + g + 1) * SSD_STATE]
            b_gb = b_g.astype(BF16)
            c_gb = c_g.astype(BF16)
            scores = _nt_dot(c_gb, b_gb)
            gs = slice(g * gw, (g + 1) * gw)

            prev = state_ref[:, gs]
            y_off = _dot(c_gb, prev.astype(BF16))
            st_new = _dot(b_g.T.astype(BF16), xdec_b[:, gs])
            state_ref[:, gs] = prev * ea_e[L - 1:L, gs] + st_new

            pieces = []
            for pair in range(heads_per_group // 2):
                ms = []
                for hh in range(2):
                    h = g * heads_per_group + 2 * pair + hh
                    seg = a_cum[:, h:h + 1] - a_cum_t[h:h + 1, :]
                    lmat = jnp.exp2(jnp.where(causal, seg, NEG_BIG))
                    ms.append((scores * lmat).astype(BF16))
                m_pair = jnp.concatenate(ms, axis=1)
                cs = slice(g * gw + pair * LANES, g * gw + (pair + 1) * LANES)
                x_pair = xc_b[:, cs]
                zero = jnp.zeros_like(x_pair)
                rhs = jnp.concatenate([jnp.where(low_half, x_pair, zero),
                                       jnp.where(low_half, zero, x_pair)], axis=0)
                pieces.append(_dot(m_pair, rhs))
            y_diag = jnp.concatenate(pieces, axis=1)

            y = y_diag + y_off * ea_e[:, gs] + xs[:, gs] * dskip_ref[:, gs]
            zg = z_ref[r, gs].astype(F32)
            ug = y * (zg * _sigmoid(zg))
            ug = ug * lax.rsqrt(jnp.mean(ug * ug, axis=-1, keepdims=True) + NORM_EPS)
            y_ref[r, gs] = (ug * nrm_ref[:, gs]).astype(y_ref.dtype)

    for ci in range(xs_ref.shape[0] // L):
        chunk(slice(ci * L, (ci + 1) * L))


def _ssd_operands(proj, dt_raw, conv_w, conv_b, dt_bias, a_log, d_skip, ssd_norm, seq, R):
    L = SSD_CHUNK
    nc = seq // R

    def pad_heads(v):
        return jnp.pad(v.astype(F32), (0, LANES - SSD_HEADS)).reshape(1, LANES)

    head_of_chan = jnp.arange(SSD_INNER, dtype=jnp.int32) // SSD_HEAD_DIM
    expand = (jnp.arange(LANES, dtype=jnp.int32)[:, None] == head_of_chan[None, :]).astype(BF16)
    tri = (jnp.arange(L)[:, None] >= jnp.arange(L)[None, :]).astype(BF16)
    d_chan = jnp.repeat(d_skip.astype(F32), SSD_HEAD_DIM).reshape(1, SSD_INNER)

    def rows(b, c):
        return b * nc + c

    const = lambda b, c: (0, 0)
    in_specs = [
            pl.BlockSpec((R, SSD_INNER), lambda b, c: (rows(b, c), P0_Z // SSD_INNER)),
            pl.BlockSpec((R, SSD_INNER), lambda b, c: (rows(b, c), P0_X // SSD_INNER)),
            pl.BlockSpec((R, SSD_BC), lambda b, c: (rows(b, c), P0_BC // SSD_BC)),
            pl.BlockSpec((R, LANES), lambda b, c: (rows(b, c), 0)),
            pl.BlockSpec((SSD_CONV, SSD_CONV_DIM), const),
            pl.BlockSpec((1, SSD_CONV_DIM), const),
            pl.BlockSpec((1, LANES), const),
            pl.BlockSpec((1, LANES), const),
            pl.BlockSpec((1, SSD_INNER), const),
            pl.BlockSpec((1, SSD_INNER), const),
            pl.BlockSpec((LANES, SSD_INNER), const),
            pl.BlockSpec((L, L), const),
    ]
    args = (proj, proj, proj, dt_raw, conv_w.astype(F32), conv_b.reshape(1, -1).astype(F32),
            pad_heads(dt_bias), pad_heads(a_log), d_chan, ssd_norm.reshape(1, -1).astype(F32), expand, tri)
    return in_specs, args


def _moba_body(q_ref, k_ref, v_ref, avg_ref, hot_ref, eye_ref, o_ref, *, nb, q_tile):
    blk = MOBA_BLOCK
    n_dense = MOBA_TOPK + 1

    kmean = _dot(avg_ref[...], k_ref[...]).astype(BF16)
    jrow = lax.broadcasted_iota(jnp.int32, (nb, blk), 0)
    row = lax.broadcasted_iota(jnp.int32, (q_tile, blk), 0)
    col = lax.broadcasted_iota(jnp.int32, (q_tile, blk), 1)

    def query_block(i):
        q_i = q_ref[i * blk:(i + 1) * blk, :]
        if i < n_dense:
            return q_i, None
        gate_t = _nt_dot(kmean, q_i)[0:nb, :]
        valid = jrow < i
        gm = jnp.where(valid, gate_t, -jnp.inf)
        rank = jnp.zeros((nb, blk), F32)
        for r in range(1, nb):
            gr = pltpu.roll(gm, r, axis=0)
            lower = ((jrow - r) & (nb - 1)) < jrow
            rank = rank + jnp.where(gr > gm, 1.0, 0.0) + jnp.where((gr == gm) & lower, 1.0, 0.0)
        keep = valid & (rank < float(MOBA_TOPK))
        bias_t = jnp.where(keep, 0.0, NEG_BIG)
        bias_t = jnp.concatenate([bias_t, jnp.zeros((LANES - nb, blk), F32)], axis=0).astype(BF16)
        bias = _nt_dot(eye_ref[...], bias_t).astype(BF16)
        return q_i, jnp.concatenate([q_i, bias], axis=1)

    def scores(i, h, q_i, q_aug):
        rs = slice(h * q_tile, (h + 1) * q_tile)
        s_own = _nt_dot(q_i[rs, :], k_ref[i * blk:(i + 1) * blk, :])
        s_own = jnp.where(col <= row + h * q_tile, s_own, NEG_BIG)
        if i == 0:
            return s_own, None
        if q_aug is None:
            return s_own, _nt_dot(q_i[rs, :], k_ref[0:i * blk, :])
        k_past = jnp.concatenate([k_ref[0:i * blk, :], hot_ref[0:i * blk, :]], axis=1)
        return s_own, _nt_dot(q_aug[rs, :], k_past)

    def finish(i, h, s_own, s_past):
        m = jnp.max(s_own, axis=-1, keepdims=True)
        if s_past is None:
            p = jnp.exp2(s_own - m)
        else:
            m = jnp.maximum(m, jnp.max(s_past, axis=-1, keepdims=True))
            p = jnp.concatenate([jnp.exp2(s_past - m), jnp.exp2(s_own - m)], axis=1)
        vb = v_ref[0:(i + 1) * blk, :]
        pv = _dot(p.astype(BF16), jnp.concatenate([vb, jnp.ones_like(vb)], axis=1))
        d = vb.shape[1]
        o_ref[i * blk + h * q_tile:i * blk + (h + 1) * q_tile, :] = (pv[:, :d] / pv[:, d:]).astype(o_ref.dtype)

    tiles = [(i, h) for i in range(nb) for h in range(blk // q_tile)]
    qcache = {}
    pending = None
    for i, h in tiles:
        if i not in qcache:
            qcache = {i: query_block(i)}
        s_next = scores(i, h, *qcache[i])
        if pending is not None:
            finish(*pending)
        pending = (i, h) + s_next
    finish(*pending)


def _moba_operands(proj, seq):
    blk = MOBA_BLOCK
    nb = seq // blk
    d = MOBA_HEAD_DIM
    blk_of_key = jnp.arange(seq, dtype=jnp.int32) // blk
    avg = (jnp.arange(2 * nb, dtype=jnp.int32)[:, None] == blk_of_key[None, :]).astype(BF16) / blk
    hot = (blk_of_key[:, None] == jnp.arange(LANES, dtype=jnp.int32)[None, :]).astype(BF16)
    eye = jnp.eye(blk, dtype=BF16)

    def head_block(base):
        return pl.BlockSpec((seq, d), lambda b, h: (b, base // d + h))

    const = lambda b, h: (0, 0)
    in_specs = [head_block(P0_Q), head_block(P0_K), head_block(P0_V),
                pl.BlockSpec((2 * nb, seq), const), pl.BlockSpec((seq, LANES), const),
                pl.BlockSpec((blk, blk), const)]
    return in_specs, (proj, proj, proj, avg, hot, eye)


def _ssd(proj, dt_raw, conv_w, conv_b, dt_bias, a_log, d_skip, ssd_norm, bsz, seq, *, chunks_per_step=SSD_CHUNKS_PER_STEP):
    t = bsz * seq
    R = chunks_per_step * SSD_CHUNK
    in_specs, args = _ssd_operands(proj, dt_raw, conv_w, conv_b, dt_bias, a_log, d_skip, ssd_norm, seq, R)
    nc = seq // R
    return pl.pallas_call(
        _ssd_body,
        grid=(bsz, nc),
        in_specs=in_specs,
        out_specs=pl.BlockSpec((R, SSD_INNER), lambda b, c: (b * nc + c, 0)),
        out_shape=jax.ShapeDtypeStruct((t, SSD_INNER), BF16),
        scratch_shapes=[pltpu.VMEM((SSD_STATE, SSD_INNER), F32),
                        pltpu.VMEM((8 + SSD_CHUNK, SSD_CONV_DIM), F32)],
        compiler_params=pltpu.CompilerParams(
            dimension_semantics=("parallel", "arbitrary"), vmem_limit_bytes=VMEM_LIMIT),
        name="ssd_mixer",
    )(*args)


def _moba(proj, bsz, seq):
    t = bsz * seq
    in_specs, args = _moba_operands(proj, seq)
    return pl.pallas_call(
        functools.partial(_moba_body, nb=seq // MOBA_BLOCK, q_tile=MOBA_Q_TILE),
        grid=(bsz, MOBA_HEADS),
        in_specs=in_specs,
        out_specs=pl.BlockSpec((seq, MOBA_HEAD_DIM), lambda b, h: (b, h)),
        out_shape=jax.ShapeDtypeStruct((t, MOBA_INNER), BF16),
        compiler_params=pltpu.CompilerParams(
            dimension_semantics=("parallel", "parallel"), vmem_limit_bytes=VMEM_LIMIT),
        name="moba_attn",
    )(*args)


def _out_proj_body(*refs, n_plain, final_norm):
    plain = refs[:n_plain]
    a_ref, g_ref, x_ref, w_ref = refs[n_plain:n_plain + 4]
    o_ref = refs[-1]
    g = g_ref[...].astype(F32)
    gated = (a_ref[...].astype(F32) * (g * _sigmoid(g))).astype(BF16)
    k0 = 0
    y = x_ref[...]
    for r in plain:
        kw = r.shape[1]
        y = y + _dot(r[...], w_ref[k0:k0 + kw, :])
        k0 += kw
    y = y + _dot(gated, w_ref[k0:k0 + gated.shape[1], :])
    if final_norm:
        fn_ref = refs[-2]
        y = y * lax.rsqrt(jnp.mean(y * y, axis=-1, keepdims=True) + NORM_EPS) * fn_ref[...]
    o_ref[...] = y


def _out_proj(plain, gated, gate_src, gate_col, x2d, w, final_gain=None, *, tm=OUT_PROJ_TM):
    t, d = x2d.shape
    kg = gated.shape[1]
    in_specs = [pl.BlockSpec((tm, a.shape[1]), lambda i: (i, 0)) for a in plain]
    in_specs += [pl.BlockSpec((tm, kg), lambda i: (i, 0)),
                 pl.BlockSpec((tm, kg), lambda i: (i, gate_col // kg)),
                 pl.BlockSpec((tm, d), lambda i: (i, 0)),
                 pl.BlockSpec(w.shape, lambda i: (0, 0))]
    args = list(plain) + [gated, gate_src, x2d, w]
    if final_gain is not None:
        in_specs.append(pl.BlockSpec((1, d), lambda i: (0, 0)))
        args.append(final_gain.reshape(1, d).astype(F32))
    return pl.pallas_call(
        functools.partial(_out_proj_body, n_plain=len(plain), final_norm=final_gain is not None),
        grid=(t // tm,),
        in_specs=in_specs,
        out_specs=pl.BlockSpec((tm, d), lambda i: (i, 0)),
        out_shape=jax.ShapeDtypeStruct((t, d), F32),
        compiler_params=pltpu.CompilerParams(
            dimension_semantics=("parallel",), vmem_limit_bytes=VMEM_LIMIT),
        name="out_proj_final" if final_gain is not None else "out_proj",
    )(*args)


DIL_PHASES = 8


def _dil_tiles(seq):
    ph, qb = DIL_PHASES, DIL_QBLOCK
    per = seq // ph
    tiles = []
    rows = qb // ph
    for n in range(seq // qb):
        qs = [(p * per + rows * n, rows) for p in range(ph)]
        if n == 0:
            ks, mask = [(p * per, rows) for p in range(ph)], "a_first"
        else:
            ks, mask = [(p * per + rows * (n - 1), 2 * rows) for p in range(ph)], "a"
        tiles.append((qs, ks, mask, "init"))
    seg = ph // 4
    rows = qb // seg
    for p4 in range(4):
        for n in range(seq // 4 // qb):
            qs = [((p4 + 4 * j) * per + rows * n, rows) for j in range(seg)]
            if n == 0:
                ks, mask = [((p4 + 4 * j) * per, rows) for j in range(seg)], "b_first"
            else:
                ks, mask = [((p4 + 4 * j) * per + rows * (n - 1), 2 * rows) for j in range(seg)], "b"
            tiles.append((qs, ks, mask, "merge"))
    for p in range(ph):
        for h in range(per // qb):
            qs = [(p * per + qb * h, qb)]
            ks = [(p * per, qb * (h + 1))]
            tiles.append((qs, ks, "c%d" % h, "final"))
    return tiles


def _dil_masks(seq):
    ph, qb = DIL_PHASES, DIL_QBLOCK
    per = seq // ph

    def pos(n_rows, seg_rows, step):
        i = np.arange(n_rows)
        return step * (i % seg_rows) + i // seg_rows

    def band(qpos, kpos, span):
        dist = qpos[:, None] - kpos[None, :]
        return np.where((dist >= 0) & (dist <= span), 0.0, NEG_BIG).astype(np.float32)

    masks = {}
    for name, seg in (("a", ph), ("b", ph // 4)):
        rows = qb // seg
        masks[name] = band(qb + pos(qb, rows, seg), pos(2 * qb, 2 * rows, seg), qb)
        masks[name + "_first"] = band(pos(qb, rows, seg), pos(qb, rows, seg), qb)
    for h in range(per // qb):
        lq = qb * h + np.arange(qb)
        lk = np.arange(qb * (h + 1))
        same = (lq[:, None] - lk[None, :]) % 2 == 0
        masks["c%d" % h] = np.where(same & (lk[None, :] <= lq[:, None]), 0.0, NEG_BIG).astype(np.float32)
    return masks


def _dil_body(*refs, seq, mask_names):
    q_ref, k_ref, v_ref = refs[:3]
    mask_refs = dict(zip(mask_names, refs[3:3 + len(mask_names)]))
    o_ref, onat_ref = refs[3 + len(mask_names):][:2]
    state_a, state_b = refs[-6:-3], refs[-3:]
    state_in = {"merge": state_a, "final": state_b}
    state_out = {"init": state_a, "merge": state_b}
    qb = DIL_QBLOCK
    d = DIL_HEAD_DIM

    def rows_of(ref, slices):
        parts = [ref[s:s + n, :] for s, n in slices]
        return parts[0] if len(parts) == 1 else jnp.concatenate(parts, axis=0)

    def put_rows(ref, slices, val):
        r0 = 0
        for s, n in slices:
            ref[s:s + n, :] = val[r0:r0 + n, :]
            r0 += n

    def scores(tile):
        qs, ks, mask, _ = tile
        return _nt_dot(rows_of(q_ref, qs), rows_of(k_ref, ks)) + mask_refs[mask][...]

    def probs(tile, s):
        qs, _, _, stage = tile
        m_new = jnp.broadcast_to(jnp.max(s, axis=-1, keepdims=True), (s.shape[0], d))
        m_old = None
        if stage != "init":
            m_old = rows_of(state_in[stage][1], qs)
            m_new = jnp.maximum(m_old, m_new)
        p = jnp.concatenate([jnp.exp2(s[:, c0:c0 + d] - m_new) for c0 in range(0, s.shape[1], d)], axis=1)
        return p.astype(BF16), m_new, m_old

    def accumulate(tile, p, m_new, m_old):
        qs, ks, _, stage = tile
        vb = rows_of(v_ref, ks)
        pv = _dot(p, jnp.concatenate([vb, jnp.ones_like(vb)], axis=1))
        acc, l = pv[:, :d], pv[:, d:]
        if stage != "init":
            alpha = jnp.exp2(m_old - m_new)
            acc_ref, _, l_ref = state_in[stage]
            acc = alpha * rows_of(acc_ref, qs) + acc
            l = alpha * rows_of(l_ref, qs) + l
        if stage == "final":
            (s0, _), = qs
            p_idx, l0 = divmod(s0, seq // DIL_PHASES)
            onat_ref[pl.ds(DIL_PHASES * l0 + p_idx, qb, stride=DIL_PHASES), :] = acc / l
        else:
            acc_out, m_out, l_out = state_out[stage]
            put_rows(acc_out, qs, acc)
            put_rows(l_out, qs, l)
            put_rows(m_out, qs, m_new)

    tiles = _dil_tiles(seq)

    def owned(tile):
        return {r for s0, n in tile[0] for r in range(s0, s0 + n)}

    for ta, tb in zip(tiles, tiles[1:]):
        assert not owned(ta) & owned(tb)
    scored, ready = None, None
    for tile in tiles + [None, None]:
        s_new = None if tile is None else (tile, scores(tile))
        p_new = None if scored is None else (scored[0],) + probs(*scored)
        if ready is not None:
            accumulate(*ready)
        scored, ready = s_new, p_new
    o_ref[...] = onat_ref[...].astype(o_ref.dtype)


def _dilated(proj, bsz, seq):
    t = bsz * seq
    d = DIL_HEAD_DIM
    masks = _dil_masks(seq)
    names = tuple(sorted(masks))

    def head_block(base):
        return pl.BlockSpec((seq, d), lambda b, h: (b, base // d + h))

    return pl.pallas_call(
        functools.partial(_dil_body, seq=seq, mask_names=names),
        grid=(bsz, DIL_HEADS),
        in_specs=[head_block(DIL_INNER), head_block(2 * DIL_INNER), head_block(3 * DIL_INNER)]
        + [pl.BlockSpec(masks[n].shape, lambda b, h: (0, 0)) for n in names],
        out_specs=pl.BlockSpec((seq, d), lambda b, h: (b, h)),
        out_shape=jax.ShapeDtypeStruct((t, DIL_INNER), BF16),
        scratch_shapes=[pltpu.VMEM((seq, d), F32)] * 7,
        compiler_params=pltpu.CompilerParams(
            dimension_semantics=("parallel", "parallel"), vmem_limit_bytes=VMEM_LIMIT),
        name="dilated_attn",
    )(proj, proj, proj, *[jnp.asarray(masks[n]) for n in names])


def kernel(x, even_norm, even_w_in, ssd_conv_w, ssd_conv_b, ssd_dt_bias, ssd_a_log, ssd_d, ssd_norm,
           even_w_out, odd_norm, odd_w_in, odd_w_out, final_norm):
    bsz, seq, d = x.shape
    t = bsz * seq
    x2d = x.reshape(t, d)

    w = even_w_in[0]
    z_end = SSD_INNER
    xbc_end = z_end + SSD_CONV_DIM
    dt_end = xbc_end + SSD_HEADS
    q_end = dt_end + MOBA_INNER
    v_end = q_end + 2 * MOBA_INNER
    x_end = z_end + SSD_INNER
    moba_scale = MOBA_HEAD_DIM ** -0.5 * math.log2(math.e)
    g0 = even_norm[0].astype(F32)[None, :]
    w_t = w.T

    def rows(a, b, scale=1.0):
        return (w_t[a:b] * (g0 * scale)).astype(BF16)

    w_main = jnp.concatenate([rows(0, x_end), rows(v_end, w.shape[1]), rows(dt_end, q_end, moba_scale),
                              rows(q_end, v_end), rows(x_end, xbc_end)], axis=0)
    w_dt = jnp.pad(rows(xbc_end, dt_end), ((0, LANES - SSD_HEADS), (0, 0)))
    proj0, dt_raw = _norm_proj_t(x2d, w_main, w_dt, tm=seq // 2, tn=P0_N // PROJ0_COL_TILES)
    y_ssd = _ssd(proj0, dt_raw, ssd_conv_w[0], ssd_conv_b[0], ssd_dt_bias[0], ssd_a_log[0],
                 ssd_d[0], ssd_norm[0], bsz, seq)
    o_moba = _moba(proj0, bsz, seq)
    x1 = _out_proj([y_ssd], o_moba, proj0, P0_G, x2d, even_w_out[0].astype(BF16))

    w1 = odd_w_in[0]
    dil_scale = DIL_HEAD_DIM ** -0.5 * math.log2(math.e)
    g1 = odd_norm[0].astype(F32)[:, None]
    w1_b = jnp.concatenate([(w1[:, 3 * DIL_INNER:] * g1).astype(BF16),
                            (w1[:, :DIL_INNER] * (g1 * dil_scale)).astype(BF16),
                            (w1[:, DIL_INNER:3 * DIL_INNER] * g1).astype(BF16)], axis=1)
    proj1 = _norm_proj_phased(x1, w1_b, tm=seq, tn=PROJ1_TN, phases=DIL_PHASES, token_cols=DIL_INNER)
    o_dil = _dilated(proj1, bsz, seq)
    out = _out_proj([], o_dil, proj1, 0, x1, odd_w_out[0].astype(BF16), final_norm)
    return out.reshape(bsz, seq, d)
```

```python
import functools
import math

import jax
import jax.numpy as jnp
import numpy as np
from jax import lax
from jax.experimental import pallas as pl
from jax.experimental.pallas import tpu as pltpu

F32 = jnp.float32
BF16 = jnp.bfloat16

NORM_EPS = 1e-5
D_MODEL = 1024

SSD_HEADS = 16
SSD_HEAD_DIM = 64
SSD_INNER = SSD_HEADS * SSD_HEAD_DIM
SSD_GROUPS = 2
SSD_STATE = 128
SSD_CONV = 4
SSD_CHUNK = 128
SSD_BC = 2 * SSD_GROUPS * SSD_STATE
SSD_CONV_DIM = SSD_INNER + SSD_BC

MOBA_HEADS = 8
MOBA_HEAD_DIM = 128
MOBA_INNER = MOBA_HEADS * MOBA_HEAD_DIM
MOBA_BLOCK = 256
MOBA_TOPK = 3

DIL_HEADS = 16
DIL_HEAD_DIM = 128
DIL_INNER = DIL_HEADS * DIL_HEAD_DIM
DIL_QBLOCK = 128
DIL_PATTERNS = ((128, 1), (512, 4), (2048, 16))

LANES = 128
MXU_WIDTH = 256
NEG_BIG = -1e30
VMEM_LIMIT = 56 * 1024 * 1024

PROJ_ROW_SUB = 512
PROJ0_COL_TILES = 2
PROJ1_TN = 4 * MXU_WIDTH
OUT_PROJ_TM = 1024
SSD_CHUNKS_PER_STEP = 4
MOBA_Q_TILE = 256

P0_Z = 0
P0_X = P0_Z + SSD_INNER
P0_G = P0_X + SSD_INNER
P0_Q = P0_G + MOBA_INNER
P0_K = P0_Q + MOBA_INNER
P0_V = P0_K + MOBA_INNER
P0_BC = P0_V + MOBA_INNER
P0_N = P0_BC + SSD_BC


def _nt_dot(a, b):
    return lax.dot_general(a, b, (((1,), (1,)), ((), ())), preferred_element_type=F32)


def _dot(a, b):
    return jnp.dot(a, b, preferred_element_type=F32)


def _sigmoid(x):
    return 1.0 / (1.0 + jnp.exp2(x * -math.log2(math.e)))


def _norm_proj_body(*refs, kind, phases, n_token_tiles):
    if kind == "aux":
        x_ref, w_ref, waux_ref, o_ref, oaux_ref, hn_ref = refs
    else:
        x_ref, w_ref, o_ref, rows_ref, hnf_ref = refs
    j = pl.program_id(1)
    tm = x_ref.shape[0]

    if kind == "aux":
        sub = min(tm, PROJ_ROW_SUB)

        def normed(r0):
            x = x_ref[r0:r0 + sub, :]
            ms = jnp.mean(x * x, axis=-1, keepdims=True)
            return (x * lax.rsqrt(ms + NORM_EPS)).astype(BF16)

        starts = list(range(0, tm, sub))
        nxt = normed(starts[0])
        for k, r0 in enumerate(starts):
            cur = nxt
            if k + 1 < len(starts):
                nxt = normed(starts[k + 1])
            hn_ref[r0:r0 + sub, :] = cur
            o_ref[r0:r0 + sub, :] = _nt_dot(cur, w_ref[...]).astype(o_ref.dtype)

        @pl.when(j == 0)
        def _():
            oaux_ref[...] = _nt_dot(hn_ref[...], waux_ref[...])

        return

    d = x_ref.shape[1]
    sub = min(tm, PROJ_ROW_SUB)
    starts = list(range(0, tm, sub))
    per = tm // phases
    assert n_token_tiles >= 2 and phases % len(starts) == 0

    def normed(r0):
        x = x_ref[r0:r0 + sub, :]
        ms = jnp.mean(x * x, axis=-1, keepdims=True)
        return x * lax.rsqrt(ms + NORM_EPS)

    def token_rows(side_work):
        nxt = normed(starts[0])
        for k, r0 in enumerate(starts):
            cur = nxt
            if k + 1 < len(starts):
                nxt = normed(starts[k + 1])
            side_work(k, r0, cur)
            o_ref[r0:r0 + sub, :] = _dot(cur.astype(BF16), w_ref[...]).astype(o_ref.dtype)

    def keep(k, r0, hn):
        rows_ref[0, r0:r0 + sub, :] = hn.astype(BF16)
        for cc in range(d // LANES):
            hnf_ref[cc, r0:r0 + sub, :] = hn[:, cc * LANES:(cc + 1) * LANES]

    def build(k):
        for p in range(2 * k, 2 * k + 2):
            for cc in range(d // LANES):
                rows_ref[1, p * per:(p + 1) * per, cc * LANES:(cc + 1) * LANES] = (
                    hnf_ref[cc, pl.ds(p, per, stride=phases), :].astype(BF16))

    assert n_token_tiles == 2 and phases == 2 * len(starts)

    @pl.when(j == 0)
    def _():
        token_rows(keep)

    @pl.when(j == n_token_tiles)
    def _():
        build(0)
        for k, r0 in enumerate(starts):
            if k + 1 < len(starts):
                build(k + 1)
            o_ref[r0:r0 + sub, :] = _dot(rows_ref[1, r0:r0 + sub, :], w_ref[...]).astype(o_ref.dtype)

    @pl.when((j != 0) & (j != n_token_tiles))
    def _():
        order = jnp.where(j < n_token_tiles, 0, 1)
        for r0 in starts:
            o_ref[r0:r0 + sub, :] = _dot(rows_ref[order, r0:r0 + sub, :], w_ref[...]).astype(o_ref.dtype)


def _norm_proj_t(x2d, w_t, w_aux_t, *, tm, tn):
    t, d = x2d.shape
    n = w_t.shape[0]
    return pl.pallas_call(
        functools.partial(_norm_proj_body, kind="aux", phases=0, n_token_tiles=0),
        grid=(t // tm, n // tn),
        in_specs=[pl.BlockSpec((tm, d), lambda i, j: (i, 0)),
                  pl.BlockSpec((tn, d), lambda i, j: (j, 0)),
                  pl.BlockSpec((LANES, d), lambda i, j: (0, 0))],
        out_specs=[pl.BlockSpec((tm, tn), lambda i, j: (i, j)),
                   pl.BlockSpec((tm, LANES), lambda i, j: (i, 0))],
        out_shape=[jax.ShapeDtypeStruct((t, n), BF16), jax.ShapeDtypeStruct((t, LANES), F32)],
        scratch_shapes=[pltpu.VMEM((tm, d), BF16)],
        compiler_params=pltpu.CompilerParams(
            dimension_semantics=("parallel", "arbitrary"), vmem_limit_bytes=VMEM_LIMIT),
        name="norm_proj_aux",
    )(x2d, w_t, w_aux_t)


def _norm_proj_phased(x2d, w, *, tm, tn, phases, token_cols):
    t, d = x2d.shape
    n = w.shape[1]
    return pl.pallas_call(
        functools.partial(_norm_proj_body, kind="phased", phases=phases, n_token_tiles=token_cols // tn),
        grid=(t // tm, n // tn),
        in_specs=[pl.BlockSpec((tm, d), lambda i, j: (i, 0)),
                  pl.BlockSpec((d, tn), lambda i, j: (0, j))],
        out_specs=pl.BlockSpec((tm, tn), lambda i, j: (i, j)),
        out_shape=jax.ShapeDtypeStruct((t, n), BF16),
        scratch_shapes=[pltpu.VMEM((2, tm, d), BF16), pltpu.VMEM((d // LANES, tm, LANES), F32)],
        compiler_params=pltpu.CompilerParams(
            dimension_semantics=("parallel", "arbitrary"), vmem_limit_bytes=VMEM_LIMIT),
        name="norm_proj",
    )(x2d, w)


def _split3(a):
    hi = a.astype(BF16)
    r = a - hi.astype(F32)
    mid = r.astype(BF16)
    lo = (r - mid.astype(F32)).astype(BF16)
    return hi, mid, lo


def _expand_heads(a, e):
    hi, mid, lo = _split3(a)
    return _dot(hi, e) + _dot(mid, e) + _dot(lo, e)


def _ssd_body(z_ref, xs_ref, bc_ref, dt_ref, cw_ref, cb_ref, dtb_ref, alog_ref, dskip_ref, nrm_ref,
              e_ref, tri_ref, y_ref, state_ref, ubuf_ref):
    c = pl.program_id(1)
    L = SSD_CHUNK
    tail = 8

    @pl.when(c == 0)
    def _():
        state_ref[...] = jnp.zeros_like(state_ref)
        ubuf_ref[0:tail, :] = jnp.zeros((tail, SSD_CONV_DIM), F32)

    row = lax.broadcasted_iota(jnp.int32, (L, L), 0)
    col = lax.broadcasted_iota(jnp.int32, (L, L), 1)
    causal = col <= row
    lane = lax.broadcasted_iota(jnp.int32, (L, LANES), 1)
    low_half = lane < SSD_HEAD_DIM
    heads_per_group = SSD_HEADS // SSD_GROUPS
    gw = heads_per_group * SSD_HEAD_DIM

    def chunk(r):
        acts = []
        cw = cw_ref[...]
        for c0 in range(0, SSD_CONV_DIM, SSD_BC):
            cols = slice(c0, c0 + SSD_BC)
            src = xs_ref[r, cols] if c0 < SSD_INNER else bc_ref[r, :]
            u = src.astype(F32)
            ubuf_ref[tail:tail + L, cols] = u
            acc = cb_ref[:, cols] + cw[SSD_CONV - 1:SSD_CONV, cols] * u
            for k in range(SSD_CONV - 1):
                acc = acc + cw[k:k + 1, cols] * ubuf_ref[pl.ds(tail - (SSD_CONV - 1) + k, L), cols]
            ubuf_ref[0:tail, cols] = u[L - tail:L, :]
            acts.append(acc * _sigmoid(acc))
        act = jnp.concatenate(acts, axis=1)

        xs = act[:, :SSD_INNER]

        dtr = dt_ref[r, :] + dtb_ref[...]
        dt = jnp.maximum(dtr, 0.0) + jnp.log1p(jnp.exp(-jnp.abs(dtr)))
        a = -jnp.exp(alog_ref[...]) * math.log2(math.e)
        ac = dt * a
        tri = tri_ref[...]
        hi, mid, lo = _split3(ac)
        a_cum = _dot(tri, hi) + _dot(tri, mid) + _dot(tri, lo)
        a_last = a_cum[L - 1:L, :]
        dec_states = jnp.exp2(a_last - a_cum)
        exp_acum = jnp.exp2(a_cum)
        e = e_ref[...]
        dt_e = _expand_heads(dt, e)
        dec_e = _expand_heads(dec_states, e)
        ea_e = _expand_heads(exp_acum, e)
        a_cum_t = a_cum.T

        xc = xs * dt_e
        xc_b = xc.astype(BF16)
        xdec_b = (xc * dec_e).astype(BF16)

        for g in range(SSD_GROUPS):
            b_g = act[:, SSD_INNER + g * SSD_STATE:SSD_INNER + (g + 1) * SSD_STATE]
            c_g = act[:, SSD_INNER + (SSD_GROUPS + g) * SSD_STATE:
                      SSD_INNER + (SSD_GROUPS + g + 1) * SSD_STATE]
            b_gb = b_g.astype(BF16)
            c_gb = c_g.astype(BF16)
            scores = _nt_dot(c_gb, b_gb)
            gs = slice(g * gw, (g + 1) * gw)

            prev = state_ref[:, gs]
            y_off = _dot(c_gb, prev.astype(BF16))
            st_new = _dot(b_g.T.astype(BF16), xdec_b[:, gs])
            state_ref[:, gs] = prev * ea_e[L - 1:L, gs] + st_new

            pieces = []
            for pair in range(heads_per_group // 2):
                ms = []
                for hh in range(2):
                    h = g * heads_per_group + 2 * pair + hh
                    seg = a_cum[:, h:h + 1] - a_cum_t[h:h + 1, :]
                    lmat = jnp.exp2(jnp.where(causal, seg, NEG_BIG))
                    ms.append((scores * lmat).astype(BF16))
                m_pair = jnp.concatenate(ms, axis=1)
                cs = slice(g * gw + pair * LANES, g * gw + (pair + 1) * LANES)
                x_pair = xc_b[:, cs]
                zero = jnp.zeros_like(x_pair)
                rhs = jnp.concatenate([jnp.where(low_half, x_pair, zero),
                                       jnp.where(low_half, zero, x_pair)], axis=0)
                pieces.append(_dot(m_pair, rhs))
            y_diag = jnp.concatenate(pieces, axis=1)

            y = y_diag + y_off * ea_e[:, gs] + xs[:, gs] * dskip_ref[:, gs]
            zg = z_ref[r, gs].astype(F32)
            ug = y * (zg * _sigmoid(zg))
            ug = ug * lax.rsqrt(jnp.mean(ug * ug, axis=-1, keepdims=True) + NORM_EPS)
            y_ref[r, gs] = (ug * nrm_ref[:, gs]).astype(y_ref.dtype)

    for ci in range(xs_ref.shape[0] // L):
        chunk(slice(ci * L, (ci + 1) * L))


def _ssd_operands(proj, dt_raw, conv_w, conv_b, dt_bias, a_log, d_skip, ssd_norm, seq, R):
    L = SSD_CHUNK
    nc = seq // R

    def pad_heads(v):
        return jnp.pad(v.astype(F32), (0, LANES - SSD_HEADS)).reshape(1, LANES)

    head_of_chan = jnp.arange(SSD_INNER, dtype=jnp.int32) // SSD_HEAD_DIM
    expand = (jnp.arange(LANES, dtype=jnp.int32)[:, None] == head_of_chan[None, :]).astype(BF16)
    tri = (jnp.arange(L)[:, None] >= jnp.arange(L)[None, :]).astype(BF16)
    d_chan = jnp.repeat(d_skip.astype(F32), SSD_HEAD_DIM).reshape(1, SSD_INNER)

    def rows(b, c):
        return b * nc + c

    const = lambda b, c: (0, 0)
    in_specs = [
            pl.BlockSpec((R, SSD_INNER), lambda b, c: (rows(b, c), P0_Z // SSD_INNER)),
            pl.BlockSpec((R, SSD_INNER), lambda b, c: (rows(b, c), P0_X // SSD_INNER)),
            pl.BlockSpec((R, SSD_BC), lambda b, c: (rows(b, c), P0_BC // SSD_BC)),
            pl.BlockSpec((R, LANES), lambda b, c: (rows(b, c), 0)),
            pl.BlockSpec((SSD_CONV, SSD_CONV_DIM), const),
            pl.BlockSpec((1, SSD_CONV_DIM), const),
            pl.BlockSpec((1, LANES), const),
            pl.BlockSpec((1, LANES), const),
            pl.BlockSpec((1, SSD_INNER), const),
            pl.BlockSpec((1, SSD_INNER), const),
            pl.BlockSpec((LANES, SSD_INNER), const),
            pl.BlockSpec((L, L), const),
    ]
    args = (proj, proj, proj, dt_raw, conv_w.astype(F32), conv_b.reshape(1, -1).astype(F32),
            pad_heads(dt_bias), pad_heads(a_log), d_chan, ssd_norm.reshape(1, -1).astype(F32), expand, tri)
    return in_specs, args


def _moba_body(q_ref, k_ref, v_ref, avg_ref, hot_ref, eye_ref, o_ref, *, nb, q_tile):
    blk = MOBA_BLOCK
    n_dense = MOBA_TOPK + 1

    kmean = _dot(avg_ref[...], k_ref[...]).astype(BF16)
    jrow = lax.broadcasted_iota(jnp.int32, (nb, blk), 0)
    row = lax.broadcasted_iota(jnp.int32, (q_tile, blk), 0)
    col = lax.broadcasted_iota(jnp.int32, (q_tile, blk), 1)

    def query_block(i):
        q_i = q_ref[i * blk:(i + 1) * blk, :]
        if i < n_dense:
            return q_i, None
        gate_t = _nt_dot(kmean, q_i)[0:nb, :]
        valid = jrow < i
        gm = jnp.where(valid, gate_t, -jnp.inf)
        rank = jnp.zeros((nb, blk), F32)
        for r in range(1, nb):
            gr = pltpu.roll(gm, r, axis=0)
            lower = ((jrow - r) & (nb - 1)) < jrow
            rank = rank + jnp.where(gr > gm, 1.0, 0.0) + jnp.where((gr == gm) & lower, 1.0, 0.0)
        keep = valid & (rank < float(MOBA_TOPK))
        bias_t = jnp.where(keep, 0.0, NEG_BIG)
        bias_t = jnp.concatenate([bias_t, jnp.zeros((LANES - nb, blk), F32)], axis=0).astype(BF16)
        bias = _nt_dot(eye_ref[...], bias_t).astype(BF16)
        return q_i, jnp.concatenate([q_i, bias], axis=1)

    def scores(i, h, q_i, q_aug):
        rs = slice(h * q_tile, (h + 1) * q_tile)
        s_own = _nt_dot(q_i[rs, :], k_ref[i * blk:(i + 1) * blk, :])
        s_own = jnp.where(col <= row + h * q_tile, s_own, NEG_BIG)
        if i == 0:
            return s_own, None
        if q_aug is None:
            return s_own, _nt_dot(q_i[rs, :], k_ref[0:i * blk, :])
        k_past = jnp.concatenate([k_ref[0:i * blk, :], hot_ref[0:i * blk, :]], axis=1)
        return s_own, _nt_dot(q_aug[rs, :], k_past)

    def finish(i, h, s_own, s_past):
        m = jnp.max(s_own, axis=-1, keepdims=True)
        if s_past is None:
            p = jnp.exp2(s_own - m)
        else:
            m = jnp.maximum(m, jnp.max(s_past, axis=-1, keepdims=True))
            p = jnp.concatenate([jnp.exp2(s_past - m), jnp.exp2(s_own - m)], axis=1)
        vb = v_ref[0:(i + 1) * blk, :]
        pv = _dot(p.astype(BF16), jnp.concatenate([vb, jnp.ones_like(vb)], axis=1))
        d = vb.shape[1]
        o_ref[i * blk + h * q_tile:i * blk + (h + 1) * q_tile, :] = (pv[:, :d] / pv[:, d:]).astype(o_ref.dtype)

    tiles = [(i, h) for i in range(nb) for h in range(blk // q_tile)]
    qcache = {}
    pending = None
    for i, h in tiles:
        if i not in qcache:
            qcache = {i: query_block(i)}
        s_next = scores(i, h, *qcache[i])
        if pending is not None:
            finish(*pending)
        pending = (i, h) + s_next
    finish(*pending)


def _moba_operands(proj, seq):
    blk = MOBA_BLOCK
    nb = seq // blk
    d = MOBA_HEAD_DIM
    blk_of_key = jnp.arange(seq, dtype=jnp.int32) // blk
    avg = (jnp.arange(2 * nb, dtype=jnp.int32)[:, None] == blk_of_key[None, :]).astype(BF16) / blk
    hot = (blk_of_key[:, None] == jnp.arange(LANES, dtype=jnp.int32)[None, :]).astype(BF16)
    eye = jnp.eye(blk, dtype=BF16)

    def head_block(base):
        return pl.BlockSpec((seq, d), lambda b, h: (b, base // d + h))

    const = lambda b, h: (0, 0)
    in_specs = [head_block(P0_Q), head_block(P0_K), head_block(P0_V),
                pl.BlockSpec((2 * nb, seq), const), pl.BlockSpec((seq, LANES), const),
                pl.BlockSpec((blk, blk), const)]
    return in_specs, (proj, proj, proj, avg, hot, eye)


def _ssd(proj, dt_raw, conv_w, conv_b, dt_bias, a_log, d_skip, ssd_norm, bsz, seq, *, chunks_per_step=SSD_CHUNKS_PER_STEP):
    t = bsz * seq
    R = chunks_per_step * SSD_CHUNK
    in_specs, args = _ssd_operands(proj, dt_raw, conv_w, conv_b, dt_bias, a_log, d_skip, ssd_norm, seq, R)
    nc = seq // R
    return pl.pallas_call(
        _ssd_body,
        grid=(bsz, nc),
        in_specs=in_specs,
        out_specs=pl.BlockSpec((R, SSD_INNER), lambda b, c: (b * nc + c, 0)),
        out_shape=jax.ShapeDtypeStruct((t, SSD_INNER), BF16),
        scratch_shapes=[pltpu.VMEM((SSD_STATE, SSD_INNER), F32),
                        pltpu.VMEM((8 + SSD_CHUNK, SSD_CONV_DIM), F32)],
        compiler_params=pltpu.CompilerParams(
            dimension_semantics=("parallel", "arbitrary"), vmem_limit_bytes=VMEM_LIMIT),
        name="ssd_mixer",
    )(*args)


def _moba(proj, bsz, seq):
    t = bsz * seq
    in_specs, args = _moba_operands(proj, seq)
    return pl.pallas_call(
        functools.partial(_moba_body, nb=seq // MOBA_BLOCK, q_tile=MOBA_Q_TILE),
        grid=(bsz, MOBA_HEADS),
        in_specs=in_specs,
        out_specs=pl.BlockSpec((seq, MOBA_HEAD_DIM), lambda b, h: (b, h)),
        out_shape=jax.ShapeDtypeStruct((t, MOBA_INNER), BF16),
        compiler_params=pltpu.CompilerParams(
            dimension_semantics=("parallel", "parallel"), vmem_limit_bytes=VMEM_LIMIT),
        name="moba_attn",
    )(*args)


def _out_proj_body(*refs, n_plain, final_norm):
    plain = refs[:n_plain]
    a_ref, g_ref, x_ref, w_ref = refs[n_plain:n_plain + 4]
    o_ref = refs[-1]
    g = g_ref[...].astype(F32)
    gated = (a_ref[...].astype(F32) * (g * _sigmoid(g))).astype(BF16)
    k0 = 0
    y = x_ref[...]
    for r in plain:
        kw = r.shape[1]
        y = y + _dot(r[...], w_ref[k0:k0 + kw, :])
        k0 += kw
    y = y + _dot(gated, w_ref[k0:k0 + gated.shape[1], :])
    if final_norm:
        fn_ref = refs[-2]
        y = y * lax.rsqrt(jnp.mean(y * y, axis=-1, keepdims=True) + NORM_EPS) * fn_ref[...]
    o_ref[...] = y


def _out_proj(plain, gated, gate_src, gate_col, x2d, w, final_gain=None, *, tm=OUT_PROJ_TM):
    t, d = x2d.shape
    kg = gated.shape[1]
    in_specs = [pl.BlockSpec((tm, a.shape[1]), lambda i: (i, 0)) for a in plain]
    in_specs += [pl.BlockSpec((tm, kg), lambda i: (i, 0)),
                 pl.BlockSpec((tm, kg), lambda i: (i, gate_col // kg)),
                 pl.BlockSpec((tm, d), lambda i: (i, 0)),
                 pl.BlockSpec(w.shape, lambda i: (0, 0))]
    args = list(plain) + [gated, gate_src, x2d, w]
    if final_gain is not None:
        in_specs.append(pl.BlockSpec((1, d), lambda i: (0, 0)))
        args.append(final_gain.reshape(1, d).astype(F32))
    return pl.pallas_call(
        functools.partial(_out_proj_body, n_plain=len(plain), final_norm=final_gain is not None),
        grid=(t // tm,),
        in_specs=in_specs,
        out_specs=pl.BlockSpec((tm, d), lambda i: (i, 0)),
        out_shape=jax.ShapeDtypeStruct((t, d), F32),
        compiler_params=pltpu.CompilerParams(
            dimension_semantics=("parallel",), vmem_limit_bytes=VMEM_LIMIT),
        name="out_proj_final" if final_gain is not None else "out_proj",
    )(*args)


DIL_PHASES = 8


def _dil_tiles(seq):
    ph, qb = DIL_PHASES, DIL_QBLOCK
    per = seq // ph
    tiles = []
    rows = qb // ph
    for n in range(seq // qb):
        qs = [(p * per + rows * n, rows) for p in range(ph)]
        if n == 0:
            ks, mask = [(p * per, rows) for p in range(ph)], "a_first"
        else:
            ks, mask = [(p * per + rows * (n - 1), 2 * rows) for p in range(ph)], "a"
        tiles.append((qs, ks, mask, "init"))
    seg = ph // 4
    rows = qb // seg
    for p4 in range(4):
        for n in range(seq // 4 // qb):
            qs = [((p4 + 4 * j) * per + rows * n, rows) for j in range(seg)]
            if n == 0:
                ks, mask = [((p4 + 4 * j) * per, rows) for j in range(seg)], "b_first"
            else:
                ks, mask = [((p4 + 4 * j) * per + rows * (n - 1), 2 * rows) for j in range(seg)], "b"
            tiles.append((qs, ks, mask, "merge"))
    for p in range(ph):
        for h in range(per // qb):
            qs = [(p * per + qb * h, qb)]
            ks = [(p * per, qb * (h + 1))]
            tiles.append((qs, ks, "c%d" % h, "final"))
    return tiles


def _dil_masks(seq):
    ph, qb = DIL_PHASES, DIL_QBLOCK
    per = seq // ph

    def pos(n_rows, seg_rows, step):
        i = np.arange(n_rows)
        return step * (i % seg_rows) + i // seg_rows

    def band(qpos, kpos, span):
        dist = qpos[:, None] - kpos[None, :]
        return np.where((dist >= 0) & (dist <= span), 0.0, NEG_BIG).astype(np.float32)

    masks = {}
    for name, seg in (("a", ph), ("b", ph // 4)):
        rows = qb // seg
        masks[name] = band(qb + pos(qb, rows, seg), pos(2 * qb, 2 * rows, seg), qb)
        masks[name + "_first"] = band(pos(qb, rows, seg), pos(qb, rows, seg), qb)
    for h in range(per // qb):
        lq = qb * h + np.arange(qb)
        lk = np.arange(qb * (h + 1))
        same = (lq[:, None] - lk[None, :]) % 2 == 0
        masks["c%d" % h] = np.where(same & (lk[None, :] <= lq[:, None]), 0.0, NEG_BIG).astype(np.float32)
    return masks


def _dil_body(*refs, seq, mask_names):
    q_ref, k_ref, v_ref = refs[:3]
    mask_refs = dict(zip(mask_names, refs[3:3 + len(mask_names)]))
    o_ref, onat_ref = refs[3 + len(mask_names):][:2]
    state_a, state_b = refs[-6:-3], refs[-3:]
    state_in = {"merge": state_a, "final": state_b}
    state_out = {"init": state_a, "merge": state_b}
    qb = DIL_QBLOCK
    d = DIL_HEAD_DIM

    def rows_of(ref, slices):
        parts = [ref[s:s + n, :] for s, n in slices]
        return parts[0] if len(parts) == 1 else jnp.concatenate(parts, axis=0)

    def put_rows(ref, slices, val):
        r0 = 0
        for s, n in slices:
            ref[s:s + n, :] = val[r0:r0 + n, :]
            r0 += n

    def scores(tile):
        qs, ks, mask, _ = tile
        return _nt_dot(rows_of(q_ref, qs), rows_of(k_ref, ks)) + mask_refs[mask][...]

    def probs(tile, s):
        qs, _, _, stage = tile
        m_new = jnp.broadcast_to(jnp.max(s, axis=-1, keepdims=True), (s.shape[0], d))
        m_old = None
        if stage != "init":
            m_old = rows_of(state_in[stage][1], qs)
            m_new = jnp.maximum(m_old, m_new)
        p = jnp.concatenate([jnp.exp2(s[:, c0:c0 + d] - m_new) for c0 in range(0, s.shape[1], d)], axis=1)
        return p.astype(BF16), m_new, m_old

    def accumulate(tile, p, m_new, m_old):
        qs, ks, _, stage = tile
        vb = rows_of(v_ref, ks)
        pv = _dot(p, jnp.concatenate([vb, jnp.ones_like(vb)], axis=1))
        acc, l = pv[:, :d], pv[:, d:]
        if stage != "init":
            alpha = jnp.exp2(m_old - m_new)
            acc_ref, _, l_ref = state_in[stage]
            acc = alpha * rows_of(acc_ref, qs) + acc
            l = alpha * rows_of(l_ref, qs) + l
        if stage == "final":
            (s0, _), = qs
            p_idx, l0 = divmod(s0, seq // DIL_PHASES)
            onat_ref[pl.ds(DIL_PHASES * l0 + p_idx, qb, stride=DIL_PHASES), :] = acc / l
        else:
            acc_out, m_out, l_out = state_out[stage]
            put_rows(acc_out, qs, acc)
            put_rows(l_out, qs, l)
            put_rows(m_out, qs, m_new)

    tiles = _dil_tiles(seq)

    def owned(tile):
        return {r for s0, n in tile[0] for r in range(s0, s0 + n)}

    for ta, tb in zip(tiles, tiles[1:]):
        assert not owned(ta) & owned(tb)
    scored, ready = None, None
    for tile in tiles + [None, None]:
        s_new = None if tile is None else (tile, scores(tile))
        p_new = None if scored is None else (scored[0],) + probs(*scored)
        if ready is not None:
            accumulate(*ready)
        scored, ready = s_new, p_new
    o_ref[...] = onat_ref[...].astype(o_ref.dtype)


def _dilated(proj, bsz, seq):
    t = bsz * seq
    d = DIL_HEAD_DIM
    masks = _dil_masks(seq)
    names = tuple(sorted(masks))

    def head_block(base):
        return pl.BlockSpec((seq, d), lambda b, h: (b, base // d + h))

    return pl.pallas_call(
        functools.partial(_dil_body, seq=seq, mask_names=names),
        grid=(bsz, DIL_HEADS),
        in_specs=[head_block(DIL_INNER), head_block(2 * DIL_INNER), head_block(3 * DIL_INNER)]
        + [pl.BlockSpec(masks[n].shape, lambda b, h: (0, 0)) for n in names],
        out_specs=pl.BlockSpec((seq, d), lambda b, h: (b, h)),
        out_shape=jax.ShapeDtypeStruct((t, DIL_INNER), BF16),
        scratch_shapes=[pltpu.VMEM((seq, d), F32)] * 7,
        compiler_params=pltpu.CompilerParams(
            dimension_semantics=("parallel", "parallel"), vmem_limit_bytes=VMEM_LIMIT),
        name="dilated_attn",
    )(proj, proj, proj, *[jnp.asarray(masks[n]) for n in names])


def kernel(x, even_norm, even_w_in, ssd_conv_w, ssd_conv_b, ssd_dt_bias, ssd_a_log, ssd_d, ssd_norm,
           even_w_out, odd_norm, odd_w_in, odd_w_out, final_norm):
    bsz, seq, d = x.shape
    t = bsz * seq
    x2d = x.reshape(t, d)

    w = even_w_in[0]
    z_end = SSD_INNER
    xbc_end = z_end + SSD_CONV_DIM
    dt_end = xbc_end + SSD_HEADS
    q_end = dt_end + MOBA_INNER
    v_end = q_end + 2 * MOBA_INNER
    x_end = z_end + SSD_INNER
    moba_scale = MOBA_HEAD_DIM ** -0.5 * math.log2(math.e)
    g0 = even_norm[0].astype(F32)[None, :]
    w_t = w.T

    def rows(a, b, scale=1.0):
        return (w_t[a:b] * (g0 * scale)).astype(BF16)

    w_main = jnp.concatenate([rows(0, x_end), rows(v_end, w.shape[1]), rows(dt_end, q_end, moba_scale),
                              rows(q_end, v_end), rows(x_end, xbc_end)], axis=0)
    w_dt = jnp.pad(rows(xbc_end, dt_end), ((0, LANES - SSD_HEADS), (0, 0)))
    proj0, dt_raw = _norm_proj_t(x2d, w_main, w_dt, tm=seq // 2, tn=P0_N // PROJ0_COL_TILES)
    y_ssd = _ssd(proj0, dt_raw, ssd_conv_w[0], ssd_conv_b[0], ssd_dt_bias[0], ssd_a_log[0],
                 ssd_d[0], ssd_norm[0], bsz, seq)
    o_moba = _moba(proj0, bsz, seq)
    x1 = _out_proj([y_ssd], o_moba, proj0, P0_G, x2d, even_w_out[0].astype(BF16))

    w1 = odd_w_in[0]
    dil_scale = DIL_HEAD_DIM ** -0.5 * math.log2(math.e)
    g1 = odd_norm[0].astype(F32)[:, None]
    w1_b = jnp.concatenate([(w1[:, 3 * DIL_INNER:] * g1).astype(BF16),
                            (w1[:, :DIL_INNER] * (g1 * dil_scale)).astype(BF16),
                            (w1[:, DIL_INNER:3 * DIL_INNER] * g1).astype(BF16)], axis=1)
    proj1 = _norm_proj_phased(x1, w1_b, tm=seq, tn=PROJ1_TN, phases=DIL_PHASES, token_cols=DIL_INNER)
    o_dil = _dilated(proj1, bsz, seq)
    out = _out_proj([], o_dil, proj1, 0, x1, odd_w_out[0].astype(BF16), final_norm)
    return out.reshape(bsz, seq, d)
```

```python
import functools
import math

import jax
import jax.numpy as jnp
import numpy as np
from jax import lax
from jax.experimental import pallas as pl
from jax.experimental.pallas import tpu as pltpu

F32 = jnp.float32
BF16 = jnp.bfloat16

NORM_EPS = 1e-5

SSD_HEADS = 16
SSD_HEAD_DIM = 64
SSD_INNER = SSD_HEADS * SSD_HEAD_DIM
SSD_GROUPS = 2
SSD_STATE = 128
SSD_CONV = 4
SSD_CHUNK = 128
SSD_BC = 2 * SSD_GROUPS * SSD_STATE
SSD_CONV_DIM = SSD_INNER + SSD_BC

MOBA_HEADS = 8
MOBA_HEAD_DIM = 128
MOBA_INNER = MOBA_HEADS * MOBA_HEAD_DIM
MOBA_BLOCK = 256
MOBA_TOPK = 3

DIL_HEADS = 16
DIL_HEAD_DIM = 128
DIL_INNER = DIL_HEADS * DIL_HEAD_DIM
DIL_QBLOCK = 128
DIL_PATTERNS = ((128, 1), (512, 4), (2048, 16))

LANES = 128
MXU_WIDTH = 256
NEG_BIG = -1e30
VMEM_LIMIT = 56 * 1024 * 1024

PROJ_ROW_SUB = 512
PROJ0_COL_TILES = 2
PROJ1_TN = 4 * MXU_WIDTH
OUT_PROJ_TM = 1024
SSD_CHUNKS_PER_STEP = 4
MOBA_Q_TILE = 256

P0_Z = 0
P0_X = P0_Z + SSD_INNER
P0_G = P0_X + SSD_INNER
P0_Q = P0_G + MOBA_INNER
P0_K = P0_Q + MOBA_INNER
P0_V = P0_K + MOBA_INNER
P0_BC = P0_V + MOBA_INNER
P0_N = P0_BC + SSD_BC


def _nt_dot(a, b):
    return lax.dot_general(a, b, (((1,), (1,)), ((), ())), preferred_element_type=F32)


def _dot(a, b):
    return jnp.dot(a, b, preferred_element_type=F32)


def _sigmoid(x):
    return 1.0 / (1.0 + jnp.exp2(x * -math.log2(math.e)))


def _norm_proj_body(*refs, kind, phases=0, n_token_tiles=0, col_pieces=()):
    if kind == "aux":
        x_ref, w_ref, waux_ref, o_ref, oaux_ref, hn_ref = refs
    else:
        x_ref, w_ref, o_ref, rows_ref, hnf_ref = refs
    j = pl.program_id(1)
    tm = x_ref.shape[0]

    if kind == "aux":
        sub = min(tm, PROJ_ROW_SUB)

        def normed(r0):
            x = x_ref[r0:r0 + sub, :]
            ms = jnp.mean(x * x, axis=-1, keepdims=True)
            return (x * lax.rsqrt(ms + NORM_EPS)).astype(BF16)

        starts = list(range(0, tm, sub))

        def column_step(pieces):
            nxt = normed(starts[0])
            for k, r0 in enumerate(starts):
                cur = nxt
                if k + 1 < len(starts):
                    nxt = normed(starts[k + 1])
                hn_ref[r0:r0 + sub, :] = cur
                c0 = 0
                for a, b in pieces:
                    o_ref[r0:r0 + sub, c0:c0 + b - a] = _nt_dot(cur, w_ref[a:b, :]).astype(o_ref.dtype)
                    c0 += b - a

        for step, pieces in enumerate(col_pieces):
            @pl.when(j == step)
            def _(pieces=pieces):
                column_step(pieces)

        @pl.when(j == 0)
        def _():
            oaux_ref[...] = _nt_dot(hn_ref[...], waux_ref[...])

        return

    d = x_ref.shape[1]
    sub = min(tm, PROJ_ROW_SUB)
    starts = list(range(0, tm, sub))
    per = tm // phases
    assert n_token_tiles >= 2 and phases % len(starts) == 0

    def normed(r0):
        x = x_ref[r0:r0 + sub, :]
        ms = jnp.mean(x * x, axis=-1, keepdims=True)
        return x * lax.rsqrt(ms + NORM_EPS)

    def token_rows(side_work):
        nxt = normed(starts[0])
        for k, r0 in enumerate(starts):
            cur = nxt
            if k + 1 < len(starts):
                nxt = normed(starts[k + 1])
            side_work(k, r0, cur)
            o_ref[r0:r0 + sub, :] = _dot(cur.astype(BF16), w_ref[...]).astype(o_ref.dtype)

    def keep(k, r0, hn):
        rows_ref[0, r0:r0 + sub, :] = hn.astype(BF16)
        for cc in range(d // LANES):
            hnf_ref[cc, r0:r0 + sub, :] = hn[:, cc * LANES:(cc + 1) * LANES]

    def build(k):
        for p in range(2 * k, 2 * k + 2):
            for cc in range(d // LANES):
                rows_ref[1, p * per:(p + 1) * per, cc * LANES:(cc + 1) * LANES] = (
                    hnf_ref[cc, pl.ds(p, per, stride=phases), :].astype(BF16))

    assert n_token_tiles == 2 and phases == 2 * len(starts)

    @pl.when(j == 0)
    def _():
        token_rows(keep)

    @pl.when(j == n_token_tiles)
    def _():
        build(0)
        for k, r0 in enumerate(starts):
            if k + 1 < len(starts):
                build(k + 1)
            o_ref[r0:r0 + sub, :] = _dot(rows_ref[1, r0:r0 + sub, :], w_ref[...]).astype(o_ref.dtype)

    @pl.when((j != 0) & (j != n_token_tiles))
    def _():
        order = jnp.where(j < n_token_tiles, 0, 1)
        for r0 in starts:
            o_ref[r0:r0 + sub, :] = _dot(rows_ref[order, r0:r0 + sub, :], w_ref[...]).astype(o_ref.dtype)


def _split_ranges(ranges, width):
    groups, room = [[]], width
    for a, b in ranges:
        while a < b:
            take = min(b - a, room)
            groups[-1].append((a, a + take))
            a, room = a + take, room - take
            if room == 0:
                groups.append([])
                room = width
    assert not groups[-1] and room == width
    return tuple(tuple(g) for g in groups[:-1])


def _norm_proj_t(x2d, w_t, w_aux_t, col_pieces, *, tm):
    t, d = x2d.shape
    widths = {sum(b - a for a, b in pieces) for pieces in col_pieces}
    (tn,) = widths
    n = tn * len(col_pieces)
    return pl.pallas_call(
        functools.partial(_norm_proj_body, kind="aux", col_pieces=col_pieces),
        grid=(t // tm, len(col_pieces)),
        in_specs=[pl.BlockSpec((tm, d), lambda i, j: (i, 0)),
                  pl.BlockSpec(w_t.shape, lambda i, j: (0, 0), pipeline_mode=pl.Buffered(1)),
                  pl.BlockSpec((LANES, d), lambda i, j: (0, 0))],
        out_specs=[pl.BlockSpec((tm, tn), lambda i, j: (i, j)),
                   pl.BlockSpec((tm, LANES), lambda i, j: (i, 0))],
        out_shape=[jax.ShapeDtypeStruct((t, n), BF16), jax.ShapeDtypeStruct((t, LANES), F32)],
        scratch_shapes=[pltpu.VMEM((tm, d), BF16)],
        compiler_params=pltpu.CompilerParams(
            dimension_semantics=("parallel", "arbitrary"), vmem_limit_bytes=VMEM_LIMIT),
        name="norm_proj_aux",
    )(x2d, w_t, w_aux_t)


def _norm_proj_phased(x2d, w, *, tm, tn, phases, token_cols):
    t, d = x2d.shape
    n = w.shape[1]
    nt, n_token_tiles = n // tn, token_cols // tn

    def col_tile(j):
        return (j + nt - n_token_tiles) % nt

    return pl.pallas_call(
        functools.partial(_norm_proj_body, kind="phased", phases=phases, n_token_tiles=n_token_tiles),
        grid=(t // tm, nt),
        in_specs=[pl.BlockSpec((tm, d), lambda i, j: (i, 0)),
                  pl.BlockSpec((d, tn), lambda i, j: (0, col_tile(j)))],
        out_specs=pl.BlockSpec((tm, tn), lambda i, j: (i, col_tile(j))),
        out_shape=jax.ShapeDtypeStruct((t, n), BF16),
        scratch_shapes=[pltpu.VMEM((2, tm, d), BF16), pltpu.VMEM((d // LANES, tm, LANES), F32)],
        compiler_params=pltpu.CompilerParams(
            dimension_semantics=("parallel", "arbitrary"), vmem_limit_bytes=VMEM_LIMIT),
        name="norm_proj",
    )(x2d, w)


def _split3(a):
    hi = a.astype(BF16)
    r = a - hi.astype(F32)
    mid = r.astype(BF16)
    lo = (r - mid.astype(F32)).astype(BF16)
    return hi, mid, lo


def _expand_heads(a, e):
    hi, mid, lo = _split3(a)
    return _dot(hi, e) + _dot(mid, e) + _dot(lo, e)


def _ssd_body(z_ref, xs_ref, bc_ref, dt_ref, cw_ref, cb_ref, dtb_ref, alog_ref, dskip_ref, nrm_ref,
              e_ref, tri_ref, y_ref, state_ref, ubuf_ref):
    c = pl.program_id(1)
    L = SSD_CHUNK
    tail = 8

    @pl.when(c == 0)
    def _():
        state_ref[...] = jnp.zeros_like(state_ref)
        ubuf_ref[0:tail, :] = jnp.zeros((tail, SSD_CONV_DIM), F32)

    row = lax.broadcasted_iota(jnp.int32, (L, L), 0)
    col = lax.broadcasted_iota(jnp.int32, (L, L), 1)
    causal = col <= row
    lane = lax.broadcasted_iota(jnp.int32, (L, LANES), 1)
    low_half = lane < SSD_HEAD_DIM
    heads_per_group = SSD_HEADS // SSD_GROUPS
    gw = heads_per_group * SSD_HEAD_DIM

    def chunk(r):
        acts = []
        cw = cw_ref[...]
        for c0 in range(0, SSD_CONV_DIM, SSD_BC):
            cols = slice(c0, c0 + SSD_BC)
            src = xs_ref[r, cols] if c0 < SSD_INNER else bc_ref[r, :]
            u = src.astype(F32)
            ubuf_ref[tail:tail + L, cols] = u
            acc = cb_ref[:, cols] + cw[SSD_CONV - 1:SSD_CONV, cols] * u
            for k in range(SSD_CONV - 1):
                acc = acc + cw[k:k + 1, cols] * ubuf_ref[pl.ds(tail - (SSD_CONV - 1) + k, L), cols]
            ubuf_ref[0:tail, cols] = u[L - tail:L, :]
            acts.append(acc * _sigmoid(acc))
        act = jnp.concatenate(acts, axis=1)

        xs = act[:, :SSD_INNER]

        dtr = dt_ref[r, :] + dtb_ref[...]
        dt = jnp.maximum(dtr, 0.0) + jnp.log1p(jnp.exp(-jnp.abs(dtr)))
        a = -jnp.exp(alog_ref[...]) * math.log2(math.e)
        ac = dt * a
        tri = tri_ref[...]
        hi, mid, lo = _split3(ac)
        a_cum = _dot(tri, hi) + _dot(tri, mid) + _dot(tri, lo)
        a_last = a_cum[L - 1:L, :]
        dec_states = jnp.exp2(a_last - a_cum)
        exp_acum = jnp.exp2(a_cum)
        e = e_ref[...]
        dt_e = _expand_heads(dt, e)
        dec_e = _expand_heads(dec_states, e)
        ea_e = _expand_heads(exp_acum, e)
        a_cum_t = a_cum.T

        xc = xs * dt_e
        xc_b = xc.astype(BF16)
        xdec_b = (xc * dec_e).astype(BF16)

        for g in range(SSD_GROUPS):
            b_g = act[:, SSD_INNER + g * SSD_STATE:SSD_INNER + (g + 1) * SSD_STATE]
            c_g = act[:, SSD_INNER + (SSD_GROUPS + g) * SSD_STATE:
                      SSD_INNER + (SSD_GROUPS + g + 1) * SSD_STATE]
            b_gb = b_g.astype(BF16)
            c_gb = c_g.astype(BF16)
            scores = _nt_dot(c_gb, b_gb)
            gs = slice(g * gw, (g + 1) * gw)

            prev = state_ref[:, gs]
            y_off = _dot(c_gb, prev.astype(BF16))
            st_new = _dot(b_g.T.astype(BF16), xdec_b[:, gs])
            state_ref[:, gs] = prev * ea_e[L - 1:L, gs] + st_new

            pieces = []
            for pair in range(heads_per_group // 2):
                ms = []
                for hh in range(2):
                    h = g * heads_per_group + 2 * pair + hh
                    seg = a_cum[:, h:h + 1] - a_cum_t[h:h + 1, :]
                    lmat = jnp.exp2(jnp.where(causal, seg, NEG_BIG))
                    ms.append((scores * lmat).astype(BF16))
                m_pair = jnp.concatenate(ms, axis=1)
                cs = slice(g * gw + pair * LANES, g * gw + (pair + 1) * LANES)
                x_pair = xc_b[:, cs]
                zero = jnp.zeros_like(x_pair)
                rhs = jnp.concatenate([jnp.where(low_half, x_pair, zero),
                                       jnp.where(low_half, zero, x_pair)], axis=0)
                pieces.append(_dot(m_pair, rhs))
            y_diag = jnp.concatenate(pieces, axis=1)

            y = y_diag + y_off * ea_e[:, gs] + xs[:, gs] * dskip_ref[:, gs]
            zg = z_ref[r, gs].astype(F32)
            ug = y * (zg * _sigmoid(zg))
            ug = ug * lax.rsqrt(jnp.mean(ug * ug, axis=-1, keepdims=True) + NORM_EPS)
            y_ref[r, gs] = (ug * nrm_ref[:, gs]).astype(y_ref.dtype)

    for ci in range(xs_ref.shape[0] // L):
        chunk(slice(ci * L, (ci + 1) * L))


def _ssd_operands(proj, dt_raw, conv_w, conv_b, dt_bias, a_log, d_skip, ssd_norm, seq, R):
    L = SSD_CHUNK
    nc = seq // R

    def pad_heads(v):
        return jnp.pad(v.astype(F32), (0, LANES - SSD_HEADS)).reshape(1, LANES)

    head_of_chan = jnp.arange(SSD_INNER, dtype=jnp.int32) // SSD_HEAD_DIM
    expand = (jnp.arange(LANES, dtype=jnp.int32)[:, None] == head_of_chan[None, :]).astype(BF16)
    tri = (jnp.arange(L)[:, None] >= jnp.arange(L)[None, :]).astype(BF16)
    d_chan = jnp.repeat(d_skip.astype(F32), SSD_HEAD_DIM).reshape(1, SSD_INNER)

    def rows(b, c):
        return b * nc + c

    const = lambda b, c: (0, 0)
    in_specs = [
            pl.BlockSpec((R, SSD_INNER), lambda b, c: (rows(b, c), P0_Z // SSD_INNER)),
            pl.BlockSpec((R, SSD_INNER), lambda b, c: (rows(b, c), P0_X // SSD_INNER)),
            pl.BlockSpec((R, SSD_BC), lambda b, c: (rows(b, c), P0_BC // SSD_BC)),
            pl.BlockSpec((R, LANES), lambda b, c: (rows(b, c), 0)),
            pl.BlockSpec((SSD_CONV, SSD_CONV_DIM), const),
            pl.BlockSpec((1, SSD_CONV_DIM), const),
            pl.BlockSpec((1, LANES), const),
            pl.BlockSpec((1, LANES), const),
            pl.BlockSpec((1, SSD_INNER), const),
            pl.BlockSpec((1, SSD_INNER), const),
            pl.BlockSpec((LANES, SSD_INNER), const),
            pl.BlockSpec((L, L), const),
    ]
    args = (proj, proj, proj, dt_raw, conv_w.astype(F32), conv_b.reshape(1, -1).astype(F32),
            pad_heads(dt_bias), pad_heads(a_log), d_chan, ssd_norm.reshape(1, -1).astype(F32), expand, tri)
    return in_specs, args


def _moba_body(q_ref, k_ref, v_ref, avg_ref, hot_ref, eye_ref, o_ref, *, nb, q_tile):
    blk = MOBA_BLOCK
    n_dense = MOBA_TOPK + 1

    kmean = _dot(avg_ref[...], k_ref[...]).astype(BF16)
    jrow = lax.broadcasted_iota(jnp.int32, (nb, blk), 0)
    row = lax.broadcasted_iota(jnp.int32, (q_tile, blk), 0)
    col = lax.broadcasted_iota(jnp.int32, (q_tile, blk), 1)

    def query_block(i):
        q_i = q_ref[i * blk:(i + 1) * blk, :]
        if i < n_dense:
            return q_i, None
        gate_t = _nt_dot(kmean, q_i)[0:nb, :]
        valid = jrow < i
        gm = jnp.where(valid, gate_t, -jnp.inf)
        rank = jnp.zeros((nb, blk), F32)
        for r in range(1, nb):
            gr = pltpu.roll(gm, r, axis=0)
            lower = ((jrow - r) & (nb - 1)) < jrow
            rank = rank + jnp.where(gr > gm, 1.0, 0.0) + jnp.where((gr == gm) & lower, 1.0, 0.0)
        keep = valid & (rank < float(MOBA_TOPK))
        bias_t = jnp.where(keep, 0.0, NEG_BIG)
        bias_t = jnp.concatenate([bias_t, jnp.zeros((LANES - nb, blk), F32)], axis=0).astype(BF16)
        bias = _nt_dot(eye_ref[...], bias_t).astype(BF16)
        return q_i, jnp.concatenate([q_i, bias], axis=1)

    def scores(i, h, q_i, q_aug):
        rs = slice(h * q_tile, (h + 1) * q_tile)
        s_own = _nt_dot(q_i[rs, :], k_ref[i * blk:(i + 1) * blk, :])
        s_own = jnp.where(col <= row + h * q_tile, s_own, NEG_BIG)
        if i == 0:
            return s_own, None
        if q_aug is None:
            return s_own, _nt_dot(q_i[rs, :], k_ref[0:i * blk, :])
        k_past = jnp.concatenate([k_ref[0:i * blk, :], hot_ref[0:i * blk, :]], axis=1)
        return s_own, _nt_dot(q_aug[rs, :], k_past)

    def finish(i, h, s_own, s_past):
        m = jnp.max(s_own, axis=-1, keepdims=True)
        if s_past is None:
            p = jnp.exp2(s_own - m)
        else:
            m = jnp.maximum(m, jnp.max(s_past, axis=-1, keepdims=True))
            p = jnp.concatenate([jnp.exp2(s_past - m), jnp.exp2(s_own - m)], axis=1)
        vb = v_ref[0:(i + 1) * blk, :]
        pv = _dot(p.astype(BF16), jnp.concatenate([vb, jnp.ones_like(vb)], axis=1))
        d = vb.shape[1]
        o_ref[i * blk + h * q_tile:i * blk + (h + 1) * q_tile, :] = (pv[:, :d] / pv[:, d:]).astype(o_ref.dtype)

    tiles = [(i, h) for i in range(nb) for h in range(blk // q_tile)]
    qcache = {}
    pending = None
    for i, h in tiles:
        if i not in qcache:
            qcache = {i: query_block(i)}
        s_next = scores(i, h, *qcache[i])
        if pending is not None:
            finish(*pending)
        pending = (i, h) + s_next
    finish(*pending)


def _moba_operands(proj, seq):
    blk = MOBA_BLOCK
    nb = seq // blk
    d = MOBA_HEAD_DIM
    blk_of_key = jnp.arange(seq, dtype=jnp.int32) // blk
    avg = (jnp.arange(2 * nb, dtype=jnp.int32)[:, None] == blk_of_key[None, :]).astype(BF16) / blk
    hot = (blk_of_key[:, None] == jnp.arange(LANES, dtype=jnp.int32)[None, :]).astype(BF16)
    eye = jnp.eye(blk, dtype=BF16)

    def head_block(base):
        return pl.BlockSpec((seq, d), lambda b, h: (b, base // d + h))

    const = lambda b, h: (0, 0)
    in_specs = [head_block(P0_Q), head_block(P0_K), head_block(P0_V),
                pl.BlockSpec((2 * nb, seq), const), pl.BlockSpec((seq, LANES), const),
                pl.BlockSpec((blk, blk), const)]
    return in_specs, (proj, proj, proj, avg, hot, eye)


def _ssd(proj, dt_raw, conv_w, conv_b, dt_bias, a_log, d_skip, ssd_norm, bsz, seq, *, chunks_per_step=SSD_CHUNKS_PER_STEP):
    t = bsz * seq
    R = chunks_per_step * SSD_CHUNK
    in_specs, args = _ssd_operands(proj, dt_raw, conv_w, conv_b, dt_bias, a_log, d_skip, ssd_norm, seq, R)
    nc = seq // R
    return pl.pallas_call(
        _ssd_body,
        grid=(bsz, nc),
        in_specs=in_specs,
        out_specs=pl.BlockSpec((R, SSD_INNER), lambda b, c: (b * nc + c, 0)),
        out_shape=jax.ShapeDtypeStruct((t, SSD_INNER), BF16),
        scratch_shapes=[pltpu.VMEM((SSD_STATE, SSD_INNER), F32),
                        pltpu.VMEM((8 + SSD_CHUNK, SSD_CONV_DIM), F32)],
        compiler_params=pltpu.CompilerParams(
            dimension_semantics=("parallel", "arbitrary"), vmem_limit_bytes=VMEM_LIMIT),
        name="ssd_mixer",
    )(*args)


def _moba(proj, bsz, seq):
    t = bsz * seq
    in_specs, args = _moba_operands(proj, seq)
    return pl.pallas_call(
        functools.partial(_moba_body, nb=seq // MOBA_BLOCK, q_tile=MOBA_Q_TILE),
        grid=(bsz, MOBA_HEADS),
        in_specs=in_specs,
        out_specs=pl.BlockSpec((seq, MOBA_HEAD_DIM), lambda b, h: (b, h)),
        out_shape=jax.ShapeDtypeStruct((t, MOBA_INNER), BF16),
        compiler_params=pltpu.CompilerParams(
            dimension_semantics=("parallel", "parallel"), vmem_limit_bytes=VMEM_LIMIT),
        name="moba_attn",
    )(*args)


def _out_proj_body(*refs, n_plain, final_norm):
    plain = refs[:n_plain]
    a_ref, g_ref, x_ref, wf_ref = refs[n_plain:n_plain + 4]
    o_ref, w_ref = refs[-2:]

    @pl.when(pl.program_id(0) == 0)
    def _():
        w_ref[...] = wf_ref[...].astype(BF16)

    g = g_ref[...].astype(F32)
    gated = (a_ref[...].astype(F32) * (g * _sigmoid(g))).astype(BF16)
    k0 = 0
    y = x_ref[...]
    for r in plain:
        kw = r.shape[1]
        y = y + _dot(r[...], w_ref[k0:k0 + kw, :])
        k0 += kw
    y = y + _dot(gated, w_ref[k0:k0 + gated.shape[1], :])
    if final_norm:
        fn_ref = refs[-3]
        y = y * lax.rsqrt(jnp.mean(y * y, axis=-1, keepdims=True) + NORM_EPS) * fn_ref[...]
    o_ref[...] = y


def _out_proj(plain, gated, gate_src, gate_col, x2d, w, final_gain=None, *, tm=OUT_PROJ_TM):
    t, d = x2d.shape
    kg = gated.shape[1]
    in_specs = [pl.BlockSpec((tm, a.shape[1]), lambda i: (i, 0)) for a in plain]
    in_specs += [pl.BlockSpec((tm, kg), lambda i: (i, 0)),
                 pl.BlockSpec((tm, kg), lambda i: (i, gate_col // kg)),
                 pl.BlockSpec((tm, d), lambda i: (i, 0)),
                 pl.BlockSpec(w.shape, lambda i: (0, 0), pipeline_mode=pl.Buffered(1))]
    args = list(plain) + [gated, gate_src, x2d, w.astype(F32)]
    if final_gain is not None:
        in_specs.append(pl.BlockSpec((1, d), lambda i: (0, 0)))
        args.append(final_gain.reshape(1, d).astype(F32))
    return pl.pallas_call(
        functools.partial(_out_proj_body, n_plain=len(plain), final_norm=final_gain is not None),
        grid=(t // tm,),
        in_specs=in_specs,
        out_specs=pl.BlockSpec((tm, d), lambda i: (i, 0)),
        out_shape=jax.ShapeDtypeStruct((t, d), F32),
        scratch_shapes=[pltpu.VMEM(w.shape, BF16)],
        compiler_params=pltpu.CompilerParams(
            dimension_semantics=("arbitrary",), vmem_limit_bytes=VMEM_LIMIT),
        name="out_proj_final" if final_gain is not None else "out_proj",
    )(*args)


DIL_PHASES = 8


def _dil_tiles(seq):
    ph, qb = DIL_PHASES, DIL_QBLOCK
    per = seq // ph
    assert [r for _, r in DIL_PATTERNS] == [1, 4, 16] and all(w // r == qb for w, r in DIL_PATTERNS)
    assert seq == DIL_PATTERNS[-1][0]
    tiles = []
    rows = qb // ph
    for n in range(seq // qb):
        qs = [(p * per + rows * n, rows) for p in range(ph)]
        if n == 0:
            ks, mask = [(p * per, rows) for p in range(ph)], "a_first"
        else:
            ks, mask = [(p * per + rows * (n - 1), 2 * rows) for p in range(ph)], "a"
        tiles.append((qs, ks, mask, "init"))
    seg = ph // 4
    rows = qb // seg
    for p4 in range(4):
        for n in range(seq // 4 // qb):
            qs = [((p4 + 4 * j) * per + rows * n, rows) for j in range(seg)]
            if n == 0:
                ks, mask = [((p4 + 4 * j) * per, rows) for j in range(seg)], "b_first"
            else:
                ks, mask = [((p4 + 4 * j) * per + rows * (n - 1), 2 * rows) for j in range(seg)], "b"
            tiles.append((qs, ks, mask, "merge"))
    for p in range(ph):
        for h in range(per // qb):
            qs = [(p * per + qb * h, qb)]
            ks = [(p * per, qb * (h + 1))]
            tiles.append((qs, ks, "c%d" % h, "final"))
    return tiles


def _dil_masks(seq):
    ph, qb = DIL_PHASES, DIL_QBLOCK
    per = seq // ph

    def pos(n_rows, seg_rows, step):
        i = np.arange(n_rows)
        return step * (i % seg_rows) + i // seg_rows

    def band(qpos, kpos, span):
        dist = qpos[:, None] - kpos[None, :]
        return np.where((dist >= 0) & (dist <= span), 0.0, NEG_BIG).astype(np.float32)

    masks = {}
    for name, seg in (("a", ph), ("b", ph // 4)):
        rows = qb // seg
        masks[name] = band(qb + pos(qb, rows, seg), pos(2 * qb, 2 * rows, seg), qb)
        masks[name + "_first"] = band(pos(qb, rows, seg), pos(qb, rows, seg), qb)
    for h in range(per // qb):
        lq = qb * h + np.arange(qb)
        lk = np.arange(qb * (h + 1))
        same = (lq[:, None] - lk[None, :]) % 2 == 0
        masks["c%d" % h] = np.where(same & (lk[None, :] <= lq[:, None]), 0.0, NEG_BIG).astype(np.float32)
    return masks


def _dil_body(*refs, seq, mask_names):
    q_ref, k_ref, v_ref = refs[:3]
    mask_refs = dict(zip(mask_names, refs[3:3 + len(mask_names)]))
    o_ref, onat_ref = refs[3 + len(mask_names):][:2]
    state_a, state_b = refs[-6:-3], refs[-3:]
    state_in = {"merge": state_a, "final": state_b}
    state_out = {"init": state_a, "merge": state_b}
    qb = DIL_QBLOCK
    d = DIL_HEAD_DIM

    def rows_of(ref, slices):
        parts = [ref[s:s + n, :] for s, n in slices]
        return parts[0] if len(parts) == 1 else jnp.concatenate(parts, axis=0)

    def put_rows(ref, slices, val):
        r0 = 0
        for s, n in slices:
            ref[s:s + n, :] = val[r0:r0 + n, :]
            r0 += n

    def scores(tile):
        qs, ks, mask, _ = tile
        return _nt_dot(rows_of(q_ref, qs), rows_of(k_ref, ks)) + mask_refs[mask][...]

    def probs(tile, s):
        qs, _, _, stage = tile
        m_new = jnp.broadcast_to(jnp.max(s, axis=-1, keepdims=True), (s.shape[0], d))
        m_old = None
        if stage != "init":
            m_old = rows_of(state_in[stage][1], qs)
            m_new = jnp.maximum(m_old, m_new)
        p = jnp.concatenate([jnp.exp2(s[:, c0:c0 + d] - m_new) for c0 in range(0, s.shape[1], d)], axis=1)
        return p.astype(BF16), m_new, m_old

    def accumulate(tile, p, m_new, m_old):
        qs, ks, _, stage = tile
        vb = rows_of(v_ref, ks)
        pv = _dot(p, jnp.concatenate([vb, jnp.ones_like(vb)], axis=1))
        acc, l = pv[:, :d], pv[:, d:]
        if stage != "init":
            alpha = jnp.exp2(m_old - m_new)
            acc_ref, _, l_ref = state_in[stage]
            acc = alpha * rows_of(acc_ref, qs) + acc
            l = alpha * rows_of(l_ref, qs) + l
        if stage == "final":
            (s0, _), = qs
            p_idx, l0 = divmod(s0, seq // DIL_PHASES)
            onat_ref[pl.ds(DIL_PHASES * l0 + p_idx, qb, stride=DIL_PHASES), :] = acc / l
        else:
            acc_out, m_out, l_out = state_out[stage]
            put_rows(acc_out, qs, acc)
            put_rows(l_out, qs, l)
            put_rows(m_out, qs, m_new)

    tiles = _dil_tiles(seq)

    def owned(tile):
        return {r for s0, n in tile[0] for r in range(s0, s0 + n)}

    for ta, tb in zip(tiles, tiles[1:]):
        assert not owned(ta) & owned(tb)
    scored, ready = None, None
    for tile in tiles + [None, None]:
        s_new = None if tile is None else (tile, scores(tile))
        p_new = None if scored is None else (scored[0],) + probs(*scored)
        if ready is not None:
            accumulate(*ready)
        scored, ready = s_new, p_new
    o_ref[...] = onat_ref[...].astype(o_ref.dtype)


def _dilated(proj, bsz, seq):
    t = bsz * seq
    d = DIL_HEAD_DIM
    masks = _dil_masks(seq)
    names = tuple(sorted(masks))

    def head_block(base):
        return pl.BlockSpec((seq, d), lambda b, h: (b, base // d + h))

    return pl.pallas_call(
        functools.partial(_dil_body, seq=seq, mask_names=names),
        grid=(bsz, DIL_HEADS),
        in_specs=[head_block(0), head_block(DIL_INNER), head_block(2 * DIL_INNER)]
        + [pl.BlockSpec(masks[n].shape, lambda b, h: (0, 0)) for n in names],
        out_specs=pl.BlockSpec((seq, d), lambda b, h: (b, h)),
        out_shape=jax.ShapeDtypeStruct((t, DIL_INNER), BF16),
        scratch_shapes=[pltpu.VMEM((seq, d), F32)] * 7,
        compiler_params=pltpu.CompilerParams(
            dimension_semantics=("parallel", "parallel"), vmem_limit_bytes=VMEM_LIMIT),
        name="dilated_attn",
    )(proj, proj, proj, *[jnp.asarray(masks[n]) for n in names])


def kernel(x, even_norm, even_w_in, ssd_conv_w, ssd_conv_b, ssd_dt_bias, ssd_a_log, ssd_d, ssd_norm,
           even_w_out, odd_norm, odd_w_in, odd_w_out, final_norm):
    bsz, seq, d = x.shape
    t = bsz * seq
    x2d = x.reshape(t, d)

    w = even_w_in[0]
    z_end = SSD_INNER
    xbc_end = z_end + SSD_CONV_DIM
    dt_end = xbc_end + SSD_HEADS
    q_end = dt_end + MOBA_INNER
    v_end = q_end + 2 * MOBA_INNER
    x_end = z_end + SSD_INNER
    moba_scale = MOBA_HEAD_DIM ** -0.5 * math.log2(math.e)
    row_scale = jnp.ones((w.shape[1], 1), F32).at[dt_end:q_end].set(moba_scale)
    w_t = (w.T * row_scale * even_norm[0].astype(F32)[None, :]).astype(BF16)
    w_dt = jnp.pad(w_t[xbc_end:dt_end], ((0, LANES - SSD_HEADS), (0, 0)))
    col_pieces = _split_ranges([(0, x_end), (v_end, w.shape[1]), (dt_end, v_end), (x_end, xbc_end)],
                               P0_N // PROJ0_COL_TILES)
    proj0, dt_raw = _norm_proj_t(x2d, w_t, w_dt, col_pieces, tm=seq // 2)
    y_ssd = _ssd(proj0, dt_raw, ssd_conv_w[0], ssd_conv_b[0], ssd_dt_bias[0], ssd_a_log[0],
                 ssd_d[0], ssd_norm[0], bsz, seq)
    o_moba = _moba(proj0, bsz, seq)
    x1 = _out_proj([y_ssd], o_moba, proj0, P0_G, x2d, even_w_out[0])

    w1 = odd_w_in[0]
    dil_scale = DIL_HEAD_DIM ** -0.5 * math.log2(math.e)
    col_scale1 = jnp.ones((1, w1.shape[1]), F32).at[:, :DIL_INNER].set(dil_scale)
    w1_b = (w1 * col_scale1 * odd_norm[0].astype(F32)[:, None]).astype(BF16)
    proj1 = _norm_proj_phased(x1, w1_b, tm=seq, tn=PROJ1_TN, phases=DIL_PHASES, token_cols=DIL_INNER)
    o_dil = _dilated(proj1, bsz, seq)
    out = _out_proj([], o_dil, proj1, 3 * DIL_INNER, x1, odd_w_out[0], final_norm)
    return out.reshape(bsz, seq, d)
```

```python
import functools
import math

import jax
import jax.numpy as jnp
import numpy as np
from jax import lax
from jax.experimental import pallas as pl
from jax.experimental.pallas import tpu as pltpu

F32 = jnp.float32
BF16 = jnp.bfloat16

NORM_EPS = 1e-5

SSD_HEADS = 16
SSD_HEAD_DIM = 64
SSD_INNER = SSD_HEADS * SSD_HEAD_DIM
SSD_GROUPS = 2
SSD_STATE = 128
SSD_CONV = 4
SSD_CHUNK = 128
SSD_BC = 2 * SSD_GROUPS * SSD_STATE
SSD_CONV_DIM = SSD_INNER + SSD_BC

MOBA_HEADS = 8
MOBA_HEAD_DIM = 128
MOBA_INNER = MOBA_HEADS * MOBA_HEAD_DIM
MOBA_BLOCK = 256
MOBA_TOPK = 3

DIL_HEADS = 16
DIL_HEAD_DIM = 128
DIL_INNER = DIL_HEADS * DIL_HEAD_DIM
DIL_QBLOCK = 128
DIL_PATTERNS = ((128, 1), (512, 4), (2048, 16))

LANES = 128
MXU_WIDTH = 256
NEG_BIG = -1e30
VMEM_LIMIT = 56 * 1024 * 1024

PROJ_ROW_SUB = 512
PROJ0_COL_TILES = 2
PROJ1_TN = 4 * MXU_WIDTH
OUT_PROJ_TM = 1024
SSD_CHUNKS_PER_STEP = 4
MOBA_Q_TILE = 256

P0_Z = 0
P0_X = P0_Z + SSD_INNER
P0_G = P0_X + SSD_INNER
P0_Q = P0_G + MOBA_INNER
P0_K = P0_Q + MOBA_INNER
P0_V = P0_K + MOBA_INNER
P0_BC = P0_V + MOBA_INNER
P0_N = P0_BC + SSD_BC


def _nt_dot(a, b):
    return lax.dot_general(a, b, (((1,), (1,)), ((), ())), preferred_element_type=F32)


def _dot(a, b):
    return jnp.dot(a, b, preferred_element_type=F32)


def _sigmoid(x):
    return 1.0 / (1.0 + jnp.exp2(x * -math.log2(math.e)))


def _norm_proj_body(*refs, kind, phases=0, n_token_tiles=0, col_pieces=()):
    if kind == "aux":
        x_ref, w_ref, waux_ref, o_ref, oaux_ref, hn_ref = refs
    else:
        x_ref, w_ref, o_ref, rows_ref, hnf_ref = refs
    j = pl.program_id(1)
    tm = x_ref.shape[0]

    if kind == "aux":
        sub = min(tm, PROJ_ROW_SUB)

        def normed(r0):
            x = x_ref[r0:r0 + sub, :]
            ms = jnp.mean(x * x, axis=-1, keepdims=True)
            return (x * lax.rsqrt(ms + NORM_EPS)).astype(BF16)

        starts = list(range(0, tm, sub))

        def column_step(pieces):
            nxt = normed(starts[0])
            for k, r0 in enumerate(starts):
                cur = nxt
                if k + 1 < len(starts):
                    nxt = normed(starts[k + 1])
                hn_ref[r0:r0 + sub, :] = cur
                c0 = 0
                for a, b in pieces:
                    o_ref[r0:r0 + sub, c0:c0 + b - a] = _nt_dot(cur, w_ref[a:b, :]).astype(o_ref.dtype)
                    c0 += b - a

        for step, pieces in enumerate(col_pieces):
            @pl.when(j == step)
            def _(pieces=pieces):
                column_step(pieces)

        @pl.when(j == 0)
        def _():
            oaux_ref[...] = _nt_dot(hn_ref[...], waux_ref[...])

        return

    d = x_ref.shape[1]
    sub = min(tm, PROJ_ROW_SUB)
    starts = list(range(0, tm, sub))
    per = tm // phases
    assert n_token_tiles >= 2 and phases % len(starts) == 0

    def normed(r0):
        x = x_ref[r0:r0 + sub, :]
        ms = jnp.mean(x * x, axis=-1, keepdims=True)
        return x * lax.rsqrt(ms + NORM_EPS)

    def token_rows(side_work):
        nxt = normed(starts[0])
        for k, r0 in enumerate(starts):
            cur = nxt
            if k + 1 < len(starts):
                nxt = normed(starts[k + 1])
            side_work(k, r0, cur)
            o_ref[r0:r0 + sub, :] = _dot(cur.astype(BF16), w_ref[...]).astype(o_ref.dtype)

    def keep(k, r0, hn):
        rows_ref[0, r0:r0 + sub, :] = hn.astype(BF16)
        for cc in range(d // LANES):
            hnf_ref[cc, r0:r0 + sub, :] = hn[:, cc * LANES:(cc + 1) * LANES]

    def build(k):
        for p in range(2 * k, 2 * k + 2):
            for cc in range(d // LANES):
                rows_ref[1, p * per:(p + 1) * per, cc * LANES:(cc + 1) * LANES] = (
                    hnf_ref[cc, pl.ds(p, per, stride=phases), :].astype(BF16))

    assert n_token_tiles == 2 and phases == 2 * len(starts)

    @pl.when(j == 0)
    def _():
        token_rows(keep)

    @pl.when(j == n_token_tiles)
    def _():
        build(0)
        for k, r0 in enumerate(starts):
            if k + 1 < len(starts):
                build(k + 1)
            o_ref[r0:r0 + sub, :] = _dot(rows_ref[1, r0:r0 + sub, :], w_ref[...]).astype(o_ref.dtype)

    @pl.when((j != 0) & (j != n_token_tiles))
    def _():
        order = jnp.where(j < n_token_tiles, 0, 1)
        for r0 in starts:
            o_ref[r0:r0 + sub, :] = _dot(rows_ref[order, r0:r0 + sub, :], w_ref[...]).astype(o_ref.dtype)


def _split_ranges(ranges, width):
    groups, room = [[]], width
    for a, b in ranges:
        while a < b:
            take = min(b - a, room)
            groups[-1].append((a, a + take))
            a, room = a + take, room - take
            if room == 0:
                groups.append([])
                room = width
    assert not groups[-1] and room == width
    return tuple(tuple(g) for g in groups[:-1])


def _norm_proj_t(x2d, w_t, w_aux_t, col_pieces, *, tm):
    t, d = x2d.shape
    widths = {sum(b - a for a, b in pieces) for pieces in col_pieces}
    (tn,) = widths
    n = tn * len(col_pieces)
    return pl.pallas_call(
        functools.partial(_norm_proj_body, kind="aux", col_pieces=col_pieces),
        grid=(t // tm, len(col_pieces)),
        in_specs=[pl.BlockSpec((tm, d), lambda i, j: (i, 0)),
                  pl.BlockSpec(w_t.shape, lambda i, j: (0, 0), pipeline_mode=pl.Buffered(1)),
                  pl.BlockSpec((LANES, d), lambda i, j: (0, 0))],
        out_specs=[pl.BlockSpec((tm, tn), lambda i, j: (i, j)),
                   pl.BlockSpec((tm, LANES), lambda i, j: (i, 0))],
        out_shape=[jax.ShapeDtypeStruct((t, n), BF16), jax.ShapeDtypeStruct((t, LANES), F32)],
        scratch_shapes=[pltpu.VMEM((tm, d), BF16)],
        compiler_params=pltpu.CompilerParams(
            dimension_semantics=("parallel", "arbitrary"), vmem_limit_bytes=VMEM_LIMIT),
        name="norm_proj_aux",
    )(x2d, w_t, w_aux_t)


def _norm_proj_phased(x2d, w, *, tm, tn, phases, token_cols):
    t, d = x2d.shape
    n = w.shape[1]
    nt, n_token_tiles = n // tn, token_cols // tn

    def col_tile(j):
        return (j + nt - n_token_tiles) % nt

    return pl.pallas_call(
        functools.partial(_norm_proj_body, kind="phased", phases=phases, n_token_tiles=n_token_tiles),
        grid=(t // tm, nt),
        in_specs=[pl.BlockSpec((tm, d), lambda i, j: (i, 0)),
                  pl.BlockSpec((d, tn), lambda i, j: (0, col_tile(j)))],
        out_specs=pl.BlockSpec((tm, tn), lambda i, j: (i, col_tile(j))),
        out_shape=jax.ShapeDtypeStruct((t, n), BF16),
        scratch_shapes=[pltpu.VMEM((2, tm, d), BF16), pltpu.VMEM((d // LANES, tm, LANES), F32)],
        compiler_params=pltpu.CompilerParams(
            dimension_semantics=("parallel", "arbitrary"), vmem_limit_bytes=VMEM_LIMIT),
        name="norm_proj",
    )(x2d, w)


def _split3(a):
    hi = a.astype(BF16)
    r = a - hi.astype(F32)
    mid = r.astype(BF16)
    lo = (r - mid.astype(F32)).astype(BF16)
    return hi, mid, lo


def _expand_heads(a, e):
    hi, mid, lo = _split3(a)
    return _dot(hi, e) + _dot(mid, e) + _dot(lo, e)


def _ssd_body(z_ref, xs_ref, bc_ref, dt_ref, cw_ref, cb_ref, dtb_ref, alog_ref, dskip_ref, nrm_ref,
              e_ref, tri_ref, y_ref, state_ref, ubuf_ref):
    c = pl.program_id(1)
    L = SSD_CHUNK
    tail = 8

    @pl.when(c == 0)
    def _():
        state_ref[...] = jnp.zeros_like(state_ref)
        ubuf_ref[0:tail, :] = jnp.zeros((tail, SSD_CONV_DIM), F32)

    row = lax.broadcasted_iota(jnp.int32, (L, L), 0)
    col = lax.broadcasted_iota(jnp.int32, (L, L), 1)
    causal = col <= row
    lane = lax.broadcasted_iota(jnp.int32, (L, LANES), 1)
    low_half = lane < SSD_HEAD_DIM
    heads_per_group = SSD_HEADS // SSD_GROUPS
    gw = heads_per_group * SSD_HEAD_DIM

    def chunk(r):
        acts = []
        cw = cw_ref[...]
        for c0 in range(0, SSD_CONV_DIM, SSD_BC):
            cols = slice(c0, c0 + SSD_BC)
            src = xs_ref[r, cols] if c0 < SSD_INNER else bc_ref[r, :]
            u = src.astype(F32)
            ubuf_ref[tail:tail + L, cols] = u
            acc = cb_ref[:, cols] + cw[SSD_CONV - 1:SSD_CONV, cols] * u
            for k in range(SSD_CONV - 1):
                acc = acc + cw[k:k + 1, cols] * ubuf_ref[pl.ds(tail - (SSD_CONV - 1) + k, L), cols]
            ubuf_ref[0:tail, cols] = u[L - tail:L, :]
            acts.append(acc * _sigmoid(acc))
        act = jnp.concatenate(acts, axis=1)

        xs = act[:, :SSD_INNER]

        dtr = dt_ref[r, :] + dtb_ref[...]
        dt = jnp.maximum(dtr, 0.0) + jnp.log1p(jnp.exp(-jnp.abs(dtr)))
        a = -jnp.exp(alog_ref[...]) * math.log2(math.e)
        ac = dt * a
        tri = tri_ref[...]
        hi, mid, lo = _split3(ac)
        a_cum = _dot(tri, hi) + _dot(tri, mid) + _dot(tri, lo)
        a_last = a_cum[L - 1:L, :]
        dec_states = jnp.exp2(a_last - a_cum)
        exp_acum = jnp.exp2(a_cum)
        e = e_ref[...]
        dt_e = _expand_heads(dt, e)
        dec_e = _expand_heads(dec_states, e)
        ea_e = _expand_heads(exp_acum, e)
        a_cum_t = a_cum.T

        xc = xs * dt_e
        xc_b = xc.astype(BF16)
        xdec_b = (xc * dec_e).astype(BF16)

        for g in range(SSD_GROUPS):
            b_g = act[:, SSD_INNER + g * SSD_STATE:SSD_INNER + (g + 1) * SSD_STATE]
            c_g = act[:, SSD_INNER + (SSD_GROUPS + g) * SSD_STATE:
                      SSD_INNER + (SSD_GROUPS + g + 1) * SSD_STATE]
            b_gb = b_g.astype(BF16)
            c_gb = c_g.astype(BF16)
            scores = _nt_dot(c_gb, b_gb)
            gs = slice(g * gw, (g + 1) * gw)

            prev = state_ref[:, gs]
            y_off = _dot(c_gb, prev.astype(BF16))
            st_new = _dot(b_g.T.astype(BF16), xdec_b[:, gs])
            state_ref[:, gs] = prev * ea_e[L - 1:L, gs] + st_new

            pieces = []
            for pair in range(heads_per_group // 2):
                ms = []
                for hh in range(2):
                    h = g * heads_per_group + 2 * pair + hh
                    seg = a_cum[:, h:h + 1] - a_cum_t[h:h + 1, :]
                    lmat = jnp.exp2(jnp.where(causal, seg, NEG_BIG))
                    ms.append((scores * lmat).astype(BF16))
                m_pair = jnp.concatenate(ms, axis=1)
                cs = slice(g * gw + pair * LANES, g * gw + (pair + 1) * LANES)
                x_pair = xc_b[:, cs]
                zero = jnp.zeros_like(x_pair)
                rhs = jnp.concatenate([jnp.where(low_half, x_pair, zero),
                                       jnp.where(low_half, zero, x_pair)], axis=0)
                pieces.append(_dot(m_pair, rhs))
            y_diag = jnp.concatenate(pieces, axis=1)

            y = y_diag + y_off * ea_e[:, gs] + xs[:, gs] * dskip_ref[:, gs]
            zg = z_ref[r, gs].astype(F32)
            ug = y * (zg * _sigmoid(zg))
            ug = ug * lax.rsqrt(jnp.mean(ug * ug, axis=-1, keepdims=True) + NORM_EPS)
            y_ref[r, gs] = (ug * nrm_ref[:, gs]).astype(y_ref.dtype)

    for ci in range(xs_ref.shape[0] // L):
        chunk(slice(ci * L, (ci + 1) * L))


def _ssd_operands(proj, dt_raw, conv_w, conv_b, dt_bias, a_log, d_skip, ssd_norm, seq, R):
    L = SSD_CHUNK
    nc = seq // R

    def pad_heads(v):
        return jnp.pad(v.astype(F32), (0, LANES - SSD_HEADS)).reshape(1, LANES)

    head_of_chan = np.arange(SSD_INNER) // SSD_HEAD_DIM
    expand = jnp.asarray(np.arange(LANES)[:, None] == head_of_chan[None, :], BF16)
    tri = jnp.asarray(np.arange(L)[:, None] >= np.arange(L)[None, :], BF16)
    d_chan = jnp.repeat(d_skip.astype(F32), SSD_HEAD_DIM).reshape(1, SSD_INNER)

    def rows(b, c):
        return b * nc + c

    const = lambda b, c: (0, 0)
    in_specs = [
            pl.BlockSpec((R, SSD_INNER), lambda b, c: (rows(b, c), P0_Z // SSD_INNER)),
            pl.BlockSpec((R, SSD_INNER), lambda b, c: (rows(b, c), P0_X // SSD_INNER)),
            pl.BlockSpec((R, SSD_BC), lambda b, c: (rows(b, c), P0_BC // SSD_BC)),
            pl.BlockSpec((R, LANES), lambda b, c: (rows(b, c), 0)),
            pl.BlockSpec((SSD_CONV, SSD_CONV_DIM), const),
            pl.BlockSpec((1, SSD_CONV_DIM), const),
            pl.BlockSpec((1, LANES), const),
            pl.BlockSpec((1, LANES), const),
            pl.BlockSpec((1, SSD_INNER), const),
            pl.BlockSpec((1, SSD_INNER), const),
            pl.BlockSpec((LANES, SSD_INNER), const),
            pl.BlockSpec((L, L), const),
    ]
    args = (proj, proj, proj, dt_raw, conv_w.astype(F32), conv_b.reshape(1, -1).astype(F32),
            pad_heads(dt_bias), pad_heads(a_log), d_chan, ssd_norm.reshape(1, -1).astype(F32), expand, tri)
    return in_specs, args


def _moba_body(q_ref, k_ref, v_ref, avg_ref, hot_ref, eye_ref, o_ref, *, nb, q_tile):
    blk = MOBA_BLOCK
    n_dense = MOBA_TOPK + 1

    kmean = _dot(avg_ref[...], k_ref[...]).astype(BF16)
    jrow = lax.broadcasted_iota(jnp.int32, (nb, blk), 0)
    row = lax.broadcasted_iota(jnp.int32, (q_tile, blk), 0)
    col = lax.broadcasted_iota(jnp.int32, (q_tile, blk), 1)

    def query_block(i):
        q_i = q_ref[i * blk:(i + 1) * blk, :]
        if i < n_dense:
            return q_i, None
        gate_t = _nt_dot(kmean, q_i)[0:nb, :]
        valid = jrow < i
        gm = jnp.where(valid, gate_t, -jnp.inf)
        rank = jnp.zeros((nb, blk), F32)
        for r in range(1, nb):
            gr = pltpu.roll(gm, r, axis=0)
            lower = ((jrow - r) & (nb - 1)) < jrow
            rank = rank + jnp.where(gr > gm, 1.0, 0.0) + jnp.where((gr == gm) & lower, 1.0, 0.0)
        keep = valid & (rank < float(MOBA_TOPK))
        bias_t = jnp.where(keep, 0.0, NEG_BIG)
        bias_t = jnp.concatenate([bias_t, jnp.zeros((LANES - nb, blk), F32)], axis=0).astype(BF16)
        bias = _nt_dot(eye_ref[...], bias_t).astype(BF16)
        return q_i, jnp.concatenate([q_i, bias], axis=1)

    def scores(i, h, q_i, q_aug):
        rs = slice(h * q_tile, (h + 1) * q_tile)
        s_own = _nt_dot(q_i[rs, :], k_ref[i * blk:(i + 1) * blk, :])
        s_own = jnp.where(col <= row + h * q_tile, s_own, NEG_BIG)
        if i == 0:
            return s_own, None
        if q_aug is None:
            return s_own, _nt_dot(q_i[rs, :], k_ref[0:i * blk, :])
        k_past = jnp.concatenate([k_ref[0:i * blk, :], hot_ref[0:i * blk, :]], axis=1)
        return s_own, _nt_dot(q_aug[rs, :], k_past)

    def finish(i, h, s_own, s_past):
        m = jnp.max(s_own, axis=-1, keepdims=True)
        if s_past is None:
            p = jnp.exp2(s_own - m)
        else:
            m = jnp.maximum(m, jnp.max(s_past, axis=-1, keepdims=True))
            p = jnp.concatenate([jnp.exp2(s_past - m), jnp.exp2(s_own - m)], axis=1)
        vb = v_ref[0:(i + 1) * blk, :]
        pv = _dot(p.astype(BF16), jnp.concatenate([vb, jnp.ones_like(vb)], axis=1))
        d = vb.shape[1]
        o_ref[i * blk + h * q_tile:i * blk + (h + 1) * q_tile, :] = (pv[:, :d] / pv[:, d:]).astype(o_ref.dtype)

    tiles = [(i, h) for i in range(nb) for h in range(blk // q_tile)]
    qcache = {}
    pending = None
    for i, h in tiles:
        if i not in qcache:
            qcache = {i: query_block(i)}
        s_next = scores(i, h, *qcache[i])
        if pending is not None:
            finish(*pending)
        pending = (i, h) + s_next
    finish(*pending)


def _moba_operands(proj, seq):
    blk = MOBA_BLOCK
    nb = seq // blk
    d = MOBA_HEAD_DIM
    blk_of_key = np.arange(seq) // blk
    avg = jnp.asarray((np.arange(2 * nb)[:, None] == blk_of_key[None, :]) / blk, BF16)
    hot = jnp.asarray(blk_of_key[:, None] == np.arange(LANES)[None, :], BF16)
    eye = jnp.asarray(np.eye(blk), BF16)

    def head_block(base):
        return pl.BlockSpec((seq, d), lambda b, h: (b, base // d + h))

    const = lambda b, h: (0, 0)
    in_specs = [head_block(P0_Q), head_block(P0_K), head_block(P0_V),
                pl.BlockSpec((2 * nb, seq), const), pl.BlockSpec((seq, LANES), const),
                pl.BlockSpec((blk, blk), const)]
    return in_specs, (proj, proj, proj, avg, hot, eye)


def _ssd(proj, dt_raw, conv_w, conv_b, dt_bias, a_log, d_skip, ssd_norm, bsz, seq, *, chunks_per_step=SSD_CHUNKS_PER_STEP):
    t = bsz * seq
    R = chunks_per_step * SSD_CHUNK
    in_specs, args = _ssd_operands(proj, dt_raw, conv_w, conv_b, dt_bias, a_log, d_skip, ssd_norm, seq, R)
    nc = seq // R
    return pl.pallas_call(
        _ssd_body,
        grid=(bsz, nc),
        in_specs=in_specs,
        out_specs=pl.BlockSpec((R, SSD_INNER), lambda b, c: (b * nc + c, 0)),
        out_shape=jax.ShapeDtypeStruct((t, SSD_INNER), BF16),
        scratch_shapes=[pltpu.VMEM((SSD_STATE, SSD_INNER), F32),
                        pltpu.VMEM((8 + SSD_CHUNK, SSD_CONV_DIM), F32)],
        compiler_params=pltpu.CompilerParams(
            dimension_semantics=("parallel", "arbitrary"), vmem_limit_bytes=VMEM_LIMIT),
        name="ssd_mixer",
    )(*args)


def _moba(proj, bsz, seq):
    t = bsz * seq
    in_specs, args = _moba_operands(proj, seq)
    return pl.pallas_call(
        functools.partial(_moba_body, nb=seq // MOBA_BLOCK, q_tile=MOBA_Q_TILE),
        grid=(bsz, MOBA_HEADS),
        in_specs=in_specs,
        out_specs=pl.BlockSpec((seq, MOBA_HEAD_DIM), lambda b, h: (b, h)),
        out_shape=jax.ShapeDtypeStruct((t, MOBA_INNER), BF16),
        compiler_params=pltpu.CompilerParams(
            dimension_semantics=("parallel", "parallel"), vmem_limit_bytes=VMEM_LIMIT),
        name="moba_attn",
    )(*args)


def _out_proj_body(*refs, n_plain, final_norm):
    plain = refs[:n_plain]
    a_ref, g_ref, x_ref, wf_ref = refs[n_plain:n_plain + 4]
    o_ref, w_ref = refs[-2:]

    @pl.when(pl.program_id(0) == 0)
    def _():
        w_ref[...] = wf_ref[...].astype(BF16)

    g = g_ref[...].astype(F32)
    gated = (a_ref[...].astype(F32) * (g * _sigmoid(g))).astype(BF16)
    k0 = 0
    y = x_ref[...]
    for r in plain:
        kw = r.shape[1]
        y = y + _dot(r[...], w_ref[k0:k0 + kw, :])
        k0 += kw
    y = y + _dot(gated, w_ref[k0:k0 + gated.shape[1], :])
    if final_norm:
        fn_ref = refs[-3]
        y = y * lax.rsqrt(jnp.mean(y * y, axis=-1, keepdims=True) + NORM_EPS) * fn_ref[...]
    o_ref[...] = y


def _out_proj(plain, gated, gate_src, gate_col, x2d, w, final_gain=None, *, tm=OUT_PROJ_TM):
    t, d = x2d.shape
    kg = gated.shape[1]
    in_specs = [pl.BlockSpec((tm, a.shape[1]), lambda i: (i, 0)) for a in plain]
    in_specs += [pl.BlockSpec((tm, kg), lambda i: (i, 0)),
                 pl.BlockSpec((tm, kg), lambda i: (i, gate_col // kg)),
                 pl.BlockSpec((tm, d), lambda i: (i, 0)),
                 pl.BlockSpec(w.shape, lambda i: (0, 0), pipeline_mode=pl.Buffered(1))]
    args = list(plain) + [gated, gate_src, x2d, w.astype(F32)]
    if final_gain is not None:
        in_specs.append(pl.BlockSpec((1, d), lambda i: (0, 0)))
        args.append(final_gain.reshape(1, d).astype(F32))
    return pl.pallas_call(
        functools.partial(_out_proj_body, n_plain=len(plain), final_norm=final_gain is not None),
        grid=(t // tm,),
        in_specs=in_specs,
        out_specs=pl.BlockSpec((tm, d), lambda i: (i, 0)),
        out_shape=jax.ShapeDtypeStruct((t, d), F32),
        scratch_shapes=[pltpu.VMEM(w.shape, BF16)],
        compiler_params=pltpu.CompilerParams(
            dimension_semantics=("arbitrary",), vmem_limit_bytes=VMEM_LIMIT),
        name="out_proj_final" if final_gain is not None else "out_proj",
    )(*args)


DIL_PHASES = 8


def _dil_tiles(seq):
    ph, qb = DIL_PHASES, DIL_QBLOCK
    per = seq // ph
    assert [r for _, r in DIL_PATTERNS] == [1, 4, 16] and all(w // r == qb for w, r in DIL_PATTERNS)
    assert seq == DIL_PATTERNS[-1][0]
    tiles = []
    rows = qb // ph
    for n in range(seq // qb):
        qs = [(p * per + rows * n, rows) for p in range(ph)]
        if n == 0:
            ks, mask = [(p * per, rows) for p in range(ph)], "a_first"
        else:
            ks, mask = [(p * per + rows * (n - 1), 2 * rows) for p in range(ph)], "a"
        tiles.append((qs, ks, mask, "init"))
    seg = ph // 4
    rows = qb // seg
    for p4 in range(4):
        for n in range(seq // 4 // qb):
            qs = [((p4 + 4 * j) * per + rows * n, rows) for j in range(seg)]
            if n == 0:
                ks, mask = [((p4 + 4 * j) * per, rows) for j in range(seg)], "b_first"
            else:
                ks, mask = [((p4 + 4 * j) * per + rows * (n - 1), 2 * rows) for j in range(seg)], "b"
            tiles.append((qs, ks, mask, "merge"))
    for p in range(ph):
        for h in range(per // qb):
            qs = [(p * per + qb * h, qb)]
            ks = [(p * per, qb * (h + 1))]
            tiles.append((qs, ks, "c%d" % h, "final"))
    return tiles


def _dil_masks(seq):
    ph, qb = DIL_PHASES, DIL_QBLOCK
    per = seq // ph

    def pos(n_rows, seg_rows, step):
        i = np.arange(n_rows)
        return step * (i % seg_rows) + i // seg_rows

    def band(qpos, kpos, span):
        dist = qpos[:, None] - kpos[None, :]
        return np.where((dist >= 0) & (dist <= span), 0.0, NEG_BIG).astype(np.float32)

    masks = {}
    for name, seg in (("a", ph), ("b", ph // 4)):
        rows = qb // seg
        masks[name] = band(qb + pos(qb, rows, seg), pos(2 * qb, 2 * rows, seg), qb)
        masks[name + "_first"] = band(pos(qb, rows, seg), pos(qb, rows, seg), qb)
    for h in range(per // qb):
        lq = qb * h + np.arange(qb)
        lk = np.arange(qb * (h + 1))
        same = (lq[:, None] - lk[None, :]) % 2 == 0
        masks["c%d" % h] = np.where(same & (lk[None, :] <= lq[:, None]), 0.0, NEG_BIG).astype(np.float32)
    return masks


def _dil_body(*refs, seq, mask_names):
    q_ref, k_ref, v_ref = refs[:3]
    mask_refs = dict(zip(mask_names, refs[3:3 + len(mask_names)]))
    o_ref, onat_ref = refs[3 + len(mask_names):][:2]
    state_a, state_b = refs[-6:-3], refs[-3:]
    state_in = {"merge": state_a, "final": state_b}
    state_out = {"init": state_a, "merge": state_b}
    qb = DIL_QBLOCK
    d = DIL_HEAD_DIM

    def rows_of(ref, slices):
        parts = [ref[s:s + n, :] for s, n in slices]
        return parts[0] if len(parts) == 1 else jnp.concatenate(parts, axis=0)

    def put_rows(ref, slices, val):
        r0 = 0
        for s, n in slices:
            ref[s:s + n, :] = val[r0:r0 + n, :]
            r0 += n

    def scores(tile):
        qs, ks, mask, _ = tile
        return _nt_dot(rows_of(q_ref, qs), rows_of(k_ref, ks)) + mask_refs[mask][...]

    def probs(tile, s):
        qs, _, _, stage = tile
        m_new = jnp.broadcast_to(jnp.max(s, axis=-1, keepdims=True), (s.shape[0], d))
        m_old = None
        if stage != "init":
            m_old = rows_of(state_in[stage][1], qs)
            m_new = jnp.maximum(m_old, m_new)
        p = jnp.concatenate([jnp.exp2(s[:, c0:c0 + d] - m_new) for c0 in range(0, s.shape[1], d)], axis=1)
        return p.astype(BF16), m_new, m_old

    def accumulate(tile, p, m_new, m_old):
        qs, ks, _, stage = tile
        vb = rows_of(v_ref, ks)
        pv = _dot(p, jnp.concatenate([vb, jnp.ones_like(vb)], axis=1))
        acc, l = pv[:, :d], pv[:, d:]
        if stage != "init":
            alpha = jnp.exp2(m_old - m_new)
            acc_ref, _, l_ref = state_in[stage]
            acc = alpha * rows_of(acc_ref, qs) + acc
            l = alpha * rows_of(l_ref, qs) + l
        if stage == "final":
            (s0, _), = qs
            p_idx, l0 = divmod(s0, seq // DIL_PHASES)
            onat_ref[pl.ds(DIL_PHASES * l0 + p_idx, qb, stride=DIL_PHASES), :] = acc / l
        else:
            acc_out, m_out, l_out = state_out[stage]
            put_rows(acc_out, qs, acc)
            put_rows(l_out, qs, l)
            put_rows(m_out, qs, m_new)

    tiles = _dil_tiles(seq)

    def owned(tile):
        return {r for s0, n in tile[0] for r in range(s0, s0 + n)}

    for ta, tb in zip(tiles, tiles[1:]):
        assert not owned(ta) & owned(tb)
    scored, ready = None, None
    for tile in tiles + [None, None]:
        s_new = None if tile is None else (tile, scores(tile))
        p_new = None if scored is None else (scored[0],) + probs(*scored)
        if ready is not None:
            accumulate(*ready)
        scored, ready = s_new, p_new
    o_ref[...] = onat_ref[...].astype(o_ref.dtype)


def _dilated(proj, bsz, seq):
    t = bsz * seq
    d = DIL_HEAD_DIM
    masks = _dil_masks(seq)
    names = tuple(sorted(masks))

    def head_block(base):
        return pl.BlockSpec((seq, d), lambda b, h: (b, base // d + h))

    return pl.pallas_call(
        functools.partial(_dil_body, seq=seq, mask_names=names),
        grid=(bsz, DIL_HEADS),
        in_specs=[head_block(0), head_block(DIL_INNER), head_block(2 * DIL_INNER)]
        + [pl.BlockSpec(masks[n].shape, lambda b, h: (0, 0)) for n in names],
        out_specs=pl.BlockSpec((seq, d), lambda b, h: (b, h)),
        out_shape=jax.ShapeDtypeStruct((t, DIL_INNER), BF16),
        scratch_shapes=[pltpu.VMEM((seq, d), F32)] * 7,
        compiler_params=pltpu.CompilerParams(
            dimension_semantics=("parallel", "parallel"), vmem_limit_bytes=VMEM_LIMIT),
        name="dilated_attn",
    )(proj, proj, proj, *[jnp.asarray(masks[n]) for n in names])


def kernel(x, even_norm, even_w_in, ssd_conv_w, ssd_conv_b, ssd_dt_bias, ssd_a_log, ssd_d, ssd_norm,
           even_w_out, odd_norm, odd_w_in, odd_w_out, final_norm):
    bsz, seq, d = x.shape
    t = bsz * seq
    x2d = x.reshape(t, d)

    w = even_w_in[0]
    z_end = SSD_INNER
    xbc_end = z_end + SSD_CONV_DIM
    dt_end = xbc_end + SSD_HEADS
    q_end = dt_end + MOBA_INNER
    v_end = q_end + 2 * MOBA_INNER
    x_end = z_end + SSD_INNER
    moba_scale = MOBA_HEAD_DIM ** -0.5 * math.log2(math.e)
    row_scale = jnp.ones((w.shape[1], 1), F32).at[dt_end:q_end].set(moba_scale)
    w_t = (w.T * row_scale * even_norm[0].astype(F32)[None, :]).astype(BF16)
    w_dt = jnp.pad(w_t[xbc_end:dt_end], ((0, LANES - SSD_HEADS), (0, 0)))
    col_pieces = _split_ranges([(0, x_end), (v_end, w.shape[1]), (dt_end, v_end), (x_end, xbc_end)],
                               P0_N // PROJ0_COL_TILES)
    proj0, dt_raw = _norm_proj_t(x2d, w_t, w_dt, col_pieces, tm=seq // 2)
    y_ssd = _ssd(proj0, dt_raw, ssd_conv_w[0], ssd_conv_b[0], ssd_dt_bias[0], ssd_a_log[0],
                 ssd_d[0], ssd_norm[0], bsz, seq)
    o_moba = _moba(proj0, bsz, seq)
    x1 = _out_proj([y_ssd], o_moba, proj0, P0_G, x2d, even_w_out[0])

    w1 = odd_w_in[0]
    dil_scale = DIL_HEAD_DIM ** -0.5 * math.log2(math.e)
    col_scale1 = jnp.ones((1, w1.shape[1]), F32).at[:, :DIL_INNER].set(dil_scale)
    w1_b = (w1 * col_scale1 * odd_norm[0].astype(F32)[:, None]).astype(BF16)
    proj1 = _norm_proj_phased(x1, w1_b, tm=seq, tn=PROJ1_TN, phases=DIL_PHASES, token_cols=DIL_INNER)
    o_dil = _dilated(proj1, bsz, seq)
    out = _out_proj([], o_dil, proj1, 3 * DIL_INNER, x1, odd_w_out[0], final_norm)
    return out.reshape(bsz, seq, d)
```

```python
import functools
import math

import jax
import jax.numpy as jnp
import numpy as np
from jax import lax
from jax.experimental import pallas as pl
from jax.experimental.pallas import tpu as pltpu

F32 = jnp.float32
BF16 = jnp.bfloat16

NORM_EPS = 1e-5

SSD_HEADS = 16
SSD_HEAD_DIM = 64
SSD_INNER = SSD_HEADS * SSD_HEAD_DIM
SSD_GROUPS = 2
SSD_STATE = 128
SSD_CONV = 4
SSD_CHUNK = 128
SSD_BC = 2 * SSD_GROUPS * SSD_STATE
SSD_CONV_DIM = SSD_INNER + SSD_BC

MOBA_HEADS = 8
MOBA_HEAD_DIM = 128
MOBA_INNER = MOBA_HEADS * MOBA_HEAD_DIM
MOBA_BLOCK = 256
MOBA_TOPK = 3

DIL_HEADS = 16
DIL_HEAD_DIM = 128
DIL_INNER = DIL_HEADS * DIL_HEAD_DIM
DIL_QBLOCK = 128
DIL_PATTERNS = ((128, 1), (512, 4), (2048, 16))

LANES = 128
MXU_WIDTH = 256
NEG_BIG = -1e30
VMEM_LIMIT = 56 * 1024 * 1024

PROJ_ROW_SUB = 512
PROJ0_COL_TILES = 2
PROJ1_TN = 4 * MXU_WIDTH
OUT_PROJ_TM = 1024
SSD_CHUNKS_PER_STEP = 4
MOBA_Q_TILE = 256

P0_Z = 0
P0_X = P0_Z + SSD_INNER
P0_G = P0_X + SSD_INNER
P0_Q = P0_G + MOBA_INNER
P0_K = P0_Q + MOBA_INNER
P0_V = P0_K + MOBA_INNER
P0_BC = P0_V + MOBA_INNER
P0_N = P0_BC + SSD_BC


def _nt_dot(a, b):
    return lax.dot_general(a, b, (((1,), (1,)), ((), ())), preferred_element_type=F32)


def _dot(a, b):
    return jnp.dot(a, b, preferred_element_type=F32)


def _sigmoid(x):
    return 1.0 / (1.0 + jnp.exp2(x * -math.log2(math.e)))


def _norm_proj_body(*refs, kind, phases=0, n_token_tiles=0, col_pieces=()):
    if kind == "aux":
        x_ref, w_ref, waux_ref, o_ref, oaux_ref, hn_ref = refs
    else:
        x_ref, w_ref, o_ref, rows_ref, hnf_ref = refs
    j = pl.program_id(1)
    tm = x_ref.shape[0]

    if kind == "aux":
        sub = min(tm, PROJ_ROW_SUB)

        def normed(r0):
            x = x_ref[r0:r0 + sub, :]
            ms = jnp.mean(x * x, axis=-1, keepdims=True)
            return (x * lax.rsqrt(ms + NORM_EPS)).astype(BF16)

        starts = list(range(0, tm, sub))

        def column_step(pieces):
            nxt = normed(starts[0])
            for k, r0 in enumerate(starts):
                cur = nxt
                if k + 1 < len(starts):
                    nxt = normed(starts[k + 1])
                hn_ref[r0:r0 + sub, :] = cur
                c0 = 0
                for a, b in pieces:
                    o_ref[r0:r0 + sub, c0:c0 + b - a] = _nt_dot(cur, w_ref[a:b, :]).astype(o_ref.dtype)
                    c0 += b - a

        for step, pieces in enumerate(col_pieces):
            @pl.when(j == step)
            def _(pieces=pieces):
                column_step(pieces)

        @pl.when(j == 0)
        def _():
            oaux_ref[...] = _nt_dot(hn_ref[...], waux_ref[...])

        return

    d = x_ref.shape[1]
    sub = min(tm, PROJ_ROW_SUB)
    starts = list(range(0, tm, sub))
    per = tm // phases
    assert n_token_tiles >= 2 and phases % len(starts) == 0

    def normed(r0):
        x = x_ref[r0:r0 + sub, :]
        ms = jnp.mean(x * x, axis=-1, keepdims=True)
        return x * lax.rsqrt(ms + NORM_EPS)

    def token_rows(side_work):
        nxt = normed(starts[0])
        for k, r0 in enumerate(starts):
            cur = nxt
            if k + 1 < len(starts):
                nxt = normed(starts[k + 1])
            side_work(k, r0, cur)
            o_ref[r0:r0 + sub, :] = _dot(cur.astype(BF16), w_ref[...]).astype(o_ref.dtype)

    def keep(k, r0, hn):
        rows_ref[0, r0:r0 + sub, :] = hn.astype(BF16)
        for cc in range(d // LANES):
            hnf_ref[cc, r0:r0 + sub, :] = hn[:, cc * LANES:(cc + 1) * LANES]

    def build(k):
        for p in range(2 * k, 2 * k + 2):
            for cc in range(d // LANES):
                rows_ref[1, p * per:(p + 1) * per, cc * LANES:(cc + 1) * LANES] = (
                    hnf_ref[cc, pl.ds(p, per, stride=phases), :].astype(BF16))

    assert n_token_tiles == 2 and phases == 2 * len(starts)

    @pl.when(j == 0)
    def _():
        token_rows(keep)

    @pl.when(j == n_token_tiles)
    def _():
        build(0)
        for k, r0 in enumerate(starts):
            if k + 1 < len(starts):
                build(k + 1)
            o_ref[r0:r0 + sub, :] = _dot(rows_ref[1, r0:r0 + sub, :], w_ref[...]).astype(o_ref.dtype)

    @pl.when((j != 0) & (j != n_token_tiles))
    def _():
        order = jnp.where(j < n_token_tiles, 0, 1)
        for r0 in starts:
            o_ref[r0:r0 + sub, :] = _dot(rows_ref[order, r0:r0 + sub, :], w_ref[...]).astype(o_ref.dtype)


def _split_ranges(ranges, width):
    groups, room = [[]], width
    for a, b in ranges:
        while a < b:
            take = min(b - a, room)
            groups[-1].append((a, a + take))
            a, room = a + take, room - take
            if room == 0:
                groups.append([])
                room = width
    assert not groups[-1] and room == width
    return tuple(tuple(g) for g in groups[:-1])


def _norm_proj_t(x2d, w_t, w_aux_t, col_pieces, *, tm):
    t, d = x2d.shape
    widths = {sum(b - a for a, b in pieces) for pieces in col_pieces}
    (tn,) = widths
    n = tn * len(col_pieces)
    return pl.pallas_call(
        functools.partial(_norm_proj_body, kind="aux", col_pieces=col_pieces),
        grid=(t // tm, len(col_pieces)),
        in_specs=[pl.BlockSpec((tm, d), lambda i, j: (i, 0)),
                  pl.BlockSpec(w_t.shape, lambda i, j: (0, 0), pipeline_mode=pl.Buffered(1)),
                  pl.BlockSpec((LANES, d), lambda i, j: (0, 0))],
        out_specs=[pl.BlockSpec((tm, tn), lambda i, j: (i, j)),
                   pl.BlockSpec((tm, LANES), lambda i, j: (i, 0))],
        out_shape=[jax.ShapeDtypeStruct((t, n), BF16), jax.ShapeDtypeStruct((t, LANES), F32)],
        scratch_shapes=[pltpu.VMEM((tm, d), BF16)],
        compiler_params=pltpu.CompilerParams(
            dimension_semantics=("parallel", "arbitrary"), vmem_limit_bytes=VMEM_LIMIT),
        name="norm_proj_aux",
    )(x2d, w_t, w_aux_t)


def _norm_proj_phased(x2d, w, *, tm, tn, phases, token_cols):
    t, d = x2d.shape
    n = w.shape[1]
    nt, n_token_tiles = n // tn, token_cols // tn

    def col_tile(j):
        return (j + nt - n_token_tiles) % nt

    return pl.pallas_call(
        functools.partial(_norm_proj_body, kind="phased", phases=phases, n_token_tiles=n_token_tiles),
        grid=(t // tm, nt),
        in_specs=[pl.BlockSpec((tm, d), lambda i, j: (i, 0)),
                  pl.BlockSpec((d, tn), lambda i, j: (0, col_tile(j)))],
        out_specs=pl.BlockSpec((tm, tn), lambda i, j: (i, col_tile(j))),
        out_shape=jax.ShapeDtypeStruct((t, n), BF16),
        scratch_shapes=[pltpu.VMEM((2, tm, d), BF16), pltpu.VMEM((d // LANES, tm, LANES), F32)],
        compiler_params=pltpu.CompilerParams(
            dimension_semantics=("parallel", "arbitrary"), vmem_limit_bytes=VMEM_LIMIT),
        name="norm_proj",
    )(x2d, w)


def _split3(a):
    hi = a.astype(BF16)
    r = a - hi.astype(F32)
    mid = r.astype(BF16)
    lo = (r - mid.astype(F32)).astype(BF16)
    return hi, mid, lo


def _expand_heads(a, e):
    hi, mid, lo = _split3(a)
    return _dot(hi, e) + _dot(mid, e) + _dot(lo, e)


def _ssd_body(z_ref, xs_ref, bc_ref, dt_ref, cw_ref, cb_ref, dtb_ref, alog_ref, dskip_ref, nrm_ref,
              e_ref, tri_ref, y_ref, state_ref, ubuf_ref):
    c = pl.program_id(1)
    L = SSD_CHUNK
    tail = 8

    @pl.when(c == 0)
    def _():
        state_ref[...] = jnp.zeros_like(state_ref)
        ubuf_ref[0:tail, :] = jnp.zeros((tail, SSD_CONV_DIM), F32)

    row = lax.broadcasted_iota(jnp.int32, (L, L), 0)
    col = lax.broadcasted_iota(jnp.int32, (L, L), 1)
    causal = col <= row
    lane = lax.broadcasted_iota(jnp.int32, (L, LANES), 1)
    low_half = lane < SSD_HEAD_DIM
    heads_per_group = SSD_HEADS // SSD_GROUPS
    gw = heads_per_group * SSD_HEAD_DIM

    def chunk(r):
        acts = []
        cw = cw_ref[...]
        for c0 in range(0, SSD_CONV_DIM, SSD_BC):
            cols = slice(c0, c0 + SSD_BC)
            src = xs_ref[r, cols] if c0 < SSD_INNER else bc_ref[r, :]
            u = src.astype(F32)
            ubuf_ref[tail:tail + L, cols] = u
            acc = cb_ref[:, cols] + cw[SSD_CONV - 1:SSD_CONV, cols] * u
            for k in range(SSD_CONV - 1):
                acc = acc + cw[k:k + 1, cols] * ubuf_ref[pl.ds(tail - (SSD_CONV - 1) + k, L), cols]
            ubuf_ref[0:tail, cols] = u[L - tail:L, :]
            acts.append(acc * _sigmoid(acc))
        act = jnp.concatenate(acts, axis=1)

        xs = act[:, :SSD_INNER]

        dtr = dt_ref[r, :] + dtb_ref[...]
        dt = jnp.maximum(dtr, 0.0) + jnp.log1p(jnp.exp(-jnp.abs(dtr)))
        a = -jnp.exp(alog_ref[...]) * math.log2(math.e)
        ac = dt * a
        tri = tri_ref[...]
        hi, mid, lo = _split3(ac)
        a_cum = _dot(tri, hi) + _dot(tri, mid) + _dot(tri, lo)
        a_last = a_cum[L - 1:L, :]
        dec_states = jnp.exp2(a_last - a_cum)
        exp_acum = jnp.exp2(a_cum)
        e = e_ref[...]
        dt_e = _expand_heads(dt, e)
        dec_e = _expand_heads(dec_states, e)
        ea_e = _expand_heads(exp_acum, e)
        a_cum_t = a_cum.T

        xc = xs * dt_e
        xc_b = xc.astype(BF16)
        xdec_b = (xc * dec_e).astype(BF16)

        for g in range(SSD_GROUPS):
            b_g = act[:, SSD_INNER + g * SSD_STATE:SSD_INNER + (g + 1) * SSD_STATE]
            c_g = act[:, SSD_INNER + (SSD_GROUPS + g) * SSD_STATE:
                      SSD_INNER + (SSD_GROUPS + g + 1) * SSD_STATE]
            b_gb = b_g.astype(BF16)
            c_gb = c_g.astype(BF16)
            scores = _nt_dot(c_gb, b_gb)
            gs = slice(g * gw, (g + 1) * gw)

            prev = state_ref[:, gs]
            y_off = _dot(c_gb, prev.astype(BF16))
            st_new = _dot(b_g.T.astype(BF16), xdec_b[:, gs])
            state_ref[:, gs] = prev * ea_e[L - 1:L, gs] + st_new

            pieces = []
            for pair in range(heads_per_group // 2):
                ms = []
                for hh in range(2):
                    h = g * heads_per_group + 2 * pair + hh
                    seg = a_cum[:, h:h + 1] - a_cum_t[h:h + 1, :]
                    lmat = jnp.exp2(jnp.where(causal, seg, NEG_BIG))
                    ms.append((scores * lmat).astype(BF16))
                m_pair = jnp.concatenate(ms, axis=1)
                cs = slice(g * gw + pair * LANES, g * gw + (pair + 1) * LANES)
                x_pair = xc_b[:, cs]
                zero = jnp.zeros_like(x_pair)
                rhs = jnp.concatenate([jnp.where(low_half, x_pair, zero),
                                       jnp.where(low_half, zero, x_pair)], axis=0)
                pieces.append(_dot(m_pair, rhs))
            y_diag = jnp.concatenate(pieces, axis=1)

            y = y_diag + y_off * ea_e[:, gs] + xs[:, gs] * dskip_ref[:, gs]
            zg = z_ref[r, gs].astype(F32)
            ug = y * (zg * _sigmoid(zg))
            ug = ug * lax.rsqrt(jnp.mean(ug * ug, axis=-1, keepdims=True) + NORM_EPS)
            y_ref[r, gs] = (ug * nrm_ref[:, gs]).astype(y_ref.dtype)

    for ci in range(xs_ref.shape[0] // L):
        chunk(slice(ci * L, (ci + 1) * L))


def _ssd_operands(proj, dt_raw, conv_w, conv_b, dt_bias, a_log, d_skip, ssd_norm, seq, R):
    L = SSD_CHUNK
    nc = seq // R

    def pad_heads(v):
        return jnp.pad(v.astype(F32), (0, LANES - SSD_HEADS)).reshape(1, LANES)

    head_of_chan = np.arange(SSD_INNER) // SSD_HEAD_DIM
    expand = jnp.asarray(np.arange(LANES)[:, None] == head_of_chan[None, :], BF16)
    tri = jnp.asarray(np.arange(L)[:, None] >= np.arange(L)[None, :], BF16)
    d_chan = jnp.repeat(d_skip.astype(F32), SSD_HEAD_DIM).reshape(1, SSD_INNER)

    def rows(b, c):
        return b * nc + c

    const = lambda b, c: (0, 0)
    in_specs = [
            pl.BlockSpec((R, SSD_INNER), lambda b, c: (rows(b, c), P0_Z // SSD_INNER)),
            pl.BlockSpec((R, SSD_INNER), lambda b, c: (rows(b, c), P0_X // SSD_INNER)),
            pl.BlockSpec((R, SSD_BC), lambda b, c: (rows(b, c), P0_BC // SSD_BC)),
            pl.BlockSpec((R, LANES), lambda b, c: (rows(b, c), 0)),
            pl.BlockSpec((SSD_CONV, SSD_CONV_DIM), const),
            pl.BlockSpec((1, SSD_CONV_DIM), const),
            pl.BlockSpec((1, LANES), const),
            pl.BlockSpec((1, LANES), const),
            pl.BlockSpec((1, SSD_INNER), const),
            pl.BlockSpec((1, SSD_INNER), const),
            pl.BlockSpec((LANES, SSD_INNER), const),
            pl.BlockSpec((L, L), const),
    ]
    args = (proj, proj, proj, dt_raw, conv_w.astype(F32), conv_b.reshape(1, -1).astype(F32),
            pad_heads(dt_bias), pad_heads(a_log), d_chan, ssd_norm.reshape(1, -1).astype(F32), expand, tri)
    return in_specs, args


def _moba_body(q_ref, k_ref, v_ref, avg_ref, hot_ref, eye_ref, o_ref, *, nb, q_tile):
    blk = MOBA_BLOCK
    n_dense = MOBA_TOPK + 1

    kmean = _dot(avg_ref[...], k_ref[...]).astype(BF16)
    jrow = lax.broadcasted_iota(jnp.int32, (nb, blk), 0)
    row = lax.broadcasted_iota(jnp.int32, (q_tile, blk), 0)
    col = lax.broadcasted_iota(jnp.int32, (q_tile, blk), 1)

    def query_block(i):
        q_i = q_ref[i * blk:(i + 1) * blk, :]
        if i < n_dense:
            return q_i, None
        gate_t = _nt_dot(kmean, q_i)[0:nb, :]
        valid = jrow < i
        gm = jnp.where(valid, gate_t, -jnp.inf)
        rank = jnp.zeros((nb, blk), F32)
        for r in range(1, nb):
            gr = pltpu.roll(gm, r, axis=0)
            lower = ((jrow - r) & (nb - 1)) < jrow
            rank = rank + jnp.where(gr > gm, 1.0, 0.0) + jnp.where((gr == gm) & lower, 1.0, 0.0)
        keep = valid & (rank < float(MOBA_TOPK))
        bias_t = jnp.where(keep, 0.0, NEG_BIG)
        bias_t = jnp.concatenate([bias_t, jnp.zeros((LANES - nb, blk), F32)], axis=0).astype(BF16)
        bias = _nt_dot(eye_ref[...], bias_t).astype(BF16)
        return q_i, jnp.concatenate([q_i, bias], axis=1)

    def scores(i, h, q_i, q_aug):
        rs = slice(h * q_tile, (h + 1) * q_tile)
        s_own = _nt_dot(q_i[rs, :], k_ref[i * blk:(i + 1) * blk, :])
        s_own = jnp.where(col <= row + h * q_tile, s_own, NEG_BIG)
        if i == 0:
            return s_own, None
        if q_aug is None:
            return s_own, _nt_dot(q_i[rs, :], k_ref[0:i * blk, :])
        k_past = jnp.concatenate([k_ref[0:i * blk, :], hot_ref[0:i * blk, :]], axis=1)
        return s_own, _nt_dot(q_aug[rs, :], k_past)

    def finish(i, h, s_own, s_past):
        m = jnp.max(s_own, axis=-1, keepdims=True)
        if s_past is None:
            p = jnp.exp2(s_own - m)
        else:
            m = jnp.maximum(m, jnp.max(s_past, axis=-1, keepdims=True))
            p = jnp.concatenate([jnp.exp2(s_past - m), jnp.exp2(s_own - m)], axis=1)
        vb = v_ref[0:(i + 1) * blk, :]
        pv = _dot(p.astype(BF16), jnp.concatenate([vb, jnp.ones_like(vb)], axis=1))
        d = vb.shape[1]
        o_ref[i * blk + h * q_tile:i * blk + (h + 1) * q_tile, :] = (pv[:, :d] / pv[:, d:]).astype(o_ref.dtype)

    tiles = [(i, h) for i in range(nb) for h in range(blk // q_tile)]
    qcache = {}
    pending = None
    for i, h in tiles:
        if i not in qcache:
            qcache = {i: query_block(i)}
        s_next = scores(i, h, *qcache[i])
        if pending is not None:
            finish(*pending)
        pending = (i, h) + s_next
    finish(*pending)


def _moba_operands(proj, seq):
    blk = MOBA_BLOCK
    nb = seq // blk
    d = MOBA_HEAD_DIM
    blk_of_key = np.arange(seq) // blk
    avg = jnp.asarray((np.arange(2 * nb)[:, None] == blk_of_key[None, :]) / blk, BF16)
    hot = jnp.asarray(blk_of_key[:, None] == np.arange(LANES)[None, :], BF16)
    eye = jnp.asarray(np.eye(blk), BF16)

    def head_block(base):
        return pl.BlockSpec((seq, d), lambda b, h: (b, base // d + h))

    const = lambda b, h: (0, 0)
    in_specs = [head_block(P0_Q), head_block(P0_K), head_block(P0_V),
                pl.BlockSpec((2 * nb, seq), const), pl.BlockSpec((seq, LANES), const),
                pl.BlockSpec((blk, blk), const)]
    return in_specs, (proj, proj, proj, avg, hot, eye)


def _ssd(proj, dt_raw, conv_w, conv_b, dt_bias, a_log, d_skip, ssd_norm, bsz, seq, *, chunks_per_step=SSD_CHUNKS_PER_STEP):
    t = bsz * seq
    R = chunks_per_step * SSD_CHUNK
    in_specs, args = _ssd_operands(proj, dt_raw, conv_w, conv_b, dt_bias, a_log, d_skip, ssd_norm, seq, R)
    nc = seq // R
    return pl.pallas_call(
        _ssd_body,
        grid=(bsz, nc),
        in_specs=in_specs,
        out_specs=pl.BlockSpec((R, SSD_INNER), lambda b, c: (b * nc + c, 0)),
        out_shape=jax.ShapeDtypeStruct((t, SSD_INNER), BF16),
        scratch_shapes=[pltpu.VMEM((SSD_STATE, SSD_INNER), F32),
                        pltpu.VMEM((8 + SSD_CHUNK, SSD_CONV_DIM), F32)],
        compiler_params=pltpu.CompilerParams(
            dimension_semantics=("parallel", "arbitrary"), vmem_limit_bytes=VMEM_LIMIT),
        name="ssd_mixer",
    )(*args)


def _moba(proj, bsz, seq):
    t = bsz * seq
    in_specs, args = _moba_operands(proj, seq)
    return pl.pallas_call(
        functools.partial(_moba_body, nb=seq // MOBA_BLOCK, q_tile=MOBA_Q_TILE),
        grid=(bsz, MOBA_HEADS),
        in_specs=in_specs,
        out_specs=pl.BlockSpec((seq, MOBA_HEAD_DIM), lambda b, h: (b, h)),
        out_shape=jax.ShapeDtypeStruct((t, MOBA_INNER), BF16),
        compiler_params=pltpu.CompilerParams(
            dimension_semantics=("parallel", "parallel"), vmem_limit_bytes=VMEM_LIMIT),
        name="moba_attn",
    )(*args)


def _out_proj_body(*refs, n_plain, final_norm):
    plain = refs[:n_plain]
    a_ref, g_ref, x_ref, wf_ref = refs[n_plain:n_plain + 4]
    o_ref, w_ref = refs[-2:]

    @pl.when(pl.program_id(0) == 0)
    def _():
        w_ref[...] = wf_ref[...].astype(BF16)

    g = g_ref[...].astype(F32)
    gated = (a_ref[...].astype(F32) * (g * _sigmoid(g))).astype(BF16)
    k0 = 0
    y = x_ref[...]
    for r in plain:
        kw = r.shape[1]
        y = y + _dot(r[...], w_ref[k0:k0 + kw, :])
        k0 += kw
    y = y + _dot(gated, w_ref[k0:k0 + gated.shape[1], :])
    if final_norm:
        fn_ref = refs[-3]
        y = y * lax.rsqrt(jnp.mean(y * y, axis=-1, keepdims=True) + NORM_EPS) * fn_ref[...]
    o_ref[...] = y


def _out_proj(plain, gated, gate_src, gate_col, x2d, w, final_gain=None, *, tm=OUT_PROJ_TM):
    t, d = x2d.shape
    kg = gated.shape[1]
    in_specs = [pl.BlockSpec((tm, a.shape[1]), lambda i: (i, 0)) for a in plain]
    in_specs += [pl.BlockSpec((tm, kg), lambda i: (i, 0)),
                 pl.BlockSpec((tm, kg), lambda i: (i, gate_col // kg)),
                 pl.BlockSpec((tm, d), lambda i: (i, 0)),
                 pl.BlockSpec(w.shape, lambda i: (0, 0), pipeline_mode=pl.Buffered(1))]
    args = list(plain) + [gated, gate_src, x2d, w.astype(F32)]
    if final_gain is not None:
        in_specs.append(pl.BlockSpec((1, d), lambda i: (0, 0)))
        args.append(final_gain.reshape(1, d).astype(F32))
    return pl.pallas_call(
        functools.partial(_out_proj_body, n_plain=len(plain), final_norm=final_gain is not None),
        grid=(t // tm,),
        in_specs=in_specs,
        out_specs=pl.BlockSpec((tm, d), lambda i: (i, 0)),
        out_shape=jax.ShapeDtypeStruct((t, d), F32),
        scratch_shapes=[pltpu.VMEM(w.shape, BF16)],
        compiler_params=pltpu.CompilerParams(
            dimension_semantics=("arbitrary",), vmem_limit_bytes=VMEM_LIMIT),
        name="out_proj_final" if final_gain is not None else "out_proj",
    )(*args)


DIL_PHASES = 8


def _dil_tiles(seq):
    ph, qb = DIL_PHASES, DIL_QBLOCK
    per = seq // ph
    assert [r for _, r in DIL_PATTERNS] == [1, 4, 16] and all(w // r == qb for w, r in DIL_PATTERNS)
    assert seq == DIL_PATTERNS[-1][0]
    tiles = []
    rows = qb // ph
    for n in range(seq // qb):
        qs = [(p * per + rows * n, rows) for p in range(ph)]
        if n == 0:
            ks, mask = [(p * per, rows) for p in range(ph)], "a_first"
        else:
            ks, mask = [(p * per + rows * (n - 1), 2 * rows) for p in range(ph)], "a"
        tiles.append((qs, ks, mask, "init"))
    seg = ph // 4
    rows = qb // seg
    for p4 in range(4):
        for n in range(seq // 4 // qb):
            qs = [((p4 + 4 * j) * per + rows * n, rows) for j in range(seg)]
            if n == 0:
                ks, mask = [((p4 + 4 * j) * per, rows) for j in range(seg)], "b_first"
            else:
                ks, mask = [((p4 + 4 * j) * per + rows * (n - 1), 2 * rows) for j in range(seg)], "b"
            tiles.append((qs, ks, mask, "merge"))
    for p in range(ph):
        for h in range(per // qb):
            qs = [(p * per + qb * h, qb)]
            ks = [(p * per, qb * (h + 1))]
            tiles.append((qs, ks, "c%d" % h, "final"))
    return tiles


def _dil_masks(seq):
    ph, qb = DIL_PHASES, DIL_QBLOCK
    per = seq // ph

    def pos(n_rows, seg_rows, step):
        i = np.arange(n_rows)
        return step * (i % seg_rows) + i // seg_rows

    def band(qpos, kpos, span):
        dist = qpos[:, None] - kpos[None, :]
        return np.where((dist >= 0) & (dist <= span), 0.0, NEG_BIG).astype(np.float32)

    masks = {}
    for name, seg in (("a", ph), ("b", ph // 4)):
        rows = qb // seg
        masks[name] = band(qb + pos(qb, rows, seg), pos(2 * qb, 2 * rows, seg), qb)
        masks[name + "_first"] = band(pos(qb, rows, seg), pos(qb, rows, seg), qb)
    for h in range(per // qb):
        lq = qb * h + np.arange(qb)
        lk = np.arange(qb * (h + 1))
        same = (lq[:, None] - lk[None, :]) % 2 == 0
        masks["c%d" % h] = np.where(same & (lk[None, :] <= lq[:, None]), 0.0, NEG_BIG).astype(np.float32)
    return masks


def _dil_body(*refs, seq, mask_names):
    q_ref, k_ref, v_ref = refs[:3]
    mask_refs = dict(zip(mask_names, refs[3:3 + len(mask_names)]))
    o_ref, onat_ref = refs[3 + len(mask_names):][:2]
    state_a, state_b = refs[-6:-3], refs[-3:]
    state_in = {"merge": state_a, "final": state_b}
    state_out = {"init": state_a, "merge": state_b}
    qb = DIL_QBLOCK
    d = DIL_HEAD_DIM

    def rows_of(ref, slices):
        parts = [ref[s:s + n, :] for s, n in slices]
        return parts[0] if len(parts) == 1 else jnp.concatenate(parts, axis=0)

    def put_rows(ref, slices, val):
        r0 = 0
        for s, n in slices:
            ref[s:s + n, :] = val[r0:r0 + n, :]
            r0 += n

    def scores(tile):
        qs, ks, mask, _ = tile
        return _nt_dot(rows_of(q_ref, qs), rows_of(k_ref, ks)) + mask_refs[mask][...]

    def probs(tile, s):
        qs, _, _, stage = tile
        m_new = jnp.broadcast_to(jnp.max(s, axis=-1, keepdims=True), (s.shape[0], d))
        m_old = None
        if stage != "init":
            m_old = rows_of(state_in[stage][1], qs)
            m_new = jnp.maximum(m_old, m_new)
        p = jnp.concatenate([jnp.exp2(s[:, c0:c0 + d] - m_new) for c0 in range(0, s.shape[1], d)], axis=1)
        return p.astype(BF16), m_new, m_old

    def accumulate(tile, p, m_new, m_old):
        qs, ks, _, stage = tile
        vb = rows_of(v_ref, ks)
        pv = _dot(p, jnp.concatenate([vb, jnp.ones_like(vb)], axis=1))
        acc, l = pv[:, :d], pv[:, d:]
        if stage != "init":
            alpha = jnp.exp2(m_old - m_new)
            acc_ref, _, l_ref = state_in[stage]
            acc = alpha * rows_of(acc_ref, qs) + acc
            l = alpha * rows_of(l_ref, qs) + l
        if stage == "final":
            (s0, _), = qs
            p_idx, l0 = divmod(s0, seq // DIL_PHASES)
            onat_ref[pl.ds(DIL_PHASES * l0 + p_idx, qb, stride=DIL_PHASES), :] = acc / l
        else:
            acc_out, m_out, l_out = state_out[stage]
            put_rows(acc_out, qs, acc)
            put_rows(l_out, qs, l)
            put_rows(m_out, qs, m_new)

    tiles = _dil_tiles(seq)

    def owned(tile):
        return {r for s0, n in tile[0] for r in range(s0, s0 + n)}

    for ta, tb in zip(tiles, tiles[1:]):
        assert not owned(ta) & owned(tb)
    scored, ready = None, None
    for tile in tiles + [None, None]:
        s_new = None if tile is None else (tile, scores(tile))
        p_new = None if scored is None else (scored[0],) + probs(*scored)
        if ready is not None:
            accumulate(*ready)
        scored, ready = s_new, p_new
    o_ref[...] = onat_ref[...].astype(o_ref.dtype)


def _dilated(proj, bsz, seq):
    t = bsz * seq
    d = DIL_HEAD_DIM
    masks = _dil_masks(seq)
    names = tuple(sorted(masks))

    def head_block(base):
        return pl.BlockSpec((seq, d), lambda b, h: (b, base // d + h))

    return pl.pallas_call(
        functools.partial(_dil_body, seq=seq, mask_names=names),
        grid=(bsz, DIL_HEADS),
        in_specs=[head_block(0), head_block(DIL_INNER), head_block(2 * DIL_INNER)]
        + [pl.BlockSpec(masks[n].shape, lambda b, h: (0, 0)) for n in names],
        out_specs=pl.BlockSpec((seq, d), lambda b, h: (b, h)),
        out_shape=jax.ShapeDtypeStruct((t, DIL_INNER), BF16),
        scratch_shapes=[pltpu.VMEM((seq, d), F32)] * 7,
        compiler_params=pltpu.CompilerParams(
            dimension_semantics=("parallel", "parallel"), vmem_limit_bytes=VMEM_LIMIT),
        name="dilated_attn",
    )(proj, proj, proj, *[jnp.asarray(masks[n]) for n in names])


def kernel(x, even_norm, even_w_in, ssd_conv_w, ssd_conv_b, ssd_dt_bias, ssd_a_log, ssd_d, ssd_norm,
           even_w_out, odd_norm, odd_w_in, odd_w_out, final_norm):
    bsz, seq, d = x.shape
    t = bsz * seq
    x2d = x.reshape(t, d)

    w = even_w_in[0]
    z_end = SSD_INNER
    xbc_end = z_end + SSD_CONV_DIM
    dt_end = xbc_end + SSD_HEADS
    q_end = dt_end + MOBA_INNER
    v_end = q_end + 2 * MOBA_INNER
    x_end = z_end + SSD_INNER
    moba_scale = MOBA_HEAD_DIM ** -0.5 * math.log2(math.e)
    row_scale = jnp.ones((w.shape[1], 1), F32).at[dt_end:q_end].set(moba_scale)
    g0 = even_norm[0].astype(F32)[None, :]
    w_t = (w.T * row_scale * g0).astype(BF16)
    w_dt = jnp.pad((w.T[xbc_end:dt_end] * g0).astype(BF16), ((0, LANES - SSD_HEADS), (0, 0)))
    col_pieces = _split_ranges([(0, x_end), (v_end, w.shape[1]), (dt_end, v_end), (x_end, xbc_end)],
                               P0_N // PROJ0_COL_TILES)
    proj0, dt_raw = _norm_proj_t(x2d, w_t, w_dt, col_pieces, tm=seq // 2)
    y_ssd = _ssd(proj0, dt_raw, ssd_conv_w[0], ssd_conv_b[0], ssd_dt_bias[0], ssd_a_log[0],
                 ssd_d[0], ssd_norm[0], bsz, seq)
    o_moba = _moba(proj0, bsz, seq)
    x1 = _out_proj([y_ssd], o_moba, proj0, P0_G, x2d, even_w_out[0])

    w1 = odd_w_in[0]
    dil_scale = DIL_HEAD_DIM ** -0.5 * math.log2(math.e)
    col_scale1 = jnp.ones((1, w1.shape[1]), F32).at[:, :DIL_INNER].set(dil_scale)
    w1_b = (w1 * col_scale1 * odd_norm[0].astype(F32)[:, None]).astype(BF16)
    proj1 = _norm_proj_phased(x1, w1_b, tm=seq, tn=PROJ1_TN, phases=DIL_PHASES, token_cols=DIL_INNER)
    o_dil = _dilated(proj1, bsz, seq)
    out = _out_proj([], o_dil, proj1, 3 * DIL_INNER, x1, odd_w_out[0], final_norm)
    return out.reshape(bsz, seq, d)
```

```python
import functools
import math

import jax
import jax.numpy as jnp
import numpy as np
from jax import lax
from jax.experimental import pallas as pl
from jax.experimental.pallas import tpu as pltpu

F32 = jnp.float32
BF16 = jnp.bfloat16

NORM_EPS = 1e-5

SSD_HEADS = 16
SSD_HEAD_DIM = 64
SSD_INNER = SSD_HEADS * SSD_HEAD_DIM
SSD_GROUPS = 2
SSD_STATE = 128
SSD_CONV = 4
SSD_CHUNK = 128
SSD_BC = 2 * SSD_GROUPS * SSD_STATE
SSD_CONV_DIM = SSD_INNER + SSD_BC

MOBA_HEADS = 8
MOBA_HEAD_DIM = 128
MOBA_INNER = MOBA_HEADS * MOBA_HEAD_DIM
MOBA_BLOCK = 256
MOBA_TOPK = 3

DIL_HEADS = 16
DIL_HEAD_DIM = 128
DIL_INNER = DIL_HEADS * DIL_HEAD_DIM
DIL_QBLOCK = 128
DIL_PATTERNS = ((128, 1), (512, 4), (2048, 16))

LANES = 128
MXU_WIDTH = 256
NEG_BIG = -1e30
VMEM_LIMIT = 56 * 1024 * 1024

PROJ_ROW_SUB = 512
PROJ0_COL_TILES = 2
PROJ1_TN = 4 * MXU_WIDTH
OUT_PROJ_TM = 1024
SSD_CHUNKS_PER_STEP = 4
MOBA_Q_TILE = 256

P0_Z = 0
P0_X = P0_Z + SSD_INNER
P0_G = P0_X + SSD_INNER
P0_Q = P0_G + MOBA_INNER
P0_K = P0_Q + MOBA_INNER
P0_V = P0_K + MOBA_INNER
P0_BC = P0_V + MOBA_INNER
P0_N = P0_BC + SSD_BC


def _nt_dot(a, b):
    return lax.dot_general(a, b, (((1,), (1,)), ((), ())), preferred_element_type=F32)


def _dot(a, b):
    return jnp.dot(a, b, preferred_element_type=F32)


def _sigmoid(x):
    return 1.0 / (1.0 + jnp.exp2(x * -math.log2(math.e)))


def _norm_proj_body(*refs, kind, phases=0, n_token_tiles=0, col_pieces=()):
    if kind == "aux":
        x_ref, w_ref, waux_ref, o_ref, oaux_ref, hn_ref = refs
    else:
        x_ref, w_ref, o_ref, rows_ref, hnf_ref = refs
    j = pl.program_id(1)
    tm = x_ref.shape[0]

    if kind == "aux":
        sub = min(tm, PROJ_ROW_SUB)

        def normed(r0):
            x = x_ref[r0:r0 + sub, :]
            ms = jnp.mean(x * x, axis=-1, keepdims=True)
            return (x * lax.rsqrt(ms + NORM_EPS)).astype(BF16)

        starts = list(range(0, tm, sub))

        def column_step(pieces):
            nxt = normed(starts[0])
            for k, r0 in enumerate(starts):
                cur = nxt
                if k + 1 < len(starts):
                    nxt = normed(starts[k + 1])
                hn_ref[r0:r0 + sub, :] = cur
                c0 = 0
                for a, b in pieces:
                    o_ref[r0:r0 + sub, c0:c0 + b - a] = _nt_dot(cur, w_ref[a:b, :]).astype(o_ref.dtype)
                    c0 += b - a

        for step, pieces in enumerate(col_pieces):
            @pl.when(j == step)
            def _(pieces=pieces):
                column_step(pieces)

        @pl.when(j == 0)
        def _():
            oaux_ref[...] = _nt_dot(hn_ref[...], waux_ref[...])

        return

    d = x_ref.shape[1]
    sub = min(tm, PROJ_ROW_SUB)
    starts = list(range(0, tm, sub))
    per = tm // phases
    assert n_token_tiles >= 2 and phases % len(starts) == 0

    def normed(r0):
        x = x_ref[r0:r0 + sub, :]
        ms = jnp.mean(x * x, axis=-1, keepdims=True)
        return x * lax.rsqrt(ms + NORM_EPS)

    def token_rows(side_work):
        nxt = normed(starts[0])
        for k, r0 in enumerate(starts):
            cur = nxt
            if k + 1 < len(starts):
                nxt = normed(starts[k + 1])
            side_work(k, r0, cur)
            o_ref[r0:r0 + sub, :] = _dot(cur.astype(BF16), w_ref[...]).astype(o_ref.dtype)

    def keep(k, r0, hn):
        rows_ref[0, r0:r0 + sub, :] = hn.astype(BF16)
        for cc in range(d // LANES):
            hnf_ref[cc, r0:r0 + sub, :] = hn[:, cc * LANES:(cc + 1) * LANES]

    def build(k):
        for p in range(2 * k, 2 * k + 2):
            for cc in range(d // LANES):
                rows_ref[1, p * per:(p + 1) * per, cc * LANES:(cc + 1) * LANES] = (
                    hnf_ref[cc, pl.ds(p, per, stride=phases), :].astype(BF16))

    assert n_token_tiles == 2 and phases == 2 * len(starts)

    @pl.when(j == 0)
    def _():
        token_rows(keep)

    @pl.when(j == n_token_tiles)
    def _():
        build(0)
        for k, r0 in enumerate(starts):
            if k + 1 < len(starts):
                build(k + 1)
            o_ref[r0:r0 + sub, :] = _dot(rows_ref[1, r0:r0 + sub, :], w_ref[...]).astype(o_ref.dtype)

    @pl.when((j != 0) & (j != n_token_tiles))
    def _():
        order = jnp.where(j < n_token_tiles, 0, 1)
        for r0 in starts:
            o_ref[r0:r0 + sub, :] = _dot(rows_ref[order, r0:r0 + sub, :], w_ref[...]).astype(o_ref.dtype)


def _split_ranges(ranges, width):
    groups, room = [[]], width
    for a, b in ranges:
        while a < b:
            take = min(b - a, room)
            groups[-1].append((a, a + take))
            a, room = a + take, room - take
            if room == 0:
                groups.append([])
                room = width
    assert not groups[-1] and room == width
    return tuple(tuple(g) for g in groups[:-1])


def _norm_proj_t(x2d, w_t, w_aux_t, col_pieces, *, tm):
    t, d = x2d.shape
    widths = {sum(b - a for a, b in pieces) for pieces in col_pieces}
    (tn,) = widths
    n = tn * len(col_pieces)
    return pl.pallas_call(
        functools.partial(_norm_proj_body, kind="aux", col_pieces=col_pieces),
        grid=(t // tm, len(col_pieces)),
        in_specs=[pl.BlockSpec((tm, d), lambda i, j: (i, 0)),
                  pl.BlockSpec(w_t.shape, lambda i, j: (0, 0), pipeline_mode=pl.Buffered(1)),
                  pl.BlockSpec((LANES, d), lambda i, j: (0, 0))],
        out_specs=[pl.BlockSpec((tm, tn), lambda i, j: (i, j)),
                   pl.BlockSpec((tm, LANES), lambda i, j: (i, 0))],
        out_shape=[jax.ShapeDtypeStruct((t, n), BF16), jax.ShapeDtypeStruct((t, LANES), F32)],
        scratch_shapes=[pltpu.VMEM((tm, d), BF16)],
        compiler_params=pltpu.CompilerParams(
            dimension_semantics=("parallel", "arbitrary"), vmem_limit_bytes=VMEM_LIMIT),
        name="norm_proj_aux",
    )(x2d, w_t, w_aux_t)


def _norm_proj_phased(x2d, w, *, tm, tn, phases, token_cols):
    t, d = x2d.shape
    n = w.shape[1]
    nt, n_token_tiles = n // tn, token_cols // tn

    def col_tile(j):
        return (j + nt - n_token_tiles) % nt

    return pl.pallas_call(
        functools.partial(_norm_proj_body, kind="phased", phases=phases, n_token_tiles=n_token_tiles),
        grid=(t // tm, nt),
        in_specs=[pl.BlockSpec((tm, d), lambda i, j: (i, 0)),
                  pl.BlockSpec((d, tn), lambda i, j: (0, col_tile(j)))],
        out_specs=pl.BlockSpec((tm, tn), lambda i, j: (i, col_tile(j))),
        out_shape=jax.ShapeDtypeStruct((t, n), BF16),
        scratch_shapes=[pltpu.VMEM((2, tm, d), BF16), pltpu.VMEM((d // LANES, tm, LANES), F32)],
        compiler_params=pltpu.CompilerParams(
            dimension_semantics=("parallel", "arbitrary"), vmem_limit_bytes=VMEM_LIMIT),
        name="norm_proj",
    )(x2d, w)


def _split3(a):
    hi = a.astype(BF16)
    r = a - hi.astype(F32)
    mid = r.astype(BF16)
    lo = (r - mid.astype(F32)).astype(BF16)
    return hi, mid, lo


def _expand_heads(a, e):
    hi, mid, lo = _split3(a)
    return _dot(hi, e) + _dot(mid, e) + _dot(lo, e)


def _ssd_body(z_ref, xs_ref, bc_ref, dt_ref, cw_ref, cb_ref, dtb_ref, alog_ref, dskip_ref, nrm_ref,
              e_ref, tri_ref, y_ref, state_ref, ubuf_ref):
    c = pl.program_id(1)
    L = SSD_CHUNK
    tail = 8

    @pl.when(c == 0)
    def _():
        state_ref[...] = jnp.zeros_like(state_ref)
        ubuf_ref[0:tail, :] = jnp.zeros((tail, SSD_CONV_DIM), F32)

    row = lax.broadcasted_iota(jnp.int32, (L, L), 0)
    col = lax.broadcasted_iota(jnp.int32, (L, L), 1)
    causal = col <= row
    lane = lax.broadcasted_iota(jnp.int32, (L, LANES), 1)
    low_half = lane < SSD_HEAD_DIM
    heads_per_group = SSD_HEADS // SSD_GROUPS
    gw = heads_per_group * SSD_HEAD_DIM

    def chunk(r):
        acts = []
        cw = cw_ref[...]
        for c0 in range(0, SSD_CONV_DIM, SSD_BC):
            cols = slice(c0, c0 + SSD_BC)
            src = xs_ref[r, cols] if c0 < SSD_INNER else bc_ref[r, :]
            u = src.astype(F32)
            ubuf_ref[tail:tail + L, cols] = u
            acc = cb_ref[:, cols] + cw[SSD_CONV - 1:SSD_CONV, cols] * u
            for k in range(SSD_CONV - 1):
                acc = acc + cw[k:k + 1, cols] * ubuf_ref[pl.ds(tail - (SSD_CONV - 1) + k, L), cols]
            ubuf_ref[0:tail, cols] = u[L - tail:L, :]
            acts.append(acc * _sigmoid(acc))
        act = jnp.concatenate(acts, axis=1)

        xs = act[:, :SSD_INNER]

        dtr = dt_ref[r, :] + dtb_ref[...]
        dt = jnp.maximum(dtr, 0.0) + jnp.log1p(jnp.exp(-jnp.abs(dtr)))
        a = -jnp.exp(alog_ref[...]) * math.log2(math.e)
        ac = dt * a
        tri = tri_ref[...]
        hi, mid, lo = _split3(ac)
        a_cum = _dot(tri, hi) + _dot(tri, mid) + _dot(tri, lo)
        a_last = a_cum[L - 1:L, :]
        dec_states = jnp.exp2(a_last - a_cum)
        exp_acum = jnp.exp2(a_cum)
        e = e_ref[...]
        dt_e = _expand_heads(dt, e)
        dec_e = _expand_heads(dec_states, e)
        ea_e = _expand_heads(exp_acum, e)
        a_cum_t = a_cum.T

        xc = xs * dt_e
        xc_b = xc.astype(BF16)
        xdec_b = (xc * dec_e).astype(BF16)

        for g in range(SSD_GROUPS):
            b_g = act[:, SSD_INNER + g * SSD_STATE:SSD_INNER + (g + 1) * SSD_STATE]
            c_g = act[:, SSD_INNER + (SSD_GROUPS + g) * SSD_STATE:
                      SSD_INNER + (SSD_GROUPS + g + 1) * SSD_STATE]
            b_gb = b_g.astype(BF16)
            c_gb = c_g.astype(BF16)
            scores = _nt_dot(c_gb, b_gb)
            gs = slice(g * gw, (g + 1) * gw)

            prev = state_ref[:, gs]
            y_off = _dot(c_gb, prev.astype(BF16))
            st_new = _dot(b_g.T.astype(BF16), xdec_b[:, gs])
            state_ref[:, gs] = prev * ea_e[L - 1:L, gs] + st_new

            pieces = []
            for pair in range(heads_per_group // 2):
                ms = []
                for hh in range(2):
                    h = g * heads_per_group + 2 * pair + hh
                    seg = a_cum[:, h:h + 1] - a_cum_t[h:h + 1, :]
                    lmat = jnp.exp2(jnp.where(causal, seg, NEG_BIG))
                    ms.append((scores * lmat).astype(BF16))
                m_pair = jnp.concatenate(ms, axis=1)
                cs = slice(g * gw + pair * LANES, g * gw + (pair + 1) * LANES)
                x_pair = xc_b[:, cs]
                zero = jnp.zeros_like(x_pair)
                rhs = jnp.concatenate([jnp.where(low_half, x_pair, zero),
                                       jnp.where(low_half, zero, x_pair)], axis=0)
                pieces.append(_dot(m_pair, rhs))
            y_diag = jnp.concatenate(pieces, axis=1)

            y = y_diag + y_off * ea_e[:, gs] + xs[:, gs] * dskip_ref[:, gs]
            zg = z_ref[r, gs].astype(F32)
            ug = y * (zg * _sigmoid(zg))
            ug = ug * lax.rsqrt(jnp.mean(ug * ug, axis=-1, keepdims=True) + NORM_EPS)
            y_ref[r, gs] = (ug * nrm_ref[:, gs]).astype(y_ref.dtype)

    for ci in range(xs_ref.shape[0] // L):
        chunk(slice(ci * L, (ci + 1) * L))


def _ssd_operands(proj, dt_raw, conv_w, conv_b, dt_bias, a_log, d_skip, ssd_norm, seq, R):
    L = SSD_CHUNK
    nc = seq // R

    def pad_heads(v):
        return jnp.pad(v.astype(F32), (0, LANES - SSD_HEADS)).reshape(1, LANES)

    head_of_chan = np.arange(SSD_INNER) // SSD_HEAD_DIM
    expand = jnp.asarray(np.arange(LANES)[:, None] == head_of_chan[None, :], BF16)
    tri = jnp.asarray(np.arange(L)[:, None] >= np.arange(L)[None, :], BF16)
    d_chan = jnp.repeat(d_skip.astype(F32), SSD_HEAD_DIM).reshape(1, SSD_INNER)

    def rows(b, c):
        return b * nc + c

    const = lambda b, c: (0, 0)
    in_specs = [
            pl.BlockSpec((R, SSD_INNER), lambda b, c: (rows(b, c), P0_Z // SSD_INNER)),
            pl.BlockSpec((R, SSD_INNER), lambda b, c: (rows(b, c), P0_X // SSD_INNER)),
            pl.BlockSpec((R, SSD_BC), lambda b, c: (rows(b, c), P0_BC // SSD_BC)),
            pl.BlockSpec((R, LANES), lambda b, c: (rows(b, c), 0)),
            pl.BlockSpec((SSD_CONV, SSD_CONV_DIM), const),
            pl.BlockSpec((1, SSD_CONV_DIM), const),
            pl.BlockSpec((1, LANES), const),
            pl.BlockSpec((1, LANES), const),
            pl.BlockSpec((1, SSD_INNER), const),
            pl.BlockSpec((1, SSD_INNER), const),
            pl.BlockSpec((LANES, SSD_INNER), const),
            pl.BlockSpec((L, L), const),
    ]
    args = (proj, proj, proj, dt_raw, conv_w.astype(F32), conv_b.reshape(1, -1).astype(F32),
            pad_heads(dt_bias), pad_heads(a_log), d_chan, ssd_norm.reshape(1, -1).astype(F32), expand, tri)
    return in_specs, args


def _moba_body(q_ref, k_ref, v_ref, g_ref, avg_ref, hot_ref, eye_ref, o_ref, *, nb, q_tile):
    blk = MOBA_BLOCK
    n_dense = MOBA_TOPK + 1

    kmean = _dot(avg_ref[...], k_ref[...]).astype(BF16)
    jrow = lax.broadcasted_iota(jnp.int32, (nb, blk), 0)
    row = lax.broadcasted_iota(jnp.int32, (q_tile, blk), 0)
    col = lax.broadcasted_iota(jnp.int32, (q_tile, blk), 1)

    def query_block(i):
        q_i = q_ref[i * blk:(i + 1) * blk, :]
        if i < n_dense:
            return q_i, None
        gate_t = _nt_dot(kmean, q_i)[0:nb, :]
        valid = jrow < i
        gm = jnp.where(valid, gate_t, -jnp.inf)
        rank = jnp.zeros((nb, blk), F32)
        for r in range(1, nb):
            gr = pltpu.roll(gm, r, axis=0)
            lower = ((jrow - r) & (nb - 1)) < jrow
            rank = rank + jnp.where(gr > gm, 1.0, 0.0) + jnp.where((gr == gm) & lower, 1.0, 0.0)
        keep = valid & (rank < float(MOBA_TOPK))
        bias_t = jnp.where(keep, 0.0, NEG_BIG)
        bias_t = jnp.concatenate([bias_t, jnp.zeros((LANES - nb, blk), F32)], axis=0).astype(BF16)
        bias = _nt_dot(eye_ref[...], bias_t).astype(BF16)
        return q_i, jnp.concatenate([q_i, bias], axis=1)

    def scores(i, h, q_i, q_aug):
        rs = slice(h * q_tile, (h + 1) * q_tile)
        s_own = _nt_dot(q_i[rs, :], k_ref[i * blk:(i + 1) * blk, :])
        s_own = jnp.where(col <= row + h * q_tile, s_own, NEG_BIG)
        if i == 0:
            return s_own, None
        if q_aug is None:
            return s_own, _nt_dot(q_i[rs, :], k_ref[0:i * blk, :])
        k_past = jnp.concatenate([k_ref[0:i * blk, :], hot_ref[0:i * blk, :]], axis=1)
        return s_own, _nt_dot(q_aug[rs, :], k_past)

    def finish(i, h, s_own, s_past):
        m = jnp.max(s_own, axis=-1, keepdims=True)
        if s_past is None:
            p = jnp.exp2(s_own - m)
        else:
            m = jnp.maximum(m, jnp.max(s_past, axis=-1, keepdims=True))
            p = jnp.concatenate([jnp.exp2(s_past - m), jnp.exp2(s_own - m)], axis=1)
        vb = v_ref[0:(i + 1) * blk, :]
        pv = _dot(p.astype(BF16), jnp.concatenate([vb, jnp.ones_like(vb)], axis=1))
        d = vb.shape[1]
        rows = slice(i * blk + h * q_tile, i * blk + (h + 1) * q_tile)
        g = g_ref[rows, :].astype(F32)
        o_ref[rows, :] = ((pv[:, :d] / pv[:, d:]) * (g * _sigmoid(g))).astype(o_ref.dtype)

    tiles = [(i, h) for i in range(nb) for h in range(blk // q_tile)]
    qcache = {}
    pending = None
    for i, h in tiles:
        if i not in qcache:
            qcache = {i: query_block(i)}
        s_next = scores(i, h, *qcache[i])
        if pending is not None:
            finish(*pending)
        pending = (i, h) + s_next
    finish(*pending)


def _moba_operands(proj, seq):
    blk = MOBA_BLOCK
    nb = seq // blk
    d = MOBA_HEAD_DIM
    blk_of_key = np.arange(seq) // blk
    avg = jnp.asarray((np.arange(2 * nb)[:, None] == blk_of_key[None, :]) / blk, BF16)
    hot = jnp.asarray(blk_of_key[:, None] == np.arange(LANES)[None, :], BF16)
    eye = jnp.asarray(np.eye(blk), BF16)

    def head_block(base):
        return pl.BlockSpec((seq, d), lambda b, h: (b, base // d + h))

    const = lambda b, h: (0, 0)
    in_specs = [head_block(P0_Q), head_block(P0_K), head_block(P0_V), head_block(P0_G),
                pl.BlockSpec((2 * nb, seq), const), pl.BlockSpec((seq, LANES), const),
                pl.BlockSpec((blk, blk), const)]
    return in_specs, (proj, proj, proj, proj, avg, hot, eye)


def _ssd(proj, dt_raw, conv_w, conv_b, dt_bias, a_log, d_skip, ssd_norm, bsz, seq, *, chunks_per_step=SSD_CHUNKS_PER_STEP):
    t = bsz * seq
    R = chunks_per_step * SSD_CHUNK
    in_specs, args = _ssd_operands(proj, dt_raw, conv_w, conv_b, dt_bias, a_log, d_skip, ssd_norm, seq, R)
    nc = seq // R
    return pl.pallas_call(
        _ssd_body,
        grid=(bsz, nc),
        in_specs=in_specs,
        out_specs=pl.BlockSpec((R, SSD_INNER), lambda b, c: (b * nc + c, 0)),
        out_shape=jax.ShapeDtypeStruct((t, SSD_INNER), BF16),
        scratch_shapes=[pltpu.VMEM((SSD_STATE, SSD_INNER), F32),
                        pltpu.VMEM((8 + SSD_CHUNK, SSD_CONV_DIM), F32)],
        compiler_params=pltpu.CompilerParams(
            dimension_semantics=("parallel", "arbitrary"), vmem_limit_bytes=VMEM_LIMIT),
        name="ssd_mixer",
    )(*args)


def _moba(proj, bsz, seq):
    t = bsz * seq
    in_specs, args = _moba_operands(proj, seq)
    return pl.pallas_call(
        functools.partial(_moba_body, nb=seq // MOBA_BLOCK, q_tile=MOBA_Q_TILE),
        grid=(bsz, MOBA_HEADS),
        in_specs=in_specs,
        out_specs=pl.BlockSpec((seq, MOBA_HEAD_DIM), lambda b, h: (b, h)),
        out_shape=jax.ShapeDtypeStruct((t, MOBA_INNER), BF16),
        compiler_params=pltpu.CompilerParams(
            dimension_semantics=("parallel", "parallel"), vmem_limit_bytes=VMEM_LIMIT),
        name="moba_attn",
    )(*args)


def _out_proj_body(*refs, n_plain, has_gate, final_norm):
    plain = refs[:n_plain]
    n_act = n_plain + (2 if has_gate else 0)
    x_ref, wf_ref = refs[n_act:n_act + 2]
    o_ref, w_ref = refs[-2:]

    @pl.when(pl.program_id(0) == 0)
    def _():
        w_ref[...] = wf_ref[...].astype(BF16)

    k0 = 0
    y = x_ref[...]
    for r in plain:
        kw = r.shape[1]
        y = y + _dot(r[...], w_ref[k0:k0 + kw, :])
        k0 += kw
    if has_gate:
        a_ref, g_ref = refs[n_plain:n_act]
        g = g_ref[...].astype(F32)
        gated = (a_ref[...].astype(F32) * (g * _sigmoid(g))).astype(BF16)
        y = y + _dot(gated, w_ref[k0:k0 + gated.shape[1], :])
    if final_norm:
        fn_ref = refs[-3]
        y = y * lax.rsqrt(jnp.mean(y * y, axis=-1, keepdims=True) + NORM_EPS) * fn_ref[...]
    o_ref[...] = y


def _out_proj(plain, gated, gate_src, gate_col, x2d, w, final_gain=None, *, tm=OUT_PROJ_TM):
    t, d = x2d.shape
    in_specs = [pl.BlockSpec((tm, a.shape[1]), lambda i: (i, 0)) for a in plain]
    args = list(plain)
    if gated is not None:
        kg = gated.shape[1]
        in_specs += [pl.BlockSpec((tm, kg), lambda i: (i, 0)),
                     pl.BlockSpec((tm, kg), lambda i: (i, gate_col // kg))]
        args += [gated, gate_src]
    in_specs += [pl.BlockSpec((tm, d), lambda i: (i, 0)),
                 pl.BlockSpec(w.shape, lambda i: (0, 0), pipeline_mode=pl.Buffered(1))]
    args += [x2d, w.astype(F32)]
    if final_gain is not None:
        in_specs.append(pl.BlockSpec((1, d), lambda i: (0, 0)))
        args.append(final_gain.reshape(1, d).astype(F32))
    return pl.pallas_call(
        functools.partial(_out_proj_body, n_plain=len(plain), has_gate=gated is not None,
                          final_norm=final_gain is not None),
        grid=(t // tm,),
        in_specs=in_specs,
        out_specs=pl.BlockSpec((tm, d), lambda i: (i, 0)),
        out_shape=jax.ShapeDtypeStruct((t, d), F32),
        scratch_shapes=[pltpu.VMEM(w.shape, BF16)],
        compiler_params=pltpu.CompilerParams(
            dimension_semantics=("arbitrary",), vmem_limit_bytes=VMEM_LIMIT),
        name="out_proj_final" if final_gain is not None else "out_proj",
    )(*args)


DIL_PHASES = 8


def _dil_tiles(seq):
    ph, qb = DIL_PHASES, DIL_QBLOCK
    per = seq // ph
    assert [r for _, r in DIL_PATTERNS] == [1, 4, 16] and all(w // r == qb for w, r in DIL_PATTERNS)
    assert seq == DIL_PATTERNS[-1][0]
    tiles = []
    rows = qb // ph
    for n in range(seq // qb):
        qs = [(p * per + rows * n, rows) for p in range(ph)]
        if n == 0:
            ks, mask = [(p * per, rows) for p in range(ph)], "a_first"
        else:
            ks, mask = [(p * per + rows * (n - 1), 2 * rows) for p in range(ph)], "a"
        tiles.append((qs, ks, mask, "init"))
    seg = ph // 4
    rows = qb // seg
    for p4 in range(4):
        for n in range(seq // 4 // qb):
            qs = [((p4 + 4 * j) * per + rows * n, rows) for j in range(seg)]
            if n == 0:
                ks, mask = [((p4 + 4 * j) * per, rows) for j in range(seg)], "b_first"
            else:
                ks, mask = [((p4 + 4 * j) * per + rows * (n - 1), 2 * rows) for j in range(seg)], "b"
            tiles.append((qs, ks, mask, "merge"))
    for p in range(ph):
        for h in range(per // qb):
            qs = [(p * per + qb * h, qb)]
            ks = [(p * per, qb * (h + 1))]
            tiles.append((qs, ks, "c%d" % h, "final"))
    return tiles


def _dil_masks(seq):
    ph, qb = DIL_PHASES, DIL_QBLOCK
    per = seq // ph

    def pos(n_rows, seg_rows, step):
        i = np.arange(n_rows)
        return step * (i % seg_rows) + i // seg_rows

    def band(qpos, kpos, span):
        dist = qpos[:, None] - kpos[None, :]
        return np.where((dist >= 0) & (dist <= span), 0.0, NEG_BIG).astype(np.float32)

    masks = {}
    for name, seg in (("a", ph), ("b", ph // 4)):
        rows = qb // seg
        masks[name] = band(qb + pos(qb, rows, seg), pos(2 * qb, 2 * rows, seg), qb)
        masks[name + "_first"] = band(pos(qb, rows, seg), pos(qb, rows, seg), qb)
    for h in range(per // qb):
        lq = qb * h + np.arange(qb)
        lk = np.arange(qb * (h + 1))
        same = (lq[:, None] - lk[None, :]) % 2 == 0
        masks["c%d" % h] = np.where(same & (lk[None, :] <= lq[:, None]), 0.0, NEG_BIG).astype(np.float32)
    return masks


def _dil_body(*refs, seq, mask_names):
    q_ref, k_ref, v_ref = refs[:3]
    mask_refs = dict(zip(mask_names, refs[3:3 + len(mask_names)]))
    o_ref, onat_ref = refs[3 + len(mask_names):][:2]
    state_a, state_b = refs[-6:-3], refs[-3:]
    state_in = {"merge": state_a, "final": state_b}
    state_out = {"init": state_a, "merge": state_b}
    qb = DIL_QBLOCK
    d = DIL_HEAD_DIM

    def rows_of(ref, slices):
        parts = [ref[s:s + n, :] for s, n in slices]
        return parts[0] if len(parts) == 1 else jnp.concatenate(parts, axis=0)

    def put_rows(ref, slices, val):
        r0 = 0
        for s, n in slices:
            ref[s:s + n, :] = val[r0:r0 + n, :]
            r0 += n

    def scores(tile):
        qs, ks, mask, _ = tile
        return _nt_dot(rows_of(q_ref, qs), rows_of(k_ref, ks)) + mask_refs[mask][...]

    def probs(tile, s):
        qs, _, _, stage = tile
        m_new = jnp.broadcast_to(jnp.max(s, axis=-1, keepdims=True), (s.shape[0], d))
        m_old = None
        if stage != "init":
            m_old = rows_of(state_in[stage][1], qs)
            m_new = jnp.maximum(m_old, m_new)
        p = jnp.concatenate([jnp.exp2(s[:, c0:c0 + d] - m_new) for c0 in range(0, s.shape[1], d)], axis=1)
        return p.astype(BF16), m_new, m_old

    def accumulate(tile, p, m_new, m_old):
        qs, ks, _, stage = tile
        vb = rows_of(v_ref, ks)
        pv = _dot(p, jnp.concatenate([vb, jnp.ones_like(vb)], axis=1))
        acc, l = pv[:, :d], pv[:, d:]
        if stage != "init":
            alpha = jnp.exp2(m_old - m_new)
            acc_ref, _, l_ref = state_in[stage]
            acc = alpha * rows_of(acc_ref, qs) + acc
            l = alpha * rows_of(l_ref, qs) + l
        if stage == "final":
            (s0, _), = qs
            p_idx, l0 = divmod(s0, seq // DIL_PHASES)
            onat_ref[pl.ds(DIL_PHASES * l0 + p_idx, qb, stride=DIL_PHASES), :] = acc / l
        else:
            acc_out, m_out, l_out = state_out[stage]
            put_rows(acc_out, qs, acc)
            put_rows(l_out, qs, l)
            put_rows(m_out, qs, m_new)

    tiles = _dil_tiles(seq)

    def owned(tile):
        return {r for s0, n in tile[0] for r in range(s0, s0 + n)}

    for ta, tb in zip(tiles, tiles[1:]):
        assert not owned(ta) & owned(tb)
    scored, ready = None, None
    for tile in tiles + [None, None]:
        s_new = None if tile is None else (tile, scores(tile))
        p_new = None if scored is None else (scored[0],) + probs(*scored)
        if ready is not None:
            accumulate(*ready)
        scored, ready = s_new, p_new
    o_ref[...] = onat_ref[...].astype(o_ref.dtype)


def _dilated(proj, bsz, seq):
    t = bsz * seq
    d = DIL_HEAD_DIM
    masks = _dil_masks(seq)
    names = tuple(sorted(masks))

    def head_block(base):
        return pl.BlockSpec((seq, d), lambda b, h: (b, base // d + h))

    return pl.pallas_call(
        functools.partial(_dil_body, seq=seq, mask_names=names),
        grid=(bsz, DIL_HEADS),
        in_specs=[head_block(0), head_block(DIL_INNER), head_block(2 * DIL_INNER)]
        + [pl.BlockSpec(masks[n].shape, lambda b, h: (0, 0)) for n in names],
        out_specs=pl.BlockSpec((seq, d), lambda b, h: (b, h)),
        out_shape=jax.ShapeDtypeStruct((t, DIL_INNER), BF16),
        scratch_shapes=[pltpu.VMEM((seq, d), F32)] * 7,
        compiler_params=pltpu.CompilerParams(
            dimension_semantics=("parallel", "parallel"), vmem_limit_bytes=VMEM_LIMIT),
        name="dilated_attn",
    )(proj, proj, proj, *[jnp.asarray(masks[n]) for n in names])


def kernel(x, even_norm, even_w_in, ssd_conv_w, ssd_conv_b, ssd_dt_bias, ssd_a_log, ssd_d, ssd_norm,
           even_w_out, odd_norm, odd_w_in, odd_w_out, final_norm):
    bsz, seq, d = x.shape
    t = bsz * seq
    x2d = x.reshape(t, d)

    w = even_w_in[0]
    z_end = SSD_INNER
    xbc_end = z_end + SSD_CONV_DIM
    dt_end = xbc_end + SSD_HEADS
    q_end = dt_end + MOBA_INNER
    v_end = q_end + 2 * MOBA_INNER
    x_end = z_end + SSD_INNER
    moba_scale = MOBA_HEAD_DIM ** -0.5 * math.log2(math.e)
    row_scale = jnp.ones((w.shape[1], 1), F32).at[dt_end:q_end].set(moba_scale)
    g0 = even_norm[0].astype(F32)[None, :]
    w_t = (w.T * row_scale * g0).astype(BF16)
    w_dt = jnp.pad((w.T[xbc_end:dt_end] * g0).astype(BF16), ((0, LANES - SSD_HEADS), (0, 0)))
    col_pieces = _split_ranges([(0, x_end), (v_end, w.shape[1]), (dt_end, v_end), (x_end, xbc_end)],
                               P0_N // PROJ0_COL_TILES)
    proj0, dt_raw = _norm_proj_t(x2d, w_t, w_dt, col_pieces, tm=seq // 2)
    y_ssd = _ssd(proj0, dt_raw, ssd_conv_w[0], ssd_conv_b[0], ssd_dt_bias[0], ssd_a_log[0],
                 ssd_d[0], ssd_norm[0], bsz, seq)
    o_moba = _moba(proj0, bsz, seq)
    x1 = _out_proj([y_ssd, o_moba], None, None, 0, x2d, even_w_out[0])

    w1 = odd_w_in[0]
    dil_scale = DIL_HEAD_DIM ** -0.5 * math.log2(math.e)
    col_scale1 = jnp.ones((1, w1.shape[1]), F32).at[:, :DIL_INNER].set(dil_scale)
    w1_b = (w1 * col_scale1 * odd_norm[0].astype(F32)[:, None]).astype(BF16)
    proj1 = _norm_proj_phased(x1, w1_b, tm=seq, tn=PROJ1_TN, phases=DIL_PHASES, token_cols=DIL_INNER)
    o_dil = _dilated(proj1, bsz, seq)
    out = _out_proj([], o_dil, proj1, 3 * DIL_INNER, x1, odd_w_out[0], final_norm)
    return out.reshape(bsz, seq, d)
```

```python
import functools
import math

import jax
import jax.numpy as jnp
import numpy as np
from jax import lax
from jax.experimental import pallas as pl
from jax.experimental.pallas import tpu as pltpu

F32 = jnp.float32
BF16 = jnp.bfloat16

NORM_EPS = 1e-5

SSD_HEADS = 16
SSD_HEAD_DIM = 64
SSD_INNER = SSD_HEADS * SSD_HEAD_DIM
SSD_GROUPS = 2
SSD_STATE = 128
SSD_CONV = 4
SSD_CHUNK = 128
SSD_BC = 2 * SSD_GROUPS * SSD_STATE
SSD_CONV_DIM = SSD_INNER + SSD_BC

MOBA_HEADS = 8
MOBA_HEAD_DIM = 128
MOBA_INNER = MOBA_HEADS * MOBA_HEAD_DIM
MOBA_BLOCK = 256
MOBA_TOPK = 3

DIL_HEADS = 16
DIL_HEAD_DIM = 128
DIL_INNER = DIL_HEADS * DIL_HEAD_DIM
DIL_QBLOCK = 128
DIL_PATTERNS = ((128, 1), (512, 4), (2048, 16))

LANES = 128
MXU_WIDTH = 256
NEG_BIG = -1e30
VMEM_LIMIT = 56 * 1024 * 1024

PROJ_ROW_SUB = 512
PROJ0_COL_TILES = 2
PROJ1_TN = 4 * MXU_WIDTH
OUT_PROJ_TM = 1024
SSD_CHUNKS_PER_STEP = 4
MOBA_Q_TILE = 256

P0_Z = 0
P0_X = P0_Z + SSD_INNER
P0_G = P0_X + SSD_INNER
P0_Q = P0_G + MOBA_INNER
P0_K = P0_Q + MOBA_INNER
P0_V = P0_K + MOBA_INNER
P0_BC = P0_V + MOBA_INNER
P0_N = P0_BC + SSD_BC


def _nt_dot(a, b):
    return lax.dot_general(a, b, (((1,), (1,)), ((), ())), preferred_element_type=F32)


def _dot(a, b):
    return jnp.dot(a, b, preferred_element_type=F32)


def _sigmoid(x):
    return 1.0 / (1.0 + jnp.exp2(x * -math.log2(math.e)))


def _norm_proj_body(*refs, kind, phases=0, n_token_tiles=0, col_pieces=(), silu_rows=()):
    if kind == "aux":
        x_ref, w_ref, waux_ref, o_ref, oaux_ref, hn_ref = refs
    else:
        x_ref, w_ref, o_ref, rows_ref, hnf_ref = refs
    j = pl.program_id(1)
    tm = x_ref.shape[0]

    if kind == "aux":
        sub = min(tm, PROJ_ROW_SUB)

        def normed(r0):
            x = x_ref[r0:r0 + sub, :]
            ms = jnp.mean(x * x, axis=-1, keepdims=True)
            return (x * lax.rsqrt(ms + NORM_EPS)).astype(BF16)

        starts = list(range(0, tm, sub))

        def column_step(pieces):
            nxt = normed(starts[0])
            for k, r0 in enumerate(starts):
                cur = nxt
                if k + 1 < len(starts):
                    nxt = normed(starts[k + 1])
                hn_ref[r0:r0 + sub, :] = cur
                c0 = 0
                for a, b in pieces:
                    prod = _nt_dot(cur, w_ref[a:b, :])
                    if any(lo <= a and b <= hi for lo, hi in silu_rows):
                        prod = prod * _sigmoid(prod)
                    o_ref[r0:r0 + sub, c0:c0 + b - a] = prod.astype(o_ref.dtype)
                    c0 += b - a

        for step, pieces in enumerate(col_pieces):
            @pl.when(j == step)
            def _(pieces=pieces):
                column_step(pieces)

        @pl.when(j == 0)
        def _():
            oaux_ref[...] = _nt_dot(hn_ref[...], waux_ref[...])

        return

    d = x_ref.shape[1]
    sub = min(tm, PROJ_ROW_SUB)
    starts = list(range(0, tm, sub))
    per = tm // phases
    assert n_token_tiles >= 2 and phases % len(starts) == 0

    def normed(r0):
        x = x_ref[r0:r0 + sub, :]
        ms = jnp.mean(x * x, axis=-1, keepdims=True)
        return x * lax.rsqrt(ms + NORM_EPS)

    def token_rows(side_work):
        nxt = normed(starts[0])
        for k, r0 in enumerate(starts):
            cur = nxt
            if k + 1 < len(starts):
                nxt = normed(starts[k + 1])
            side_work(k, r0, cur)
            o_ref[r0:r0 + sub, :] = _dot(cur.astype(BF16), w_ref[...]).astype(o_ref.dtype)

    def keep(k, r0, hn):
        rows_ref[0, r0:r0 + sub, :] = hn.astype(BF16)
        for cc in range(d // LANES):
            hnf_ref[cc, r0:r0 + sub, :] = hn[:, cc * LANES:(cc + 1) * LANES]

    def build(k):
        for p in range(2 * k, 2 * k + 2):
            for cc in range(d // LANES):
                rows_ref[1, p * per:(p + 1) * per, cc * LANES:(cc + 1) * LANES] = (
                    hnf_ref[cc, pl.ds(p, per, stride=phases), :].astype(BF16))

    assert n_token_tiles == 2 and phases == 2 * len(starts)

    @pl.when(j == 0)
    def _():
        token_rows(keep)

    @pl.when(j == n_token_tiles)
    def _():
        build(0)
        for k, r0 in enumerate(starts):
            if k + 1 < len(starts):
                build(k + 1)
            o_ref[r0:r0 + sub, :] = _dot(rows_ref[1, r0:r0 + sub, :], w_ref[...]).astype(o_ref.dtype)

    @pl.when((j != 0) & (j != n_token_tiles))
    def _():
        order = jnp.where(j < n_token_tiles, 0, 1)
        for r0 in starts:
            o_ref[r0:r0 + sub, :] = _dot(rows_ref[order, r0:r0 + sub, :], w_ref[...]).astype(o_ref.dtype)


def _split_ranges(ranges, width):
    groups, room = [[]], width
    for a, b in ranges:
        while a < b:
            take = min(b - a, room)
            groups[-1].append((a, a + take))
            a, room = a + take, room - take
            if room == 0:
                groups.append([])
                room = width
    assert not groups[-1] and room == width
    return tuple(tuple(g) for g in groups[:-1])


def _norm_proj_t(x2d, w_t, w_aux_t, col_pieces, *, tm, silu_rows=()):
    t, d = x2d.shape
    widths = {sum(b - a for a, b in pieces) for pieces in col_pieces}
    (tn,) = widths
    n = tn * len(col_pieces)
    return pl.pallas_call(
        functools.partial(_norm_proj_body, kind="aux", col_pieces=col_pieces, silu_rows=tuple(silu_rows)),
        grid=(t // tm, len(col_pieces)),
        in_specs=[pl.BlockSpec((tm, d), lambda i, j: (i, 0)),
                  pl.BlockSpec(w_t.shape, lambda i, j: (0, 0), pipeline_mode=pl.Buffered(1)),
                  pl.BlockSpec((LANES, d), lambda i, j: (0, 0))],
        out_specs=[pl.BlockSpec((tm, tn), lambda i, j: (i, j)),
                   pl.BlockSpec((tm, LANES), lambda i, j: (i, 0))],
        out_shape=[jax.ShapeDtypeStruct((t, n), BF16), jax.ShapeDtypeStruct((t, LANES), F32)],
        scratch_shapes=[pltpu.VMEM((tm, d), BF16)],
        compiler_params=pltpu.CompilerParams(
            dimension_semantics=("parallel", "arbitrary"), vmem_limit_bytes=VMEM_LIMIT),
        name="norm_proj_aux",
    )(x2d, w_t, w_aux_t)


def _norm_proj_phased(x2d, w, *, tm, tn, phases, token_cols):
    t, d = x2d.shape
    n = w.shape[1]
    nt, n_token_tiles = n // tn, token_cols // tn

    def col_tile(j):
        return (j + nt - n_token_tiles) % nt

    return pl.pallas_call(
        functools.partial(_norm_proj_body, kind="phased", phases=phases, n_token_tiles=n_token_tiles),
        grid=(t // tm, nt),
        in_specs=[pl.BlockSpec((tm, d), lambda i, j: (i, 0)),
                  pl.BlockSpec((d, tn), lambda i, j: (0, col_tile(j)))],
        out_specs=pl.BlockSpec((tm, tn), lambda i, j: (i, col_tile(j))),
        out_shape=jax.ShapeDtypeStruct((t, n), BF16),
        scratch_shapes=[pltpu.VMEM((2, tm, d), BF16), pltpu.VMEM((d // LANES, tm, LANES), F32)],
        compiler_params=pltpu.CompilerParams(
            dimension_semantics=("parallel", "arbitrary"), vmem_limit_bytes=VMEM_LIMIT),
        name="norm_proj",
    )(x2d, w)


def _split3(a):
    hi = a.astype(BF16)
    r = a - hi.astype(F32)
    mid = r.astype(BF16)
    lo = (r - mid.astype(F32)).astype(BF16)
    return hi, mid, lo


def _expand_heads(a, e):
    hi, mid, lo = _split3(a)
    return _dot(hi, e) + _dot(mid, e) + _dot(lo, e)


def _ssd_body(z_ref, xs_ref, bc_ref, dt_ref, cw_ref, cb_ref, dtb_ref, alog_ref, dskip_ref, nrm_ref,
              e_ref, tri_ref, y_ref, state_ref, ubuf_ref):
    c = pl.program_id(1)
    L = SSD_CHUNK
    tail = 8

    @pl.when(c == 0)
    def _():
        state_ref[...] = jnp.zeros_like(state_ref)
        ubuf_ref[0:tail, :] = jnp.zeros((tail, SSD_CONV_DIM), F32)

    row = lax.broadcasted_iota(jnp.int32, (L, L), 0)
    col = lax.broadcasted_iota(jnp.int32, (L, L), 1)
    causal = col <= row
    lane = lax.broadcasted_iota(jnp.int32, (L, LANES), 1)
    low_half = lane < SSD_HEAD_DIM
    heads_per_group = SSD_HEADS // SSD_GROUPS
    gw = heads_per_group * SSD_HEAD_DIM

    def chunk(r):
        acts = []
        cw = cw_ref[...]
        for c0 in range(0, SSD_CONV_DIM, SSD_BC):
            cols = slice(c0, c0 + SSD_BC)
            src = xs_ref[r, cols] if c0 < SSD_INNER else bc_ref[r, :]
            u = src.astype(F32)
            ubuf_ref[tail:tail + L, cols] = u
            acc = cb_ref[:, cols] + cw[SSD_CONV - 1:SSD_CONV, cols] * u
            for k in range(SSD_CONV - 1):
                acc = acc + cw[k:k + 1, cols] * ubuf_ref[pl.ds(tail - (SSD_CONV - 1) + k, L), cols]
            ubuf_ref[0:tail, cols] = u[L - tail:L, :]
            acts.append(acc * _sigmoid(acc))
        act = jnp.concatenate(acts, axis=1)

        xs = act[:, :SSD_INNER]

        dtr = dt_ref[r, :] + dtb_ref[...]
        dt = jnp.maximum(dtr, 0.0) + jnp.log1p(jnp.exp(-jnp.abs(dtr)))
        a = -jnp.exp(alog_ref[...]) * math.log2(math.e)
        ac = dt * a
        tri = tri_ref[...]
        hi, mid, lo = _split3(ac)
        a_cum = _dot(tri, hi) + _dot(tri, mid) + _dot(tri, lo)
        a_last = a_cum[L - 1:L, :]
        dec_states = jnp.exp2(a_last - a_cum)
        exp_acum = jnp.exp2(a_cum)
        e = e_ref[...]
        dt_e = _expand_heads(dt, e)
        dec_e = _expand_heads(dec_states, e)
        ea_e = _expand_heads(exp_acum, e)
        a_cum_t = a_cum.T

        xc = xs * dt_e
        xc_b = xc.astype(BF16)
        xdec_b = (xc * dec_e).astype(BF16)

        for g in range(SSD_GROUPS):
            b_g = act[:, SSD_INNER + g * SSD_STATE:SSD_INNER + (g + 1) * SSD_STATE]
            c_g = act[:, SSD_INNER + (SSD_GROUPS + g) * SSD_STATE:
                      SSD_INNER + (SSD_GROUPS + g + 1) * SSD_STATE]
            b_gb = b_g.astype(BF16)
            c_gb = c_g.astype(BF16)
            scores = _nt_dot(c_gb, b_gb)
            gs = slice(g * gw, (g + 1) * gw)

            prev = state_ref[:, gs]
            y_off = _dot(c_gb, prev.astype(BF16))
            st_new = _dot(b_g.T.astype(BF16), xdec_b[:, gs])
            state_ref[:, gs] = prev * ea_e[L - 1:L, gs] + st_new

            pieces = []
            for pair in range(heads_per_group // 2):
                ms = []
                for hh in range(2):
                    h = g * heads_per_group + 2 * pair + hh
                    seg = a_cum[:, h:h + 1] - a_cum_t[h:h + 1, :]
                    lmat = jnp.exp2(jnp.where(causal, seg, NEG_BIG))
                    ms.append((scores * lmat).astype(BF16))
                m_pair = jnp.concatenate(ms, axis=1)
                cs = slice(g * gw + pair * LANES, g * gw + (pair + 1) * LANES)
                x_pair = xc_b[:, cs]
                zero = jnp.zeros_like(x_pair)
                rhs = jnp.concatenate([jnp.where(low_half, x_pair, zero),
                                       jnp.where(low_half, zero, x_pair)], axis=0)
                pieces.append(_dot(m_pair, rhs))
            y_diag = jnp.concatenate(pieces, axis=1)

            y = y_diag + y_off * ea_e[:, gs] + xs[:, gs] * dskip_ref[:, gs]
            zg = z_ref[r, gs].astype(F32)
            ug = y * zg
            ug = ug * lax.rsqrt(jnp.mean(ug * ug, axis=-1, keepdims=True) + NORM_EPS)
            y_ref[r, gs] = (ug * nrm_ref[:, gs]).astype(y_ref.dtype)

    for ci in range(xs_ref.shape[0] // L):
        chunk(slice(ci * L, (ci + 1) * L))


def _ssd_operands(proj, dt_raw, conv_w, conv_b, dt_bias, a_log, d_skip, ssd_norm, seq, R):
    L = SSD_CHUNK
    nc = seq // R

    def pad_heads(v):
        return jnp.pad(v.astype(F32), (0, LANES - SSD_HEADS)).reshape(1, LANES)

    head_of_chan = np.arange(SSD_INNER) // SSD_HEAD_DIM
    expand = jnp.asarray(np.arange(LANES)[:, None] == head_of_chan[None, :], BF16)
    tri = jnp.asarray(np.arange(L)[:, None] >= np.arange(L)[None, :], BF16)
    d_chan = jnp.repeat(d_skip.astype(F32), SSD_HEAD_DIM).reshape(1, SSD_INNER)

    def rows(b, c):
        return b * nc + c

    const = lambda b, c: (0, 0)
    in_specs = [
            pl.BlockSpec((R, SSD_INNER), lambda b, c: (rows(b, c), P0_Z // SSD_INNER)),
            pl.BlockSpec((R, SSD_INNER), lambda b, c: (rows(b, c), P0_X // SSD_INNER)),
            pl.BlockSpec((R, SSD_BC), lambda b, c: (rows(b, c), P0_BC // SSD_BC)),
            pl.BlockSpec((R, LANES), lambda b, c: (rows(b, c), 0)),
            pl.BlockSpec((SSD_CONV, SSD_CONV_DIM), const),
            pl.BlockSpec((1, SSD_CONV_DIM), const),
            pl.BlockSpec((1, LANES), const),
            pl.BlockSpec((1, LANES), const),
            pl.BlockSpec((1, SSD_INNER), const),
            pl.BlockSpec((1, SSD_INNER), const),
            pl.BlockSpec((LANES, SSD_INNER), const),
            pl.BlockSpec((L, L), const),
    ]
    args = (proj, proj, proj, dt_raw, conv_w.astype(F32), conv_b.reshape(1, -1).astype(F32),
            pad_heads(dt_bias), pad_heads(a_log), d_chan, ssd_norm.reshape(1, -1).astype(F32), expand, tri)
    return in_specs, args


def _moba_body(q_ref, k_ref, v_ref, g_ref, avg_ref, hot_ref, eye_ref, o_ref, *, nb, q_tile):
    blk = MOBA_BLOCK
    n_dense = MOBA_TOPK + 1

    kmean = _dot(avg_ref[...], k_ref[...]).astype(BF16)
    jrow = lax.broadcasted_iota(jnp.int32, (nb, blk), 0)
    row = lax.broadcasted_iota(jnp.int32, (q_tile, blk), 0)
    col = lax.broadcasted_iota(jnp.int32, (q_tile, blk), 1)

    def query_block(i):
        q_i = q_ref[i * blk:(i + 1) * blk, :]
        if i < n_dense:
            return q_i, None
        gate_t = _nt_dot(kmean, q_i)[0:nb, :]
        valid = jrow < i
        gm = jnp.where(valid, gate_t, -jnp.inf)
        rank = jnp.zeros((nb, blk), F32)
        for r in range(1, nb):
            gr = pltpu.roll(gm, r, axis=0)
            lower = ((jrow - r) & (nb - 1)) < jrow
            rank = rank + jnp.where(gr > gm, 1.0, 0.0) + jnp.where((gr == gm) & lower, 1.0, 0.0)
        keep = valid & (rank < float(MOBA_TOPK))
        bias_t = jnp.where(keep, 0.0, NEG_BIG)
        bias_t = jnp.concatenate([bias_t, jnp.zeros((LANES - nb, blk), F32)], axis=0).astype(BF16)
        bias = _nt_dot(eye_ref[...], bias_t).astype(BF16)
        return q_i, jnp.concatenate([q_i, bias], axis=1)

    def scores(i, h, q_i, q_aug):
        rs = slice(h * q_tile, (h + 1) * q_tile)
        s_own = _nt_dot(q_i[rs, :], k_ref[i * blk:(i + 1) * blk, :])
        s_own = jnp.where(col <= row + h * q_tile, s_own, NEG_BIG)
        if i == 0:
            return s_own, None
        if q_aug is None:
            return s_own, _nt_dot(q_i[rs, :], k_ref[0:i * blk, :])
        k_past = jnp.concatenate([k_ref[0:i * blk, :], hot_ref[0:i * blk, :]], axis=1)
        return s_own, _nt_dot(q_aug[rs, :], k_past)

    def finish(i, h, s_own, s_past):
        m = jnp.max(s_own, axis=-1, keepdims=True)
        if s_past is None:
            p = jnp.exp2(s_own - m)
        else:
            m = jnp.maximum(m, jnp.max(s_past, axis=-1, keepdims=True))
            p = jnp.concatenate([jnp.exp2(s_past - m), jnp.exp2(s_own - m)], axis=1)
        vb = v_ref[0:(i + 1) * blk, :]
        pv = _dot(p.astype(BF16), jnp.concatenate([vb, jnp.ones_like(vb)], axis=1))
        d = vb.shape[1]
        rows = slice(i * blk + h * q_tile, i * blk + (h + 1) * q_tile)
        g = g_ref[rows, :].astype(F32)
        o_ref[rows, :] = ((pv[:, :d] / pv[:, d:]) * g).astype(o_ref.dtype)

    tiles = [(i, h) for i in range(nb) for h in range(blk // q_tile)]
    qcache = {}
    pending = None
    for i, h in tiles:
        if i not in qcache:
            qcache = {i: query_block(i)}
        s_next = scores(i, h, *qcache[i])
        if pending is not None:
            finish(*pending)
        pending = (i, h) + s_next
    finish(*pending)


def _moba_operands(proj, seq):
    blk = MOBA_BLOCK
    nb = seq // blk
    d = MOBA_HEAD_DIM
    blk_of_key = np.arange(seq) // blk
    avg = jnp.asarray((np.arange(2 * nb)[:, None] == blk_of_key[None, :]) / blk, BF16)
    hot = jnp.asarray(blk_of_key[:, None] == np.arange(LANES)[None, :], BF16)
    eye = jnp.asarray(np.eye(blk), BF16)

    def head_block(base):
        return pl.BlockSpec((seq, d), lambda b, h: (b, base // d + h))

    const = lambda b, h: (0, 0)
    in_specs = [head_block(P0_Q), head_block(P0_K), head_block(P0_V), head_block(P0_G),
                pl.BlockSpec((2 * nb, seq), const), pl.BlockSpec((seq, LANES), const),
                pl.BlockSpec((blk, blk), const)]
    return in_specs, (proj, proj, proj, proj, avg, hot, eye)


def _ssd(proj, dt_raw, conv_w, conv_b, dt_bias, a_log, d_skip, ssd_norm, bsz, seq, *, chunks_per_step=SSD_CHUNKS_PER_STEP):
    t = bsz * seq
    R = chunks_per_step * SSD_CHUNK
    in_specs, args = _ssd_operands(proj, dt_raw, conv_w, conv_b, dt_bias, a_log, d_skip, ssd_norm, seq, R)
    nc = seq // R
    return pl.pallas_call(
        _ssd_body,
        grid=(bsz, nc),
        in_specs=in_specs,
        out_specs=pl.BlockSpec((R, SSD_INNER), lambda b, c: (b * nc + c, 0)),
        out_shape=jax.ShapeDtypeStruct((t, SSD_INNER), BF16),
        scratch_shapes=[pltpu.VMEM((SSD_STATE, SSD_INNER), F32),
                        pltpu.VMEM((8 + SSD_CHUNK, SSD_CONV_DIM), F32)],
        compiler_params=pltpu.CompilerParams(
            dimension_semantics=("parallel", "arbitrary"), vmem_limit_bytes=VMEM_LIMIT),
        name="ssd_mixer",
    )(*args)


def _moba(proj, bsz, seq):
    t = bsz * seq
    in_specs, args = _moba_operands(proj, seq)
    return pl.pallas_call(
        functools.partial(_moba_body, nb=seq // MOBA_BLOCK, q_tile=MOBA_Q_TILE),
        grid=(bsz, MOBA_HEADS),
        in_specs=in_specs,
        out_specs=pl.BlockSpec((seq, MOBA_HEAD_DIM), lambda b, h: (b, h)),
        out_shape=jax.ShapeDtypeStruct((t, MOBA_INNER), BF16),
        compiler_params=pltpu.CompilerParams(
            dimension_semantics=("parallel", "parallel"), vmem_limit_bytes=VMEM_LIMIT),
        name="moba_attn",
    )(*args)


def _out_proj_body(*refs, n_plain, has_gate, final_norm):
    plain = refs[:n_plain]
    n_act = n_plain + (2 if has_gate else 0)
    x_ref, wf_ref = refs[n_act:n_act + 2]
    o_ref, w_ref = refs[-2:]

    @pl.when(pl.program_id(0) == 0)
    def _():
        w_ref[...] = wf_ref[...].astype(BF16)

    k0 = 0
    y = x_ref[...]
    for r in plain:
        kw = r.shape[1]
        y = y + _dot(r[...], w_ref[k0:k0 + kw, :])
        k0 += kw
    if has_gate:
        a_ref, g_ref = refs[n_plain:n_act]
        g = g_ref[...].astype(F32)
        gated = (a_ref[...].astype(F32) * (g * _sigmoid(g))).astype(BF16)
        y = y + _dot(gated, w_ref[k0:k0 + gated.shape[1], :])
    if final_norm:
        fn_ref = refs[-3]
        y = y * lax.rsqrt(jnp.mean(y * y, axis=-1, keepdims=True) + NORM_EPS) * fn_ref[...]
    o_ref[...] = y


def _out_proj(plain, gated, gate_src, gate_col, x2d, w, final_gain=None, *, tm=OUT_PROJ_TM):
    t, d = x2d.shape
    in_specs = [pl.BlockSpec((tm, a.shape[1]), lambda i: (i, 0)) for a in plain]
    args = list(plain)
    if gated is not None:
        kg = gated.shape[1]
        in_specs += [pl.BlockSpec((tm, kg), lambda i: (i, 0)),
                     pl.BlockSpec((tm, kg), lambda i: (i, gate_col // kg))]
        args += [gated, gate_src]
    in_specs += [pl.BlockSpec((tm, d), lambda i: (i, 0)),
                 pl.BlockSpec(w.shape, lambda i: (0, 0), pipeline_mode=pl.Buffered(1))]
    args += [x2d, w.astype(F32)]
    if final_gain is not None:
        in_specs.append(pl.BlockSpec((1, d), lambda i: (0, 0)))
        args.append(final_gain.reshape(1, d).astype(F32))
    return pl.pallas_call(
        functools.partial(_out_proj_body, n_plain=len(plain), has_gate=gated is not None,
                          final_norm=final_gain is not None),
        grid=(t // tm,),
        in_specs=in_specs,
        out_specs=pl.BlockSpec((tm, d), lambda i: (i, 0)),
        out_shape=jax.ShapeDtypeStruct((t, d), F32),
        scratch_shapes=[pltpu.VMEM(w.shape, BF16)],
        compiler_params=pltpu.CompilerParams(
            dimension_semantics=("arbitrary",), vmem_limit_bytes=VMEM_LIMIT),
        name="out_proj_final" if final_gain is not None else "out_proj",
    )(*args)


DIL_PHASES = 8


def _dil_tiles(seq):
    ph, qb = DIL_PHASES, DIL_QBLOCK
    per = seq // ph
    assert [r for _, r in DIL_PATTERNS] == [1, 4, 16] and all(w // r == qb for w, r in DIL_PATTERNS)
    assert seq == DIL_PATTERNS[-1][0]
    tiles = []
    rows = qb // ph
    for n in range(seq // qb):
        qs = [(p * per + rows * n, rows) for p in range(ph)]
        if n == 0:
            ks, mask = [(p * per, rows) for p in range(ph)], "a_first"
        else:
            ks, mask = [(p * per + rows * (n - 1), 2 * rows) for p in range(ph)], "a"
        tiles.append((qs, ks, mask, "init"))
    seg = ph // 4
    rows = qb // seg
    for p4 in range(4):
        for n in range(seq // 4 // qb):
            qs = [((p4 + 4 * j) * per + rows * n, rows) for j in range(seg)]
            if n == 0:
                ks, mask = [((p4 + 4 * j) * per, rows) for j in range(seg)], "b_first"
            else:
                ks, mask = [((p4 + 4 * j) * per + rows * (n - 1), 2 * rows) for j in range(seg)], "b"
            tiles.append((qs, ks, mask, "merge"))
    for p in range(ph):
        for h in range(per // qb):
            qs = [(p * per + qb * h, qb)]
            ks = [(p * per, qb * (h + 1))]
            tiles.append((qs, ks, "c%d" % h, "final"))
    return tiles


def _dil_masks(seq):
    ph, qb = DIL_PHASES, DIL_QBLOCK
    per = seq // ph

    def pos(n_rows, seg_rows, step):
        i = np.arange(n_rows)
        return step * (i % seg_rows) + i // seg_rows

    def band(qpos, kpos, span):
        dist = qpos[:, None] - kpos[None, :]
        return np.where((dist >= 0) & (dist <= span), 0.0, NEG_BIG).astype(np.float32)

    masks = {}
    for name, seg in (("a", ph), ("b", ph // 4)):
        rows = qb // seg
        masks[name] = band(qb + pos(qb, rows, seg), pos(2 * qb, 2 * rows, seg), qb)
        masks[name + "_first"] = band(pos(qb, rows, seg), pos(qb, rows, seg), qb)
    for h in range(per // qb):
        lq = qb * h + np.arange(qb)
        lk = np.arange(qb * (h + 1))
        same = (lq[:, None] - lk[None, :]) % 2 == 0
        masks["c%d" % h] = np.where(same & (lk[None, :] <= lq[:, None]), 0.0, NEG_BIG).astype(np.float32)
    return masks


def _dil_body(*refs, seq, mask_names):
    q_ref, k_ref, v_ref = refs[:3]
    mask_refs = dict(zip(mask_names, refs[3:3 + len(mask_names)]))
    o_ref, onat_ref = refs[3 + len(mask_names):][:2]
    state_a, state_b = refs[-6:-3], refs[-3:]
    state_in = {"merge": state_a, "final": state_b}
    state_out = {"init": state_a, "merge": state_b}
    qb = DIL_QBLOCK
    d = DIL_HEAD_DIM

    def rows_of(ref, slices):
        parts = [ref[s:s + n, :] for s, n in slices]
        return parts[0] if len(parts) == 1 else jnp.concatenate(parts, axis=0)

    def put_rows(ref, slices, val):
        r0 = 0
        for s, n in slices:
            ref[s:s + n, :] = val[r0:r0 + n, :]
            r0 += n

    def scores(tile):
        qs, ks, mask, _ = tile
        return _nt_dot(rows_of(q_ref, qs), rows_of(k_ref, ks)) + mask_refs[mask][...]

    def probs(tile, s):
        qs, _, _, stage = tile
        m_new = jnp.broadcast_to(jnp.max(s, axis=-1, keepdims=True), (s.shape[0], d))
        m_old = None
        if stage != "init":
            m_old = rows_of(state_in[stage][1], qs)
            m_new = jnp.maximum(m_old, m_new)
        p = jnp.concatenate([jnp.exp2(s[:, c0:c0 + d] - m_new) for c0 in range(0, s.shape[1], d)], axis=1)
        return p.astype(BF16), m_new, m_old

    def accumulate(tile, p, m_new, m_old):
        qs, ks, _, stage = tile
        vb = rows_of(v_ref, ks)
        pv = _dot(p, jnp.concatenate([vb, jnp.ones_like(vb)], axis=1))
        acc, l = pv[:, :d], pv[:, d:]
        if stage != "init":
            alpha = jnp.exp2(m_old - m_new)
            acc_ref, _, l_ref = state_in[stage]
            acc = alpha * rows_of(acc_ref, qs) + acc
            l = alpha * rows_of(l_ref, qs) + l
        if stage == "final":
            (s0, _), = qs
            p_idx, l0 = divmod(s0, seq // DIL_PHASES)
            onat_ref[pl.ds(DIL_PHASES * l0 + p_idx, qb, stride=DIL_PHASES), :] = acc / l
        else:
            acc_out, m_out, l_out = state_out[stage]
            put_rows(acc_out, qs, acc)
            put_rows(l_out, qs, l)
            put_rows(m_out, qs, m_new)

    tiles = _dil_tiles(seq)

    def owned(tile):
        return {r for s0, n in tile[0] for r in range(s0, s0 + n)}

    for ta, tb in zip(tiles, tiles[1:]):
        assert not owned(ta) & owned(tb)
    scored, ready = None, None
    for tile in tiles + [None, None]:
        s_new = None if tile is None else (tile, scores(tile))
        p_new = None if scored is None else (scored[0],) + probs(*scored)
        if ready is not None:
            accumulate(*ready)
        scored, ready = s_new, p_new
    o_ref[...] = onat_ref[...].astype(o_ref.dtype)


def _dilated(proj, bsz, seq):
    t = bsz * seq
    d = DIL_HEAD_DIM
    masks = _dil_masks(seq)
    names = tuple(sorted(masks))

    def head_block(base):
        return pl.BlockSpec((seq, d), lambda b, h: (b, base // d + h))

    return pl.pallas_call(
        functools.partial(_dil_body, seq=seq, mask_names=names),
        grid=(bsz, DIL_HEADS),
        in_specs=[head_block(0), head_block(DIL_INNER), head_block(2 * DIL_INNER)]
        + [pl.BlockSpec(masks[n].shape, lambda b, h: (0, 0)) for n in names],
        out_specs=pl.BlockSpec((seq, d), lambda b, h: (b, h)),
        out_shape=jax.ShapeDtypeStruct((t, DIL_INNER), BF16),
        scratch_shapes=[pltpu.VMEM((seq, d), F32)] * 7,
        compiler_params=pltpu.CompilerParams(
            dimension_semantics=("parallel", "parallel"), vmem_limit_bytes=VMEM_LIMIT),
        name="dilated_attn",
    )(proj, proj, proj, *[jnp.asarray(masks[n]) for n in names])


def kernel(x, even_norm, even_w_in, ssd_conv_w, ssd_conv_b, ssd_dt_bias, ssd_a_log, ssd_d, ssd_norm,
           even_w_out, odd_norm, odd_w_in, odd_w_out, final_norm):
    bsz, seq, d = x.shape
    t = bsz * seq
    x2d = x.reshape(t, d)

    w = even_w_in[0]
    z_end = SSD_INNER
    xbc_end = z_end + SSD_CONV_DIM
    dt_end = xbc_end + SSD_HEADS
    q_end = dt_end + MOBA_INNER
    v_end = q_end + 2 * MOBA_INNER
    x_end = z_end + SSD_INNER
    moba_scale = MOBA_HEAD_DIM ** -0.5 * math.log2(math.e)
    row_scale = jnp.ones((w.shape[1], 1), F32).at[dt_end:q_end].set(moba_scale)
    g0 = even_norm[0].astype(F32)[None, :]
    w_t = (w.T * row_scale * g0).astype(BF16)
    w_dt = jnp.pad((w.T[xbc_end:dt_end] * g0).astype(BF16), ((0, LANES - SSD_HEADS), (0, 0)))
    gates = [(0, z_end), (v_end, w.shape[1])]
    col_pieces = _split_ranges([(0, z_end), (z_end, x_end), (v_end, w.shape[1]), (dt_end, v_end),
                                (x_end, xbc_end)], P0_N // PROJ0_COL_TILES)
    proj0, dt_raw = _norm_proj_t(x2d, w_t, w_dt, col_pieces, tm=seq // 2, silu_rows=gates)
    y_ssd = _ssd(proj0, dt_raw, ssd_conv_w[0], ssd_conv_b[0], ssd_dt_bias[0], ssd_a_log[0],
                 ssd_d[0], ssd_norm[0], bsz, seq)
    o_moba = _moba(proj0, bsz, seq)
    x1 = _out_proj([y_ssd, o_moba], None, None, 0, x2d, even_w_out[0])

    w1 = odd_w_in[0]
    dil_scale = DIL_HEAD_DIM ** -0.5 * math.log2(math.e)
    col_scale1 = jnp.ones((1, w1.shape[1]), F32).at[:, :DIL_INNER].set(dil_scale)
    w1_b = (w1 * col_scale1 * odd_norm[0].astype(F32)[:, None]).astype(BF16)
    proj1 = _norm_proj_phased(x1, w1_b, tm=seq, tn=PROJ1_TN, phases=DIL_PHASES, token_cols=DIL_INNER)
    o_dil = _dilated(proj1, bsz, seq)
    out = _out_proj([], o_dil, proj1, 3 * DIL_INNER, x1, odd_w_out[0], final_norm)
    return out.reshape(bsz, seq, d)
```
